```python
import math
import jax, jax.numpy as jnp
from jax import lax
import numpy as np

D_MODEL = 2048
BATCH = 4
SEQ = 4096
DEPTH = 4

N_BRANCH = 3
BRANCH_WIDTH = D_MODEL // 2
S5_GROUP = 16
S5_GROUPS = BRANCH_WIDTH // S5_GROUP
S5_STATE = 64
S5_DT_MIN = 1e-3
S5_DT_MAX = 1e-1
S5_MIN_DECAY = 1e-4
GLA_HEADS = 4
GLA_DV = BRANCH_WIDTH // GLA_HEADS
GLA_DK = GLA_DV // 2
GLA_KEY = GLA_HEADS * GLA_DK
GLA_GATE_RANK = 16
GLA_GATE_TAU = 16.0
HGRN_EXPAND = 128
HGRN_HEADS = BRANCH_WIDTH // HGRN_EXPAND
HGRN_DV = BRANCH_WIDTH // HGRN_HEADS
HGRN_KEY = HGRN_HEADS * HGRN_EXPAND
CHUNK = 64
SUB_CHUNK = 16
MLP_HIDDEN = 4 * D_MODEL
DN_ALPHA = (2 * DEPTH) ** 0.25
DN_BETA = (8 * DEPTH) ** -0.25
LN_EPS = 1e-5
RMS_EPS = 1e-6

IN_WIDTHS = (BRANCH_WIDTH,
             GLA_KEY, GLA_KEY, BRANCH_WIDTH, GLA_GATE_RANK, BRANCH_WIDTH,
             HGRN_KEY, HGRN_KEY, BRANCH_WIDTH, BRANCH_WIDTH,
             N_BRANCH * D_MODEL)
IN_TOTAL = sum(IN_WIDTHS)

kernel_name = 'hybrid_s5_gla_hgrn2_deepnorm'


def layer_norm(x, g, b):
    xf = x.astype(jnp.float32)
    mu = jnp.mean(xf, axis=-1, keepdims=True)
    var = jnp.mean(jnp.square(xf - mu), axis=-1, keepdims=True)
    return ((xf - mu) * lax.rsqrt(var + LN_EPS) * g + b).astype(x.dtype)


def head_rms_norm(o, w):
    of = o.astype(jnp.float32)
    y = of * lax.rsqrt(jnp.mean(jnp.square(of), axis=-1, keepdims=True) + RMS_EPS) * w
    return y.astype(o.dtype)


def split_columns(h):
    bounds, acc = [], 0
    for w in IN_WIDTHS[:-1]:
        acc += w
        bounds.append(acc)
    return jnp.split(h, bounds, axis=-1)


def chunk_gated_linear_attention(q, k, v, log_g):
    bsz, seq, nh, dk = q.shape
    dv = v.shape[-1]
    n_chunks = seq // CHUNK
    n_sub = CHUNK // SUB_CHUNK

    def to_chunks(t):
        return jnp.moveaxis(t.astype(jnp.float32).reshape(bsz, n_chunks, CHUNK, nh, t.shape[-1]), 1, 0)

    later = (jnp.arange(n_sub)[:, None] > jnp.arange(n_sub)[None, :])[:, :, None, None, None]
    causal = (jnp.arange(SUB_CHUNK)[:, None] >= jnp.arange(SUB_CHUNK)[None, :])[:, :, None, None]

    def step(state, inp):
        qc, kc, vc, gc = inp
        gcum = jnp.cumsum(gc, axis=1)
        g_last = gcum[:, -1]
        o = jnp.einsum('blhk,bhkv->blhv', qc * jnp.exp(gcum), state)
        qs = qc.reshape(bsz, n_sub, SUB_CHUNK, nh, dk)
        ks = kc.reshape(bsz, n_sub, SUB_CHUNK, nh, dk)
        vs = vc.reshape(bsz, n_sub, SUB_CHUNK, nh, dv)
        gs = gcum.reshape(bsz, n_sub, SUB_CHUNK, nh, dk)
        g_start = jnp.concatenate([jnp.zeros_like(gs[:, :1, 0]), gs[:, :-1, -1]], axis=1)
        q_ref = qs * jnp.exp(gs - g_start[:, :, None])
        e_off = jnp.where(later, g_start[:, :, None, None] - gs[:, None], -jnp.inf)
        s_off = jnp.einsum('bpihk,bprjhk->bhpirj', q_ref, ks[:, None] * jnp.exp(e_off))
        o_off = jnp.einsum('bhpirj,brjhv->bpihv', s_off, vs)
        e_diag = jnp.where(causal, gs[:, :, :, None] - gs[:, :, None], -jnp.inf)
        s_diag = jnp.einsum('bpihk,bpijhk,bpjhk->bhpij', qs, jnp.exp(e_diag), ks)
        o_diag = jnp.einsum('bhpij,bpjhv->bpihv', s_diag, vs)
        o = o + (o_off + o_diag).reshape(bsz, CHUNK, nh, dv)
        new_state = jnp.exp(g_last)[..., None] * state + jnp.einsum(
            'blhk,blhv->bhkv', kc * jnp.exp(g_last[:, None] - gcum), vc)
        return new_state, o

    init = jnp.zeros((bsz, nh, dk, dv), jnp.float32)
    _, o = lax.scan(step, init, (to_chunks(q), to_chunks(k), to_chunks(v), to_chunks(log_g)))
    return jnp.moveaxis(o, 0, 1).reshape(bsz, seq, nh, dv).astype(v.dtype)


def _complex_scan_combine(left, right):
    a1r, a1i, b1r, b1i = left
    a2r, a2i, b2r, b2i = right
    return (a2r * a1r - a2i * a1i,
            a2r * a1i + a2i * a1r,
            a2r * b1r - a2i * b1i + b2r,
            a2r * b1i + a2i * b1r + b2i)


def s5_branch(u, lam_re, lam_im, log_dt, b_re, b_im, c_re, c_im, d_skip, w_glu, b_glu):
    bsz, seq, _ = u.shape
    f32 = jnp.float32
    ug = u.astype(f32).reshape(bsz, seq, S5_GROUPS, S5_GROUP)
    lr = jnp.minimum(lam_re.astype(f32), -S5_MIN_DECAY)
    li = lam_im.astype(f32)
    dt = jnp.exp(log_dt.astype(f32))[:, None]
    mag = jnp.exp(lr * dt)
    abar_re, abar_im = mag * jnp.cos(li * dt), mag * jnp.sin(li * dt)
    den = lr * lr + li * li
    fac_re = ((abar_re - 1.0) * lr + abar_im * li) / den
    fac_im = (abar_im * lr - (abar_re - 1.0) * li) / den
    br, bi = b_re.astype(f32), b_im.astype(f32)
    bbar_re = fac_re[..., None] * br - fac_im[..., None] * bi
    bbar_im = fac_re[..., None] * bi + fac_im[..., None] * br
    bu_re = jnp.einsum('bsgc,gpc->bsgp', ug, bbar_re)
    bu_im = jnp.einsum('bsgc,gpc->bsgp', ug, bbar_im)
    a_re = jnp.broadcast_to(abar_re[None, None], (1, seq, S5_GROUPS, S5_STATE))
    a_im = jnp.broadcast_to(abar_im[None, None], (1, seq, S5_GROUPS, S5_STATE))
    _, _, s_re, s_im = lax.associative_scan(_complex_scan_combine, (a_re, a_im, bu_re, bu_im), axis=1)
    y = (jnp.einsum('bsgp,gcp->bsgc', s_re, c_re.astype(f32))
         - jnp.einsum('bsgp,gcp->bsgc', s_im, c_im.astype(f32))
         + d_skip.astype(f32).reshape(S5_GROUPS, S5_GROUP) * ug)
    z = jax.nn.gelu(y.reshape(bsz, seq, BRANCH_WIDTH).astype(u.dtype))
    return z * jax.nn.sigmoid(z @ w_glu + b_glu)


def gla_branch(q, k, v, g_low, gate, w_gate, b_gate, norm_w):
    bsz, seq, _ = q.shape
    shp_k = (bsz, seq, GLA_HEADS, GLA_DK)
    shp_v = (bsz, seq, GLA_HEADS, GLA_DV)
    log_a = jax.nn.log_sigmoid((g_low @ w_gate + b_gate).astype(jnp.float32)) / GLA_GATE_TAU
    o = chunk_gated_linear_attention((q * GLA_DK ** -0.5).reshape(shp_k), k.reshape(shp_k),
                                     v.reshape(shp_v), log_a.reshape(shp_k))
    o = head_rms_norm(o, norm_w) * jax.nn.silu(gate).reshape(shp_v)
    return o.reshape(bsz, seq, BRANCH_WIDTH)


def hgrn2_branch(q, f_logit, i, gate, lb, norm_w):
    bsz, seq, _ = q.shape
    shp_k = (bsz, seq, HGRN_HEADS, HGRN_EXPAND)
    shp_v = (bsz, seq, HGRN_HEADS, HGRN_DV)
    f = (lb + (1.0 - lb) * jax.nn.sigmoid(f_logit.astype(jnp.float32))).reshape(shp_k)
    o = chunk_gated_linear_attention(jax.nn.silu(q).reshape(shp_k), 1.0 - f,
                                     i.reshape(shp_v), jnp.log(f))
    o = head_rms_norm(o * jax.nn.sigmoid(gate).reshape(shp_v), norm_w)
    return o.reshape(bsz, seq, BRANCH_WIDTH)


def hybrid_mixer(x, w_in, s5_lam_re, s5_lam_im, s5_log_dt, s5_b_re, s5_b_im, s5_c_re, s5_c_im,
                 s5_d, s5_w_glu, s5_b_glu, gla_w_gate, gla_b_gate, gla_norm_w, lb, hgrn_norm_w,
                 w_up, w_out):
    bsz, seq, _ = x.shape
    h = jnp.einsum('bsd,dn->bsn', x, w_in)
    (u_a, q_b, k_b, v_b, glow_b, gate_b, q_c, f_c, i_c, gate_c, merge_gates) = split_columns(h)
    y_a = s5_branch(u_a, s5_lam_re, s5_lam_im, s5_log_dt, s5_b_re, s5_b_im, s5_c_re, s5_c_im,
                    s5_d, s5_w_glu, s5_b_glu)
    y_b = gla_branch(q_b, k_b, v_b, glow_b, gate_b, gla_w_gate, gla_b_gate, gla_norm_w)
    y_c = hgrn2_branch(q_c, f_c, i_c, gate_c, lb, hgrn_norm_w)
    ys = jnp.stack([y_a, y_b, y_c], axis=2)
    up = jnp.einsum('bsnw,nwd->bsnd', ys, w_up)
    g = jax.nn.sigmoid(merge_gates.reshape(bsz, seq, N_BRANCH, D_MODEL))
    merged = jnp.sum(g * up, axis=2)
    return merged @ w_out


def squared_relu_mlp(x, w1, w2):
    return jnp.square(jax.nn.relu(x @ w1)) @ w2


def setup_inputs(seed: int = 0) -> dict:
    key = jax.random.key(seed)
    ks = jax.random.split(key, 25)
    f32 = jnp.float32
    L, D, W, G, P, C = DEPTH, D_MODEL, BRANCH_WIDTH, S5_GROUPS, S5_STATE, S5_GROUP

    def nrm(k, shape, scale):
        return jax.random.normal(k, shape, f32) * scale

    return {
        'x': nrm(ks[0], (BATCH, SEQ, D), 1.0),
        'w_in': nrm(ks[1], (L, D, IN_TOTAL), D ** -0.5),
        's5_lam_re': -0.5 + nrm(ks[2], (L, G, P), 0.01),
        's5_lam_im': jnp.pi * jnp.arange(P, dtype=f32) + nrm(ks[3], (L, G, P), 0.01),
        's5_log_dt': jax.random.uniform(ks[4], (L, G), f32, math.log(S5_DT_MIN), math.log(S5_DT_MAX)),
        's5_b_re': nrm(ks[5], (L, G, P, C), C ** -0.5),
        's5_b_im': nrm(ks[6], (L, G, P, C), C ** -0.5),
        's5_c_re': nrm(ks[7], (L, G, C, P), P ** -0.5),
        's5_c_im': nrm(ks[8], (L, G, C, P), P ** -0.5),
        's5_d': nrm(ks[9], (L, W), 1.0),
        's5_w_glu': nrm(ks[10], (L, W, W), W ** -0.5),
        's5_b_glu': nrm(ks[11], (L, W), 0.01),
        'gla_w_gate': nrm(ks[12], (L, GLA_GATE_RANK, GLA_KEY), GLA_GATE_RANK ** -0.5),
        'gla_b_gate': nrm(ks[13], (L, GLA_KEY), 0.01),
        'gla_norm_w': 1.0 + nrm(ks[14], (L, GLA_DV), 0.02),
        'hgrn_lb_logits': nrm(ks[15], (L, HGRN_KEY), 0.1),
        'hgrn_norm_w': 1.0 + nrm(ks[16], (L, HGRN_DV), 0.02),
        'w_up': nrm(ks[17], (L, N_BRANCH, W, D), W ** -0.5),
        'w_out': nrm(ks[18], (L, D, D), D ** -0.5 * DN_BETA),
        'ln1_g': 1.0 + nrm(ks[19], (L, D), 0.02),
        'ln1_b': nrm(ks[20], (L, D), 0.01),
        'ln2_g': 1.0 + nrm(ks[21], (L, D), 0.02),
        'ln2_b': nrm(ks[22], (L, D), 0.01),
        'w_mlp_in': nrm(ks[23], (L, D, MLP_HIDDEN), D ** -0.5),
        'w_mlp_out': nrm(ks[24], (L, MLP_HIDDEN, D), MLP_HIDDEN ** -0.5 * DN_BETA),
    }


def reference(x, w_in, s5_lam_re, s5_lam_im, s5_log_dt, s5_b_re, s5_b_im, s5_c_re, s5_c_im, s5_d,
              s5_w_glu, s5_b_glu, gla_w_gate, gla_b_gate, gla_norm_w, hgrn_lb_logits, hgrn_norm_w,
              w_up, w_out, ln1_g, ln1_b, ln2_g, ln2_b, w_mlp_in, w_mlp_out):
    p = jax.nn.softmax(hgrn_lb_logits.astype(jnp.float32), axis=0)
    lb = jnp.cumsum(p, axis=0) - p[0]
    for l in range(DEPTH):
        mix = hybrid_mixer(x, w_in[l], s5_lam_re[l], s5_lam_im[l], s5_log_dt[l], s5_b_re[l], s5_b_im[l],
                           s5_c_re[l], s5_c_im[l], s5_d[l], s5_w_glu[l], s5_b_glu[l], gla_w_gate[l],
                           gla_b_gate[l], gla_norm_w[l], lb[l], hgrn_norm_w[l], w_up[l], w_out[l])
        x = layer_norm(DN_ALPHA * x + mix, ln1_g[l], ln1_b[l])
        x = layer_norm(DN_ALPHA * x + squared_relu_mlp(x, w_mlp_in[l], w_mlp_out[l]), ln2_g[l], ln2_b[l])
    return x
```

```python
import functools
import math

import jax
import jax.numpy as jnp
from jax import lax
from jax.experimental import pallas as pl
from jax.experimental.pallas import tpu as pltpu

F32 = jnp.float32
BF16 = jnp.bfloat16

D_MODEL = 2048
DEPTH = 4
N_BRANCH = 3
BRANCH_WIDTH = D_MODEL // 2
S5_GROUP = 16
S5_GROUPS = BRANCH_WIDTH // S5_GROUP
S5_STATE = 64
S5_MIN_DECAY = 1e-4
GLA_HEADS = 4
GLA_DV = BRANCH_WIDTH // GLA_HEADS
GLA_DK = GLA_DV // 2
GLA_KEY = GLA_HEADS * GLA_DK
GLA_GATE_RANK = 16
GLA_GATE_TAU = 16.0
HGRN_EXPAND = 128
HGRN_HEADS = BRANCH_WIDTH // HGRN_EXPAND
HGRN_DV = BRANCH_WIDTH // HGRN_HEADS
HGRN_KEY = HGRN_HEADS * HGRN_EXPAND
MLP_HIDDEN = 4 * D_MODEL
DN_ALPHA = (2 * DEPTH) ** 0.25
LN_EPS = 1e-5
RMS_EPS = 1e-6

LANES = 128
VMEM_LIMIT_BYTES = 56 * 1024 * 1024

COL_U = 0
COL_QB = COL_U + BRANCH_WIDTH
COL_KB = COL_QB + GLA_KEY
COL_VB = COL_KB + GLA_KEY
COL_GATEB = COL_VB + BRANCH_WIDTH
COL_QC = COL_GATEB + BRANCH_WIDTH
COL_FC = COL_QC + HGRN_KEY
COL_IC = COL_FC + HGRN_KEY
COL_GATEC = COL_IC + BRANCH_WIDTH
COL_MERGE = COL_GATEC + BRANCH_WIDTH
COL_GLOW = COL_MERGE + N_BRANCH * D_MODEL
PROJ_TN = 512
PROJ_WIDTH = COL_GLOW + PROJ_TN

S5_FOLD = 16
ATT_CHUNK = 128


def _cparams(sem):
    return pltpu.CompilerParams(dimension_semantics=sem, vmem_limit_bytes=VMEM_LIMIT_BYTES)


def _dot(a, b):
    return jnp.dot(a, b, preferred_element_type=F32)


def _dot_nt(a, b):
    return lax.dot_general(a, b, (((1,), (1,)), ((), ())), preferred_element_type=F32)


def _dot_tn(a, b):
    return lax.dot_general(a, b, (((0,), (0,)), ((), ())), preferred_element_type=F32)


def _sigmoid(x):
    return 1.0 / (1.0 + jnp.exp(-x))


def _layer_norm(v, g, b):
    mu = jnp.mean(v, axis=-1, keepdims=True)
    xc = v - mu
    var = jnp.mean(xc * xc, axis=-1, keepdims=True)
    return xc * lax.rsqrt(var + LN_EPS) * g + b


def _proj_kernel(x_ref, w_ref, o_ref):
    o_ref[...] = _dot(x_ref[...], w_ref[...])


def _proj(xb, w, tm):
    n, k = xb.shape
    width = w.shape[1]
    return pl.pallas_call(
        _proj_kernel,
        grid=(n // tm, width // PROJ_TN),
        in_specs=[pl.BlockSpec((tm, k), lambda i, j: (i, 0)),
                  pl.BlockSpec((k, PROJ_TN), lambda i, j: (0, j))],
        out_specs=pl.BlockSpec((tm, PROJ_TN), lambda i, j: (i, j)),
        out_shape=jax.ShapeDtypeStruct((n, width), F32),
        compiler_params=_cparams(("parallel", "arbitrary")),
        name="proj",
    )(xb, w)


def _s5_kernel(u_ref, m_ref, q_ref, d_ref, apr_ref, api_ref, y_ref, *, folds_per_seq):
    u = u_ref[0]
    rows = u.shape[0]
    r = _dot(u.astype(BF16), m_ref[0])
    y_intra = r[:, :4 * LANES]
    s_re = r[:, 4 * LANES:5 * LANES]
    s_im = r[:, 5 * LANES:6 * LANES]
    fold = lax.broadcasted_iota(jnp.int32, (rows, LANES), 0) & (folds_per_seq - 1)
    step = 0
    dist = 1
    while dist < folds_per_seq:
        ok = fold >= dist
        sh_re = jnp.where(ok, pltpu.roll(s_re, dist, axis=0), 0.0)
        sh_im = jnp.where(ok, pltpu.roll(s_im, dist, axis=0), 0.0)
        ar = apr_ref[0, step:step + 1, :]
        ai = api_ref[0, step:step + 1, :]
        s_re, s_im = s_re + ar * sh_re - ai * sh_im, s_im + ar * sh_im + ai * sh_re
        step += 1
        dist *= 2
    ok = fold >= 1
    p_re = jnp.where(ok, pltpu.roll(s_re, 1, axis=0), 0.0)
    p_im = jnp.where(ok, pltpu.roll(s_im, 1, axis=0), 0.0)
    prev = jnp.concatenate([p_re, p_im], axis=1).astype(BF16)
    y_ref[0] = y_intra + _dot(prev, q_ref[0]) + d_ref[0] * u


def _s5_scan(u_fold, m_w, q_w, d_fold, ap_re, ap_im, folds_per_seq):
    pairs, rows, width = u_fold.shape
    kern = functools.partial(_s5_kernel, folds_per_seq=folds_per_seq)
    nsteps = ap_re.shape[1]
    return pl.pallas_call(
        kern,
        grid=(pairs,),
        in_specs=[pl.BlockSpec((1, rows, width), lambda g: (g, 0, 0)),
                  pl.BlockSpec((1, width, 6 * LANES), lambda g: (g, 0, 0)),
                  pl.BlockSpec((1, 2 * LANES, width), lambda g: (g, 0, 0)),
                  pl.BlockSpec((1, 1, width), lambda g: (g, 0, 0)),
                  pl.BlockSpec((1, nsteps, LANES), lambda g: (g, 0, 0)),
                  pl.BlockSpec((1, nsteps, LANES), lambda g: (g, 0, 0))],
        out_specs=pl.BlockSpec((1, rows, width), lambda g: (g, 0, 0)),
        out_shape=jax.ShapeDtypeStruct((pairs, rows, width), F32),
        compiler_params=_cparams(("parallel",)),
        name="s5_scan",
    )(u_fold, m_w, q_w, d_fold, ap_re, ap_im)


def _s5_weights(lam_re, lam_im, log_dt, b_re, b_im, c_re, c_im, d_skip, folds_per_seq):
    hp = lax.Precision.HIGHEST
    t_fold, g_n, p_n, c_n = S5_FOLD, S5_GROUPS, S5_STATE, S5_GROUP
    lr = jnp.minimum(lam_re.astype(F32), -S5_MIN_DECAY)
    li = lam_im.astype(F32)
    dt = jnp.exp(log_dt.astype(F32))[:, None]
    mag = jnp.exp(lr * dt)
    abar_re, abar_im = mag * jnp.cos(li * dt), mag * jnp.sin(li * dt)
    den = lr * lr + li * li
    fac_re = ((abar_re - 1.0) * lr + abar_im * li) / den
    fac_im = (abar_im * lr - (abar_re - 1.0) * li) / den
    br, bi = b_re.astype(F32), b_im.astype(F32)
    bbar_re = fac_re[..., None] * br - fac_im[..., None] * bi
    bbar_im = fac_re[..., None] * bi + fac_im[..., None] * br

    def apow(n):
        nn = jnp.asarray(n, F32)[:, None, None]
        m = jnp.exp(nn * (lr * dt)[None])
        return m * jnp.cos(nn * (li * dt)[None]), m * jnp.sin(nn * (li * dt)[None])

    pr, pi = apow(jnp.arange(t_fold + 1))
    cr, ci = c_re.astype(F32), c_im.astype(F32)
    ca_re = cr[None] * pr[:, :, None, :] - ci[None] * pi[:, :, None, :]
    ca_im = cr[None] * pi[:, :, None, :] + ci[None] * pr[:, :, None, :]
    k_tau = (jnp.einsum('tgcp,gpd->tgcd', ca_re[:t_fold], bbar_re, precision=hp)
             - jnp.einsum('tgcp,gpd->tgcd', ca_im[:t_fold], bbar_im, precision=hp))
    jj = jnp.arange(t_fold)[:, None]
    tt = jnp.arange(t_fold)[None, :]
    lag = jnp.clip(tt - jj, 0, t_fold - 1)
    toe = jnp.where((tt >= jj)[:, :, None, None, None], k_tau[lag], 0.0)
    m_y = toe.transpose(2, 0, 4, 1, 3).reshape(g_n, t_fold * c_n, t_fold * c_n)
    prj, pij = pr[t_fold - 1 - jnp.arange(t_fold)], pi[t_fold - 1 - jnp.arange(t_fold)]
    pv_re = prj[..., None] * bbar_re[None] - pij[..., None] * bbar_im[None]
    pv_im = prj[..., None] * bbar_im[None] + pij[..., None] * bbar_re[None]
    m_vre = pv_re.transpose(1, 0, 3, 2).reshape(g_n, t_fold * c_n, p_n)
    m_vim = pv_im.transpose(1, 0, 3, 2).reshape(g_n, t_fold * c_n, p_n)
    q_re = ca_re[1:].transpose(1, 3, 0, 2).reshape(g_n, p_n, t_fold * c_n)
    q_im = (-ca_im[1:]).transpose(1, 3, 0, 2).reshape(g_n, p_n, t_fold * c_n)

    half = g_n // 2
    z_y = jnp.zeros_like(m_y[0::2])
    z_v = jnp.zeros_like(m_vre[0::2])
    top = jnp.concatenate([m_y[0::2], z_y, m_vre[0::2], z_v, m_vim[0::2], z_v], axis=2)
    bot = jnp.concatenate([z_y, m_y[1::2], z_v, m_vre[1::2], z_v, m_vim[1::2]], axis=2)
    m_w = jnp.concatenate([top, bot], axis=1).astype(BF16)
    z_q = jnp.zeros_like(q_re[0::2])
    q_w = jnp.concatenate([
        jnp.concatenate([q_re[0::2], z_q], axis=2),
        jnp.concatenate([z_q, q_re[1::2]], axis=2),
        jnp.concatenate([q_im[0::2], z_q], axis=2),
        jnp.concatenate([z_q, q_im[1::2]], axis=2)], axis=1).astype(BF16)
    d_fold = jnp.tile(d_skip.astype(F32).reshape(half, 2, 1, c_n), (1, 1, t_fold, 1)).reshape(half, 1, 2 * t_fold * c_n)
    nsteps = max(int(math.log2(folds_per_seq)), 1)
    sr, si = apow(t_fold * (2 ** jnp.arange(nsteps)))
    ap_re = sr.reshape(nsteps, half, 2 * p_n).transpose(1, 0, 2)
    ap_im = si.reshape(nsteps, half, 2 * p_n).transpose(1, 0, 2)
    return m_w, q_w, d_fold, ap_re, ap_im


def _glu_kernel(y_ref, w_ref, b_ref, o_ref):
    y = y_ref[...]
    z = 0.5 * y * (1.0 + jnp.tanh(math.sqrt(2.0 / math.pi) * (y + 0.044715 * (y * y * y))))
    a = _dot(z.astype(BF16), w_ref[...]) + b_ref[...]
    o_ref[...] = (z * _sigmoid(a)).astype(o_ref.dtype)


def _glu(y, w, b, tm):
    n, width = y.shape
    return pl.pallas_call(
        _glu_kernel,
        grid=(n // tm,),
        in_specs=[pl.BlockSpec((tm, width), lambda i: (i, 0)),
                  pl.BlockSpec((width, width), lambda i: (0, 0)),
                  pl.BlockSpec((1, width), lambda i: (0, 0))],
        out_specs=pl.BlockSpec((tm, width), lambda i: (i, 0)),
        out_shape=jax.ShapeDtypeStruct((n, width), BF16),
        compiler_params=_cparams(("parallel",)),
        name="s5_glu",
    )(y, w, b)


def _s5_branch(h, n, seq, s5w, w_glu, b_glu, tm):
    folds_per_seq = seq // S5_FOLD
    half = S5_GROUPS // 2
    u = h[:, COL_U:COL_U + BRANCH_WIDTH]
    u_fold = (u.reshape(n // S5_FOLD, S5_FOLD, half, 2, S5_GROUP)
              .transpose(2, 0, 3, 1, 4).reshape(half, n // S5_FOLD, 2 * S5_FOLD * S5_GROUP))
    y_fold = _s5_scan(u_fold, *s5w, folds_per_seq)
    y = (y_fold.reshape(half, n // S5_FOLD, 2, S5_FOLD, S5_GROUP)
         .transpose(1, 3, 0, 2, 4).reshape(n, BRANCH_WIDTH))
    return _glu(y, w_glu, b_glu, tm)


def _gated_chunk(q, k, v, g, st_ref, head):
    c = q.shape[0]
    row = lax.broadcasted_iota(jnp.int32, (c, LANES), 0)
    ri = lax.broadcasted_iota(jnp.int32, (c, c), 0)
    ci = lax.broadcasted_iota(jnp.int32, (c, c), 1)
    xr = jnp.where(ri > ci, ri ^ ci, 0)
    gc = g
    dist = 1
    while dist < c:
        gc = gc + jnp.where(row >= dist, pltpu.roll(gc, dist, axis=0), 0.0)
        dist *= 2
    a = jnp.where(ri == ci, _dot_nt(q.astype(BF16), k.astype(BF16)), 0.0)
    end = gc
    half = 1
    while half < c:
        first = (row & half) == 0
        ref = jnp.where(first, end, pltpu.roll(end, half, axis=0))
        e = jnp.exp(-jnp.abs(gc - ref))
        qt = (jnp.where(first, 0.0, e) * q).astype(BF16)
        kt = (jnp.where(first, e, 0.0) * k).astype(BF16)
        a = jnp.where(xr >= half, _dot_nt(qt, kt), a)
        end = jnp.where(first, pltpu.roll(end, c - half, axis=0), end)
        half *= 2
    st = st_ref[head]
    vb = v.astype(BF16)
    o = _dot(a.astype(BF16), vb) + _dot_nt((q * jnp.exp(gc)).astype(BF16), st.astype(BF16))
    kd = (k * jnp.exp(end - gc)).astype(BF16)
    st_ref[head] = st * jnp.exp(end[0:1, :]) + _dot_tn(vb, kd)
    return o


def _gla_kernel(q_ref, k_ref, v_ref, gate_ref, glow_ref, wg_ref, bg_ref, nw_ref, o_ref, st_ref):
    @pl.when(pl.program_id(1) == 0)
    def _():
        st_ref[...] = jnp.zeros_like(st_ref)

    glow = glow_ref[...].astype(BF16)
    for hd in range(GLA_HEADS):
        ks = slice(hd * GLA_DK, (hd + 1) * GLA_DK)
        vs = slice(hd * GLA_DV, (hd + 1) * GLA_DV)
        z = _dot(glow, wg_ref[:, ks]) + bg_ref[:, ks]
        g = (jnp.minimum(z, 0.0) - jnp.log1p(jnp.exp(-jnp.abs(z)))) / GLA_GATE_TAU
        q = q_ref[:, ks] * (GLA_DK ** -0.5)
        o = _gated_chunk(q, k_ref[:, ks], v_ref[:, vs], g, st_ref, hd)
        o = o * lax.rsqrt(jnp.mean(o * o, axis=-1, keepdims=True) + RMS_EPS) * nw_ref[...]
        gate = gate_ref[:, vs]
        o_ref[:, vs] = (o * (gate * _sigmoid(gate))).astype(o_ref.dtype)


def _gla(h, bsz, seq, wg, bg, nw):
    c = min(ATT_CHUNK, seq)
    nc = seq // c

    def col(off, width):
        return pl.BlockSpec((c, width), lambda b, i: (b * nc + i, off // width))

    return pl.pallas_call(
        _gla_kernel,
        grid=(bsz, nc),
        in_specs=[col(COL_QB, GLA_KEY), col(COL_KB, GLA_KEY), col(COL_VB, BRANCH_WIDTH),
                  col(COL_GATEB, BRANCH_WIDTH), col(COL_GLOW, LANES),
                  pl.BlockSpec((LANES, GLA_KEY), lambda b, i: (0, 0)),
                  pl.BlockSpec((1, GLA_KEY), lambda b, i: (0, 0)),
                  pl.BlockSpec((1, GLA_DV), lambda b, i: (0, 0))],
        out_specs=pl.BlockSpec((c, BRANCH_WIDTH), lambda b, i: (b * nc + i, 0)),
        out_shape=jax.ShapeDtypeStruct((bsz * seq, BRANCH_WIDTH), BF16),
        scratch_shapes=[pltpu.VMEM((GLA_HEADS, GLA_DV, GLA_DK), F32)],
        compiler_params=_cparams(("parallel", "arbitrary")),
        name="gla",
    )(h, h, h, h, h, wg, bg, nw)


def _hgrn_kernel(q_ref, f_ref, i_ref, gate_ref, lb_ref, nw_ref, o_ref, st_ref):
    @pl.when(pl.program_id(1) == 0)
    def _():
        st_ref[...] = jnp.zeros_like(st_ref)

    for hd in range(HGRN_HEADS):
        ks = slice(hd * HGRN_EXPAND, (hd + 1) * HGRN_EXPAND)
        vs = slice(hd * HGRN_DV, (hd + 1) * HGRN_DV)
        lb = lb_ref[:, ks]
        f = lb + (1.0 - lb) * _sigmoid(f_ref[:, ks])
        qr = q_ref[:, ks]
        o = _gated_chunk(qr * _sigmoid(qr), 1.0 - f, i_ref[:, vs], jnp.log(f), st_ref, hd)
        o = o * _sigmoid(gate_ref[:, vs])
        o = o * lax.rsqrt(jnp.mean(o * o, axis=-1, keepdims=True) + RMS_EPS) * nw_ref[...]
        o_ref[:, vs] = o.astype(o_ref.dtype)


def _hgrn(h, bsz, seq, lb, nw):
    c = min(ATT_CHUNK, seq)
    nc = seq // c

    def col(off, width):
        return pl.BlockSpec((c, width), lambda b, i: (b * nc + i, off // width))

    return pl.pallas_call(
        _hgrn_kernel,
        grid=(bsz, nc),
        in_specs=[col(COL_QC, HGRN_KEY), col(COL_FC, HGRN_KEY), col(COL_IC, BRANCH_WIDTH),
                  col(COL_GATEC, BRANCH_WIDTH),
                  pl.BlockSpec((1, HGRN_KEY), lambda b, i: (0, 0)),
                  pl.BlockSpec((1, HGRN_DV), lambda b, i: (0, 0))],
        out_specs=pl.BlockSpec((c, BRANCH_WIDTH), lambda b, i: (b * nc + i, 0)),
        out_shape=jax.ShapeDtypeStruct((bsz * seq, BRANCH_WIDTH), BF16),
        scratch_shapes=[pltpu.VMEM((HGRN_HEADS, HGRN_DV, HGRN_EXPAND), F32)],
        compiler_params=_cparams(("parallel", "arbitrary")),
        name="hgrn",
    )(h, h, h, h, lb, nw)


def _mixer_out_kernel(ya_ref, yb_ref, yc_ref, ga_ref, gb_ref, gc_ref, wup_ref, wout_ref,
                      x_ref, lg_ref, lb_ref, o_ref, ob_ref):
    kk = pl.program_id(1)
    merged = (_sigmoid(ga_ref[...]) * _dot(ya_ref[...], wup_ref[0])
              + _sigmoid(gb_ref[...]) * _dot(yb_ref[...], wup_ref[1])
              + _sigmoid(gc_ref[...]) * _dot(yc_ref[...], wup_ref[2]))
    part = _dot(merged.astype(BF16), wout_ref[...])

    @pl.when(kk == 0)
    def _():
        o_ref[...] = part

    @pl.when(kk > 0)
    def _():
        o_ref[...] += part

    @pl.when(kk == pl.num_programs(1) - 1)
    def _():
        xn = _layer_norm(DN_ALPHA * x_ref[...] + o_ref[...], lg_ref[...], lb_ref[...])
        o_ref[...] = xn
        ob_ref[...] = xn.astype(BF16)


def _mixer_out(ya, yb, yc, h, wup, wout, x, ln_g, ln_b, tm, tk):
    n = x.shape[0]
    nk = D_MODEL // tk

    def gate(branch):
        off = (COL_MERGE + branch * D_MODEL) // tk
        return pl.BlockSpec((tm, tk), lambda i, k: (i, off + k))

    ybs = pl.BlockSpec((tm, BRANCH_WIDTH), lambda i, k: (i, 0))
    row = pl.BlockSpec((tm, D_MODEL), lambda i, k: (i, 0))
    vec = pl.BlockSpec((1, D_MODEL), lambda i, k: (0, 0))
    return pl.pallas_call(
        _mixer_out_kernel,
        grid=(n // tm, nk),
        in_specs=[ybs, ybs, ybs, gate(0), gate(1), gate(2),
                  pl.BlockSpec((N_BRANCH, BRANCH_WIDTH, tk), lambda i, k: (0, 0, k)),
                  pl.BlockSpec((tk, D_MODEL), lambda i, k: (k, 0)),
                  row, vec, vec],
        out_specs=[row, row],
        out_shape=[jax.ShapeDtypeStruct((n, D_MODEL), F32), jax.ShapeDtypeStruct((n, D_MODEL), BF16)],
        compiler_params=_cparams(("parallel", "arbitrary")),
        name="mixer_out",
    )(ya, yb, yc, h, h, h, wup, wout, x, ln_g, ln_b)


def _mlp_kernel(xb_ref, w1_ref, w2_ref, x_ref, lg_ref, lb_ref, o_ref, ob_ref):
    kk = pl.program_id(1)
    hid = jnp.maximum(_dot(xb_ref[...], w1_ref[...]), 0.0)
    part = _dot((hid * hid).astype(BF16), w2_ref[...])

    @pl.when(kk == 0)
    def _():
        o_ref[...] = part

    @pl.when(kk > 0)
    def _():
        o_ref[...] += part

    @pl.when(kk == pl.num_programs(1) - 1)
    def _():
        xn = _layer_norm(DN_ALPHA * x_ref[...] + o_ref[...], lg_ref[...], lb_ref[...])
        o_ref[...] = xn
        ob_ref[...] = xn.astype(BF16)


def _mlp(xb, w1, w2, x, ln_g, ln_b, tm, th):
    n = x.shape[0]
    row = pl.BlockSpec((tm, D_MODEL), lambda i, k: (i, 0))
    vec = pl.BlockSpec((1, D_MODEL), lambda i, k: (0, 0))
    return pl.pallas_call(
        _mlp_kernel,
        grid=(n // tm, MLP_HIDDEN // th),
        in_specs=[row,
                  pl.BlockSpec((D_MODEL, th), lambda i, k: (0, k)),
                  pl.BlockSpec((th, D_MODEL), lambda i, k: (k, 0)),
                  row, vec, vec],
        out_specs=[row, row],
        out_shape=[jax.ShapeDtypeStruct((n, D_MODEL), F32), jax.ShapeDtypeStruct((n, D_MODEL), BF16)],
        compiler_params=_cparams(("parallel", "arbitrary")),
        name="mlp",
    )(xb, w1, w2, x, ln_g, ln_b)


def _reorder_w_in(w):
    bounds = [BRANCH_WIDTH, GLA_KEY, GLA_KEY, BRANCH_WIDTH, GLA_GATE_RANK, BRANCH_WIDTH,
              HGRN_KEY, HGRN_KEY, BRANCH_WIDTH, BRANCH_WIDTH, N_BRANCH * D_MODEL]
    parts, acc = [], 0
    for wd in bounds:
        parts.append(w[:, acc:acc + wd])
        acc += wd
    u_a, q_b, k_b, v_b, glow, gate_b, q_c, f_c, i_c, gate_c, merge = parts
    pad = jnp.zeros((w.shape[0], PROJ_WIDTH - COL_GLOW - GLA_GATE_RANK), w.dtype)
    return jnp.concatenate([u_a, q_b, k_b, v_b, gate_b, q_c, f_c, i_c, gate_c, merge, glow, pad],
                           axis=1).astype(BF16)


def _forward(x, w_in, s5_lam_re, s5_lam_im, s5_log_dt, s5_b_re, s5_b_im, s5_c_re, s5_c_im, s5_d,
             s5_w_glu, s5_b_glu, gla_w_gate, gla_b_gate, gla_norm_w, hgrn_lb_logits, hgrn_norm_w,
             w_up, w_out, ln1_g, ln1_b, ln2_g, ln2_b, w_mlp_in, w_mlp_out):
    bsz, seq, d = x.shape
    n = bsz * seq
    depth = w_in.shape[0]
    tm = min(512, n)
    p = jax.nn.softmax(hgrn_lb_logits.astype(F32), axis=0)
    lb = jnp.cumsum(p, axis=0) - p[0]
    xf = x.reshape(n, d).astype(F32)
    xb = xf.astype(BF16)
    for l in range(depth):
        h = _proj(xb, _reorder_w_in(w_in[l]), min(1024, n))
        s5w = _s5_weights(s5_lam_re[l], s5_lam_im[l], s5_log_dt[l], s5_b_re[l], s5_b_im[l],
                          s5_c_re[l], s5_c_im[l], s5_d[l], seq // S5_FOLD)
        ya = _s5_branch(h, n, seq, s5w, s5_w_glu[l].astype(BF16), s5_b_glu[l].reshape(1, -1).astype(F32), tm)
        wg = jnp.concatenate([gla_w_gate[l], jnp.zeros((LANES - GLA_GATE_RANK, GLA_KEY), gla_w_gate.dtype)],
                             axis=0).astype(BF16)
        yb = _gla(h, bsz, seq, wg, gla_b_gate[l].reshape(1, -1).astype(F32),
                  gla_norm_w[l].reshape(1, -1).astype(F32))
        yc = _hgrn(h, bsz, seq, lb[l].reshape(1, -1), hgrn_norm_w[l].reshape(1, -1).astype(F32))
        xf, xb = _mixer_out(ya, yb, yc, h, w_up[l].astype(BF16), w_out[l].astype(BF16), xf,
                            ln1_g[l].reshape(1, -1), ln1_b[l].reshape(1, -1), tm, 512)
        xf, xb = _mlp(xb, w_mlp_in[l].astype(BF16), w_mlp_out[l].astype(BF16), xf,
                      ln2_g[l].reshape(1, -1), ln2_b[l].reshape(1, -1), tm, 512)
    return xf.reshape(bsz, seq, d).astype(x.dtype)


def kernel(x, w_in, s5_lam_re, s5_lam_im, s5_log_dt, s5_b_re, s5_b_im, s5_c_re, s5_c_im, s5_d, s5_w_glu, s5_b_glu, gla_w_gate, gla_b_gate, gla_norm_w, hgrn_lb_logits, hgrn_norm_w, w_up, w_out, ln1_g, ln1_b, ln2_g, ln2_b, w_mlp_in, w_mlp_out):
    return _forward(x, w_in, s5_lam_re, s5_lam_im, s5_log_dt, s5_b_re, s5_b_im, s5_c_re, s5_c_im, s5_d,
                    s5_w_glu, s5_b_glu, gla_w_gate, gla_b_gate, gla_norm_w, hgrn_lb_logits, hgrn_norm_w,
                    w_up, w_out, ln1_g, ln1_b, ln2_g, ln2_b, w_mlp_in, w_mlp_out)
```

```python
import functools
import math

import jax
import jax.numpy as jnp
from jax import lax
from jax.experimental import pallas as pl
from jax.experimental.pallas import tpu as pltpu

F32 = jnp.float32
BF16 = jnp.bfloat16

D_MODEL = 2048
DEPTH = 4
N_BRANCH = 3
BRANCH_WIDTH = D_MODEL // 2
S5_GROUP = 16
S5_GROUPS = BRANCH_WIDTH // S5_GROUP
S5_STATE = 64
S5_MIN_DECAY = 1e-4
GLA_HEADS = 4
GLA_DV = BRANCH_WIDTH // GLA_HEADS
GLA_DK = GLA_DV // 2
GLA_KEY = GLA_HEADS * GLA_DK
GLA_GATE_RANK = 16
GLA_GATE_TAU = 16.0
HGRN_EXPAND = 128
HGRN_HEADS = BRANCH_WIDTH // HGRN_EXPAND
HGRN_DV = BRANCH_WIDTH // HGRN_HEADS
HGRN_KEY = HGRN_HEADS * HGRN_EXPAND
MLP_HIDDEN = 4 * D_MODEL
DN_ALPHA = (2 * DEPTH) ** 0.25
LN_EPS = 1e-5
RMS_EPS = 1e-6

LANES = 128
SUBLANES = 8
VMEM_LIMIT_BYTES = 56 * 1024 * 1024

COL_U = 0
COL_QB = COL_U + BRANCH_WIDTH
COL_KB = COL_QB + GLA_KEY
COL_VB = COL_KB + GLA_KEY
COL_GATEB = COL_VB + BRANCH_WIDTH
COL_QC = COL_GATEB + BRANCH_WIDTH
COL_FC = COL_QC + HGRN_KEY
COL_IC = COL_FC + HGRN_KEY
COL_GATEC = COL_IC + BRANCH_WIDTH
COL_MERGE = COL_GATEC + BRANCH_WIDTH
COL_GLOW = COL_MERGE + N_BRANCH * D_MODEL
PROJ_TN = 512
PROJ_WIDTH = COL_GLOW + PROJ_TN

S5_FOLD = SUBLANES
S5_TILE_GROUPS = LANES // S5_GROUP
S5_TILES = S5_GROUPS // S5_TILE_GROUPS
S5_TILE_STATE = S5_TILE_GROUPS * S5_STATE
ATT_CHUNK = 128
ROW_TILE = 512
PROJ_TM = 1024
HID_TILE = 512


def _cparams(sem):
    return pltpu.CompilerParams(dimension_semantics=sem, vmem_limit_bytes=VMEM_LIMIT_BYTES)


def _dot(a, b):
    return jnp.dot(a, b, preferred_element_type=F32)


def _dot_nt(a, b):
    return lax.dot_general(a, b, (((1,), (1,)), ((), ())), preferred_element_type=F32)


def _dot_tn(a, b):
    return lax.dot_general(a, b, (((0,), (0,)), ((), ())), preferred_element_type=F32)


def _sigmoid(x):
    return 1.0 / (1.0 + jnp.exp(-x))


def _layer_norm(v, g, b):
    mu = jnp.mean(v, axis=-1, keepdims=True)
    xc = v - mu
    var = jnp.mean(xc * xc, axis=-1, keepdims=True)
    return xc * lax.rsqrt(var + LN_EPS) * g + b


def _proj_kernel(x_ref, w_ref, o_ref):
    o_ref[...] = _dot(x_ref[...], w_ref[...])


def _proj(xb, w, layer):
    n, k = xb.shape
    width = w.shape[2]
    tm = min(PROJ_TM, n)
    return pl.pallas_call(
        _proj_kernel,
        grid=(n // tm, width // PROJ_TN),
        in_specs=[pl.BlockSpec((tm, k), lambda i, j: (i, 0)),
                  pl.BlockSpec((None, k, PROJ_TN), lambda i, j: (layer, 0, j))],
        out_specs=pl.BlockSpec((tm, PROJ_TN), lambda i, j: (i, j)),
        out_shape=jax.ShapeDtypeStruct((n, width), F32),
        compiler_params=_cparams(("parallel", "arbitrary")),
        name="proj",
    )(xb, w)


def _s5_kernel(u_ref, m_ref, q_ref, d_ref, apr_ref, api_ref, y_ref):
    folds = u_ref.shape[0] // S5_FOLD
    us = [u_ref[pl.ds(t, folds, stride=S5_FOLD), :] for t in range(S5_FOLD)]
    ucat = jnp.concatenate([u.astype(BF16) for u in us], axis=1)
    r = _dot(ucat, m_ref[...])
    ywidth = S5_FOLD * LANES
    y_intra = r[:, :ywidth]
    s_re = r[:, ywidth:ywidth + S5_TILE_STATE]
    s_im = r[:, ywidth + S5_TILE_STATE:]
    fold = lax.broadcasted_iota(jnp.int32, (folds, S5_TILE_STATE), 0)
    step = 0
    dist = 1
    while dist < folds:
        ok = fold >= dist
        sh_re = jnp.where(ok, pltpu.roll(s_re, dist, axis=0), 0.0)
        sh_im = jnp.where(ok, pltpu.roll(s_im, dist, axis=0), 0.0)
        ar = apr_ref[step:step + 1, :]
        ai = api_ref[step:step + 1, :]
        s_re, s_im = s_re + ar * sh_re - ai * sh_im, s_im + ar * sh_im + ai * sh_re
        step += 1
        dist *= 2
    ok = fold >= 1
    p_re = jnp.where(ok, pltpu.roll(s_re, 1, axis=0), 0.0)
    p_im = jnp.where(ok, pltpu.roll(s_im, 1, axis=0), 0.0)
    prev = jnp.concatenate([p_re, p_im], axis=1).astype(BF16)
    y = y_intra + _dot(prev, q_ref[...])
    for t in range(S5_FOLD):
        y_ref[pl.ds(t, folds, stride=S5_FOLD), :] = y[:, t * LANES:(t + 1) * LANES] + d_ref[...] * us[t]


def _s5_scan(h, bsz, seq, s5w, layer):
    m_w, q_w, d_t, ap_re, ap_im = s5w
    nsteps = ap_re.shape[2]
    tile = lambda j, b: (layer, j, 0, 0)
    return pl.pallas_call(
        _s5_kernel,
        grid=(S5_TILES, bsz),
        in_specs=[pl.BlockSpec((seq, LANES), lambda j, b: (b, COL_U // LANES + j)),
                  pl.BlockSpec((None, None) + m_w.shape[2:], tile),
                  pl.BlockSpec((None, None) + q_w.shape[2:], tile),
                  pl.BlockSpec((None, None, 1, LANES), tile),
                  pl.BlockSpec((None, None, nsteps, S5_TILE_STATE), tile),
                  pl.BlockSpec((None, None, nsteps, S5_TILE_STATE), tile)],
        out_specs=pl.BlockSpec((seq, LANES), lambda j, b: (b, j)),
        out_shape=jax.ShapeDtypeStruct((bsz * seq, BRANCH_WIDTH), F32),
        compiler_params=_cparams(("parallel", "arbitrary")),
        name="s5_scan",
    )(h, m_w, q_w, d_t, ap_re, ap_im)


def _s5_weights(lam_re, lam_im, log_dt, b_re, b_im, c_re, c_im, d_skip, folds_per_seq):
    hp = lax.Precision.HIGHEST
    t_fold, g_n, p_n, c_n, tg = S5_FOLD, S5_GROUPS, S5_STATE, S5_GROUP, S5_TILE_GROUPS
    lr = jnp.minimum(lam_re.astype(F32), -S5_MIN_DECAY)
    li = lam_im.astype(F32)
    dt = jnp.exp(log_dt.astype(F32))[:, None]
    mag = jnp.exp(lr * dt)
    abar_re, abar_im = mag * jnp.cos(li * dt), mag * jnp.sin(li * dt)
    den = lr * lr + li * li
    fac_re = ((abar_re - 1.0) * lr + abar_im * li) / den
    fac_im = (abar_im * lr - (abar_re - 1.0) * li) / den
    br, bi = b_re.astype(F32), b_im.astype(F32)
    bbar_re = fac_re[..., None] * br - fac_im[..., None] * bi
    bbar_im = fac_re[..., None] * bi + fac_im[..., None] * br

    def apow(n):
        nn = jnp.asarray(n, F32)[:, None, None]
        m = jnp.exp(nn * (lr * dt)[None])
        return m * jnp.cos(nn * (li * dt)[None]), m * jnp.sin(nn * (li * dt)[None])

    pr, pi = apow(jnp.arange(t_fold + 1))
    cr, ci = c_re.astype(F32), c_im.astype(F32)
    ca_re = cr[None] * pr[:, :, None, :] - ci[None] * pi[:, :, None, :]
    ca_im = cr[None] * pi[:, :, None, :] + ci[None] * pr[:, :, None, :]
    k_tau = (jnp.einsum('tgcp,gpd->tgcd', ca_re[:t_fold], bbar_re, precision=hp)
             - jnp.einsum('tgcp,gpd->tgcd', ca_im[:t_fold], bbar_im, precision=hp))
    jj = jnp.arange(t_fold)[:, None]
    tt = jnp.arange(t_fold)[None, :]
    lag = jnp.clip(tt - jj, 0, t_fold - 1)
    toe = jnp.where((tt >= jj)[:, :, None, None, None], k_tau[lag], 0.0)
    eye = jnp.eye(tg, dtype=F32)
    tiles = g_n // tg
    toe = toe.reshape(t_fold, t_fold, tiles, tg, c_n, c_n)
    m_y = jnp.einsum('jtugcd,gh->ujgdthc', toe, eye).reshape(tiles, t_fold * LANES, t_fold * LANES)
    rev = t_fold - 1 - jnp.arange(t_fold)
    prj, pij = pr[rev], pi[rev]
    pv_re = (prj[..., None] * bbar_re[None] - pij[..., None] * bbar_im[None]).reshape(t_fold, tiles, tg, p_n, c_n)
    pv_im = (prj[..., None] * bbar_im[None] + pij[..., None] * bbar_re[None]).reshape(t_fold, tiles, tg, p_n, c_n)
    m_vre = jnp.einsum('jugpd,gh->ujgdhp', pv_re, eye).reshape(tiles, t_fold * LANES, tg * p_n)
    m_vim = jnp.einsum('jugpd,gh->ujgdhp', pv_im, eye).reshape(tiles, t_fold * LANES, tg * p_n)
    m_w = jnp.concatenate([m_y, m_vre, m_vim], axis=2).astype(BF16)
    qa_re = ca_re[1:].reshape(t_fold, tiles, tg, c_n, p_n)
    qa_im = (-ca_im[1:]).reshape(t_fold, tiles, tg, c_n, p_n)
    q_re = jnp.einsum('tugcp,gh->ugpthc', qa_re, eye).reshape(tiles, tg * p_n, t_fold * LANES)
    q_im = jnp.einsum('tugcp,gh->ugpthc', qa_im, eye).reshape(tiles, tg * p_n, t_fold * LANES)
    q_w = jnp.concatenate([q_re, q_im], axis=1).astype(BF16)
    d_t = d_skip.astype(F32).reshape(tiles, 1, LANES)
    nsteps = max(int(math.log2(folds_per_seq)), 1)
    sr, si = apow(t_fold * (2 ** jnp.arange(nsteps)))
    ap_re = sr.reshape(nsteps, tiles, tg * p_n).transpose(1, 0, 2)
    ap_im = si.reshape(nsteps, tiles, tg * p_n).transpose(1, 0, 2)
    return m_w, q_w, d_t, ap_re, ap_im


def _glu_kernel(y_ref, w_ref, b_ref, o_ref):
    y = y_ref[...]
    z = 0.5 * y * (1.0 + jnp.tanh(math.sqrt(2.0 / math.pi) * (y + 0.044715 * (y * y * y))))
    a = _dot(z.astype(BF16), w_ref[...]) + b_ref[...]
    o_ref[...] = (z * _sigmoid(a)).astype(o_ref.dtype)


def _glu(y, w, b, layer):
    n, width = y.shape
    tm = min(ROW_TILE, n)
    return pl.pallas_call(
        _glu_kernel,
        grid=(n // tm,),
        in_specs=[pl.BlockSpec((tm, width), lambda i: (i, 0)),
                  pl.BlockSpec((None, width, width), lambda i: (layer, 0, 0)),
                  pl.BlockSpec((None, 1, width), lambda i: (layer, 0, 0))],
        out_specs=pl.BlockSpec((tm, width), lambda i: (i, 0)),
        out_shape=jax.ShapeDtypeStruct((n, width), BF16),
        compiler_params=_cparams(("parallel",)),
        name="s5_glu",
    )(y, w, b)


def _gated_chunk(q, k, v, g, st_ref, head):
    c = q.shape[0]
    row = lax.broadcasted_iota(jnp.int32, (c, LANES), 0)
    ri = lax.broadcasted_iota(jnp.int32, (c, c), 0)
    ci = lax.broadcasted_iota(jnp.int32, (c, c), 1)
    xr = jnp.where(ri > ci, ri ^ ci, 0)
    gc = g
    dist = 1
    while dist < c:
        gc = gc + jnp.where(row >= dist, pltpu.roll(gc, dist, axis=0), 0.0)
        dist *= 2
    a = jnp.where(ri == ci, _dot_nt(q.astype(BF16), k.astype(BF16)), 0.0)
    end = gc
    half = 1
    while half < c:
        first = (row & half) == 0
        ref = jnp.where(first, end, pltpu.roll(end, half, axis=0))
        e = jnp.exp(-jnp.abs(gc - ref))
        qt = (jnp.where(first, 0.0, e) * q).astype(BF16)
        kt = (jnp.where(first, e, 0.0) * k).astype(BF16)
        a = jnp.where(xr >= half, _dot_nt(qt, kt), a)
        end = jnp.where(first, pltpu.roll(end, c - half, axis=0), end)
        half *= 2
    st = st_ref[head]
    vb = v.astype(BF16)
    o = _dot(a.astype(BF16), vb) + _dot_nt((q * jnp.exp(gc)).astype(BF16), st.astype(BF16))
    kd = (k * jnp.exp(end - gc)).astype(BF16)
    st_ref[head] = st * jnp.exp(end[0:1, :]) + _dot_tn(vb, kd)
    return o


def _gla_kernel(q_ref, k_ref, v_ref, gate_ref, glow_ref, wg_ref, bg_ref, nw_ref, o_ref, st_ref):
    @pl.when(pl.program_id(1) == 0)
    def _():
        st_ref[...] = jnp.zeros_like(st_ref)

    glow = glow_ref[...].astype(BF16)
    for hd in range(GLA_HEADS):
        ks = slice(hd * GLA_DK, (hd + 1) * GLA_DK)
        vs = slice(hd * GLA_DV, (hd + 1) * GLA_DV)
        z = _dot(glow, wg_ref[:, ks]) + bg_ref[:, ks]
        g = (jnp.minimum(z, 0.0) - jnp.log1p(jnp.exp(-jnp.abs(z)))) / GLA_GATE_TAU
        q = q_ref[:, ks] * (GLA_DK ** -0.5)
        o = _gated_chunk(q, k_ref[:, ks], v_ref[:, vs], g, st_ref, hd)
        o = o * lax.rsqrt(jnp.mean(o * o, axis=-1, keepdims=True) + RMS_EPS) * nw_ref[...]
        gate = gate_ref[:, vs]
        o_ref[:, vs] = (o * (gate * _sigmoid(gate))).astype(o_ref.dtype)


def _att_specs(seq):
    c = min(ATT_CHUNK, seq)
    nc = seq // c

    def col(off, width):
        return pl.BlockSpec((c, width), lambda b, i: (b * nc + i, off // width))

    return c, nc, col


def _gla(h, bsz, seq, wg, bg, nw, layer):
    c, nc, col = _att_specs(seq)
    vec = lambda b, i: (layer, 0, 0)
    return pl.pallas_call(
        _gla_kernel,
        grid=(bsz, nc),
        in_specs=[col(COL_QB, GLA_KEY), col(COL_KB, GLA_KEY), col(COL_VB, BRANCH_WIDTH),
                  col(COL_GATEB, BRANCH_WIDTH), col(COL_GLOW, LANES),
                  pl.BlockSpec((None, LANES, GLA_KEY), vec),
                  pl.BlockSpec((None, 1, GLA_KEY), vec),
                  pl.BlockSpec((None, 1, GLA_DV), vec)],
        out_specs=pl.BlockSpec((c, BRANCH_WIDTH), lambda b, i: (b * nc + i, 0)),
        out_shape=jax.ShapeDtypeStruct((bsz * seq, BRANCH_WIDTH), BF16),
        scratch_shapes=[pltpu.VMEM((GLA_HEADS, GLA_DV, GLA_DK), F32)],
        compiler_params=_cparams(("parallel", "arbitrary")),
        name="gla",
    )(h, h, h, h, h, wg, bg, nw)


def _hgrn_kernel(q_ref, f_ref, i_ref, gate_ref, lb_ref, nw_ref, o_ref, st_ref):
    @pl.when(pl.program_id(1) == 0)
    def _():
        st_ref[...] = jnp.zeros_like(st_ref)

    for hd in range(HGRN_HEADS):
        ks = slice(hd * HGRN_EXPAND, (hd + 1) * HGRN_EXPAND)
        vs = slice(hd * HGRN_DV, (hd + 1) * HGRN_DV)
        lb = lb_ref[:, ks]
        f = lb + (1.0 - lb) * _sigmoid(f_ref[:, ks])
        qr = q_ref[:, ks]
        o = _gated_chunk(qr * _sigmoid(qr), 1.0 - f, i_ref[:, vs], jnp.log(f), st_ref, hd)
        o = o * _sigmoid(gate_ref[:, vs])
        o = o * lax.rsqrt(jnp.mean(o * o, axis=-1, keepdims=True) + RMS_EPS) * nw_ref[...]
        o_ref[:, vs] = o.astype(o_ref.dtype)


def _hgrn(h, bsz, seq, lb, nw, layer):
    c, nc, col = _att_specs(seq)
    vec = lambda b, i: (layer, 0, 0)
    return pl.pallas_call(
        _hgrn_kernel,
        grid=(bsz, nc),
        in_specs=[col(COL_QC, HGRN_KEY), col(COL_FC, HGRN_KEY), col(COL_IC, BRANCH_WIDTH),
                  col(COL_GATEC, BRANCH_WIDTH),
                  pl.BlockSpec((None, 1, HGRN_KEY), vec),
                  pl.BlockSpec((None, 1, HGRN_DV), vec)],
        out_specs=pl.BlockSpec((c, BRANCH_WIDTH), lambda b, i: (b * nc + i, 0)),
        out_shape=jax.ShapeDtypeStruct((bsz * seq, BRANCH_WIDTH), BF16),
        scratch_shapes=[pltpu.VMEM((HGRN_HEADS, HGRN_DV, HGRN_EXPAND), F32)],
        compiler_params=_cparams(("parallel", "arbitrary")),
        name="hgrn",
    )(h, h, h, h, lb, nw)


def _accumulate_then_norm(part, x_ref, lg_ref, lb_ref, o_ref, ob_ref):
    kk = pl.program_id(1)

    @pl.when(kk == 0)
    def _():
        o_ref[...] = part

    @pl.when(kk > 0)
    def _():
        o_ref[...] += part

    @pl.when(kk == pl.num_programs(1) - 1)
    def _():
        xn = _layer_norm(DN_ALPHA * x_ref[...] + o_ref[...], lg_ref[...], lb_ref[...])
        o_ref[...] = xn
        ob_ref[...] = xn.astype(BF16)


def _mixer_out_kernel(ya_ref, yb_ref, yc_ref, ga_ref, gb_ref, gc_ref, wup_ref, wout_ref,
                      x_ref, lg_ref, lb_ref, o_ref, ob_ref):
    merged = (_sigmoid(ga_ref[...]) * _dot(ya_ref[...], wup_ref[0])
              + _sigmoid(gb_ref[...]) * _dot(yb_ref[...], wup_ref[1])
              + _sigmoid(gc_ref[...]) * _dot(yc_ref[...], wup_ref[2]))
    part = _dot(merged.astype(BF16), wout_ref[...])
    _accumulate_then_norm(part, x_ref, lg_ref, lb_ref, o_ref, ob_ref)


def _mixer_out(ya, yb, yc, h, wup, wout, x, ln_g, ln_b, layer):
    n = x.shape[0]
    tm = min(ROW_TILE, n)
    tk = HID_TILE

    def gate(branch):
        off = (COL_MERGE + branch * D_MODEL) // tk
        return pl.BlockSpec((tm, tk), lambda i, k: (i, off + k))

    ybs = pl.BlockSpec((tm, BRANCH_WIDTH), lambda i, k: (i, 0))
    row = pl.BlockSpec((tm, D_MODEL), lambda i, k: (i, 0))
    vec = pl.BlockSpec((None, 1, D_MODEL), lambda i, k: (layer, 0, 0))
    return pl.pallas_call(
        _mixer_out_kernel,
        grid=(n // tm, D_MODEL // tk),
        in_specs=[ybs, ybs, ybs, gate(0), gate(1), gate(2),
                  pl.BlockSpec((None, N_BRANCH, BRANCH_WIDTH, tk), lambda i, k: (layer, 0, 0, k)),
                  pl.BlockSpec((None, tk, D_MODEL), lambda i, k: (layer, k, 0)),
                  row, vec, vec],
        out_specs=[row, row],
        out_shape=[jax.ShapeDtypeStruct((n, D_MODEL), F32), jax.ShapeDtypeStruct((n, D_MODEL), BF16)],
        compiler_params=_cparams(("parallel", "arbitrary")),
        name="mixer_out",
    )(ya, yb, yc, h, h, h, wup, wout, x, ln_g, ln_b)


def _mlp_kernel(xb_ref, w1_ref, w2_ref, x_ref, lg_ref, lb_ref, o_ref, ob_ref):
    hid = jnp.maximum(_dot(xb_ref[...], w1_ref[...]), 0.0)
    part = _dot((hid * hid).astype(BF16), w2_ref[...])
    _accumulate_then_norm(part, x_ref, lg_ref, lb_ref, o_ref, ob_ref)


def _mlp(xb, w1, w2, x, ln_g, ln_b, layer):
    n = x.shape[0]
    tm = min(ROW_TILE, n)
    th = HID_TILE
    row = pl.BlockSpec((tm, D_MODEL), lambda i, k: (i, 0))
    vec = pl.BlockSpec((None, 1, D_MODEL), lambda i, k: (layer, 0, 0))
    return pl.pallas_call(
        _mlp_kernel,
        grid=(n // tm, MLP_HIDDEN // th),
        in_specs=[row,
                  pl.BlockSpec((None, D_MODEL, th), lambda i, k: (layer, 0, k)),
                  pl.BlockSpec((None, th, D_MODEL), lambda i, k: (layer, k, 0)),
                  row, vec, vec],
        out_specs=[row, row],
        out_shape=[jax.ShapeDtypeStruct((n, D_MODEL), F32), jax.ShapeDtypeStruct((n, D_MODEL), BF16)],
        compiler_params=_cparams(("parallel", "arbitrary")),
        name="mlp",
    )(xb, w1, w2, x, ln_g, ln_b)


def _reorder_w_in(w):
    glow0 = BRANCH_WIDTH + 2 * GLA_KEY + BRANCH_WIDTH
    glow1 = glow0 + GLA_GATE_RANK
    pad = jnp.zeros(w.shape[:-1] + (PROJ_WIDTH - COL_GLOW - GLA_GATE_RANK,), w.dtype)
    return jnp.concatenate([w[..., :glow0], w[..., glow1:], w[..., glow0:glow1], pad], axis=-1).astype(BF16)


def _row(v):
    return v.astype(F32)[:, None, :]


def kernel(x, w_in, s5_lam_re, s5_lam_im, s5_log_dt, s5_b_re, s5_b_im, s5_c_re, s5_c_im, s5_d, s5_w_glu, s5_b_glu, gla_w_gate, gla_b_gate, gla_norm_w, hgrn_lb_logits, hgrn_norm_w, w_up, w_out, ln1_g, ln1_b, ln2_g, ln2_b, w_mlp_in, w_mlp_out):
    bsz, seq, d = x.shape
    n = bsz * seq
    depth = w_in.shape[0]
    p = jax.nn.softmax(hgrn_lb_logits.astype(F32), axis=0)
    lb = _row(jnp.cumsum(p, axis=0) - p[0])
    w_in_b = _reorder_w_in(w_in)
    s5w = jax.vmap(functools.partial(_s5_weights, folds_per_seq=seq // S5_FOLD))(
        s5_lam_re, s5_lam_im, s5_log_dt, s5_b_re, s5_b_im, s5_c_re, s5_c_im, s5_d)
    w_glu_b = s5_w_glu.astype(BF16)
    b_glu = _row(s5_b_glu)
    wg = jnp.concatenate([gla_w_gate, jnp.zeros((depth, LANES - GLA_GATE_RANK, GLA_KEY), gla_w_gate.dtype)],
                         axis=1).astype(BF16)
    bg, gla_nw, hgrn_nw = _row(gla_b_gate), _row(gla_norm_w), _row(hgrn_norm_w)
    w_up_b, w_out_b = w_up.astype(BF16), w_out.astype(BF16)
    w1_b, w2_b = w_mlp_in.astype(BF16), w_mlp_out.astype(BF16)
    g1, b1, g2, b2 = _row(ln1_g), _row(ln1_b), _row(ln2_g), _row(ln2_b)

    xf = x.reshape(n, d).astype(F32)
    xb = xf.astype(BF16)
    for l in range(depth):
        h = _proj(xb, w_in_b, l)
        ya = _glu(_s5_scan(h, bsz, seq, s5w, l), w_glu_b, b_glu, l)
        yb = _gla(h, bsz, seq, wg, bg, gla_nw, l)
        yc = _hgrn(h, bsz, seq, lb, hgrn_nw, l)
        xf, xb = _mixer_out(ya, yb, yc, h, w_up_b, w_out_b, xf, g1, b1, l)
        xf, xb = _mlp(xb, w1_b, w2_b, xf, g2, b2, l)
    return xf.reshape(bsz, seq, d).astype(x.dtype)
```

```python
import functools
import math

import jax
import jax.numpy as jnp
from jax import lax
from jax.experimental import pallas as pl
from jax.experimental.pallas import tpu as pltpu

F32 = jnp.float32
BF16 = jnp.bfloat16

D_MODEL = 2048
DEPTH = 4
N_BRANCH = 3
BRANCH_WIDTH = D_MODEL // 2
S5_GROUP = 16
S5_GROUPS = BRANCH_WIDTH // S5_GROUP
S5_STATE = 64
S5_MIN_DECAY = 1e-4
GLA_HEADS = 4
GLA_DV = BRANCH_WIDTH // GLA_HEADS
GLA_DK = GLA_DV // 2
GLA_KEY = GLA_HEADS * GLA_DK
GLA_GATE_RANK = 16
GLA_GATE_TAU = 16.0
HGRN_EXPAND = 128
HGRN_HEADS = BRANCH_WIDTH // HGRN_EXPAND
HGRN_DV = BRANCH_WIDTH // HGRN_HEADS
HGRN_KEY = HGRN_HEADS * HGRN_EXPAND
MLP_HIDDEN = 4 * D_MODEL
DN_ALPHA = (2 * DEPTH) ** 0.25
LN_EPS = 1e-5
RMS_EPS = 1e-6

LANES = 128
SUBLANES = 8
VMEM_LIMIT_BYTES = 56 * 1024 * 1024

COL_U = 0
COL_QB = COL_U + BRANCH_WIDTH
COL_KB = COL_QB + GLA_KEY
COL_VB = COL_KB + GLA_KEY
COL_GATEB = COL_VB + BRANCH_WIDTH
COL_QC = COL_GATEB + BRANCH_WIDTH
COL_FC = COL_QC + HGRN_KEY
COL_IC = COL_FC + HGRN_KEY
COL_GATEC = COL_IC + BRANCH_WIDTH
COL_MERGE = COL_GATEC + BRANCH_WIDTH
PROJ_WIDTH = COL_MERGE + N_BRANCH * D_MODEL
W_IN_GLOW = BRANCH_WIDTH + 2 * GLA_KEY + BRANCH_WIDTH

S5_FOLD = SUBLANES
S5_TILE_GROUPS = LANES // S5_GROUP
S5_TILES = S5_GROUPS // S5_TILE_GROUPS
S5_TILE_STATE = S5_TILE_GROUPS * S5_STATE
S5_ROWS = S5_FOLD * LANES
ATT_CHUNK = 128
ROW_TILE = 512
PROJ_TM = 2048
PROJ_TN = 1024
MERGE_TILE = 512
HID_TILE = 1024


def _cparams(sem):
    return pltpu.CompilerParams(dimension_semantics=sem, vmem_limit_bytes=VMEM_LIMIT_BYTES)


def _dot(a, b):
    return jnp.dot(a, b, preferred_element_type=F32)


def _dot_nt(a, b):
    return lax.dot_general(a, b, (((1,), (1,)), ((), ())), preferred_element_type=F32)


def _dot_tn(a, b):
    return lax.dot_general(a, b, (((0,), (0,)), ((), ())), preferred_element_type=F32)


def _sigmoid(x):
    return 1.0 / (1.0 + jnp.exp(-x))


def _layer_norm(v, g, b):
    mu = jnp.mean(v, axis=-1, keepdims=True)
    xc = v - mu
    var = jnp.mean(xc * xc, axis=-1, keepdims=True)
    return xc * lax.rsqrt(var + LN_EPS) * g + b


def _proj_kernel(x_ref, w_ref, o_ref):
    o_ref[...] = _dot(x_ref[...], w_ref[...])


def _proj(xb, w, layer):
    n, k = xb.shape
    width = w.shape[2]
    tm = min(PROJ_TM, n)
    return pl.pallas_call(
        _proj_kernel,
        grid=(n // tm, width // PROJ_TN),
        in_specs=[pl.BlockSpec((tm, k), lambda i, j: (i, 0)),
                  pl.BlockSpec((None, k, PROJ_TN), lambda i, j: (layer, 0, j))],
        out_specs=pl.BlockSpec((tm, PROJ_TN), lambda i, j: (i, j)),
        out_shape=jax.ShapeDtypeStruct((n, width), F32),
        compiler_params=_cparams(("parallel", "arbitrary")),
        name="proj",
    )(xb, w)


def _s5_expand(ky_ref, kv_ref, kq_ref, my_ref, mv_ref, mq_ref):
    kk = lax.broadcasted_iota(jnp.int32, (LANES, S5_ROWS), 0)
    cc = lax.broadcasted_iota(jnp.int32, (LANES, S5_ROWS), 1)
    e_tc = jnp.where(((kk >> 4) == (cc >> 7)) & ((kk & 15) == (cc & 15)), 1.0, 0.0).astype(BF16)
    e_rp = jnp.where(((kk >> 6) == (cc >> 9)) & ((kk & 63) == (cc & 63)), 1.0, 0.0).astype(BF16)
    rr = lax.broadcasted_iota(jnp.int32, (S5_ROWS, S5_ROWS), 0)
    cc = lax.broadcasted_iota(jnp.int32, (S5_ROWS, S5_ROWS), 1)
    g_jgd, g_rgp = (rr >> 4) & 7, (rr >> 6) & 7
    h_thc, h_rhp = (cc >> 4) & 7, (cc >> 6) & 7
    my_ref[...] = jnp.where(g_jgd == h_thc, _dot(ky_ref[...], e_tc), 0.0).astype(BF16)
    mv_ref[...] = jnp.where(g_jgd == h_rhp, _dot(kv_ref[...], e_rp), 0.0).astype(BF16)
    mq_ref[...] = jnp.where(g_rgp == h_thc, _dot(kq_ref[...], e_tc), 0.0).astype(BF16)


def _s5_kernel(u_ref, ky_ref, kv_ref, kq_ref, d_ref, apr_ref, api_ref, y_ref, my_ref, mv_ref, mq_ref):
    @pl.when(pl.program_id(1) == 0)
    def _():
        _s5_expand(ky_ref, kv_ref, kq_ref, my_ref, mv_ref, mq_ref)

    folds = u_ref.shape[0] // S5_FOLD
    us = [u_ref[pl.ds(t, folds, stride=S5_FOLD), :] for t in range(S5_FOLD)]
    ucat = jnp.concatenate([u.astype(BF16) for u in us], axis=1)
    y_intra = _dot(ucat, my_ref[...])
    r = _dot(ucat, mv_ref[...])
    s_re = r[:, :S5_TILE_STATE]
    s_im = r[:, S5_TILE_STATE:]
    fold = lax.broadcasted_iota(jnp.int32, (folds, S5_TILE_STATE), 0)
    step = 0
    dist = 1
    while dist < folds:
        ok = fold >= dist
        sh_re = jnp.where(ok, pltpu.roll(s_re, dist, axis=0), 0.0)
        sh_im = jnp.where(ok, pltpu.roll(s_im, dist, axis=0), 0.0)
        ar = apr_ref[step:step + 1, :]
        ai = api_ref[step:step + 1, :]
        s_re, s_im = s_re + ar * sh_re - ai * sh_im, s_im + ar * sh_im + ai * sh_re
        step += 1
        dist *= 2
    ok = fold >= 1
    p_re = jnp.where(ok, pltpu.roll(s_re, 1, axis=0), 0.0)
    p_im = jnp.where(ok, pltpu.roll(s_im, 1, axis=0), 0.0)
    prev = jnp.concatenate([p_re, p_im], axis=1).astype(BF16)
    y = y_intra + _dot(prev, mq_ref[...])
    for t in range(S5_FOLD):
        y_ref[pl.ds(t, folds, stride=S5_FOLD), :] = y[:, t * LANES:(t + 1) * LANES] + d_ref[...] * us[t]


def _s5_scan(h, bsz, seq, s5w, layer):
    k_y, k_v, k_q, d_t, ap_re, ap_im = s5w
    nsteps = ap_re.shape[2]
    tile = lambda j, b: (layer, j, 0, 0)
    compact = pl.BlockSpec((None, None, S5_ROWS, LANES), tile)
    return pl.pallas_call(
        _s5_kernel,
        grid=(S5_TILES, bsz),
        in_specs=[pl.BlockSpec((seq, LANES), lambda j, b: (b, COL_U // LANES + j)),
                  compact, compact, compact,
                  pl.BlockSpec((None, None, 1, LANES), tile),
                  pl.BlockSpec((None, None, nsteps, S5_TILE_STATE), tile),
                  pl.BlockSpec((None, None, nsteps, S5_TILE_STATE), tile)],
        out_specs=pl.BlockSpec((seq, LANES), lambda j, b: (b, j)),
        out_shape=jax.ShapeDtypeStruct((bsz * seq, BRANCH_WIDTH), F32),
        scratch_shapes=[pltpu.VMEM((S5_ROWS, S5_ROWS), BF16)] * 3,
        compiler_params=_cparams(("parallel", "arbitrary")),
        name="s5_scan",
    )(h, k_y, k_v, k_q, d_t, ap_re, ap_im)


def _s5_weights(lam_re, lam_im, log_dt, b_re, b_im, c_re, c_im, d_skip, folds_per_seq):
    hp = lax.Precision.HIGHEST
    t_fold, g_n, p_n, c_n, tg = S5_FOLD, S5_GROUPS, S5_STATE, S5_GROUP, S5_TILE_GROUPS
    tiles = g_n // tg
    lr = jnp.minimum(lam_re.astype(F32), -S5_MIN_DECAY)
    li = lam_im.astype(F32)
    dt = jnp.exp(log_dt.astype(F32))[:, None]
    mag = jnp.exp(lr * dt)
    abar_re, abar_im = mag * jnp.cos(li * dt), mag * jnp.sin(li * dt)
    den = lr * lr + li * li
    fac_re = ((abar_re - 1.0) * lr + abar_im * li) / den
    fac_im = (abar_im * lr - (abar_re - 1.0) * li) / den
    br, bi = b_re.astype(F32), b_im.astype(F32)
    bbar_re = fac_re[..., None] * br - fac_im[..., None] * bi
    bbar_im = fac_re[..., None] * bi + fac_im[..., None] * br

    def apow(n):
        nn = jnp.asarray(n, F32)[:, None, None]
        m = jnp.exp(nn * (lr * dt)[None])
        return m * jnp.cos(nn * (li * dt)[None]), m * jnp.sin(nn * (li * dt)[None])

    pr, pi = apow(jnp.arange(t_fold + 1))
    cr, ci = c_re.astype(F32), c_im.astype(F32)
    ca_re = cr[None] * pr[:, :, None, :] - ci[None] * pi[:, :, None, :]
    ca_im = cr[None] * pi[:, :, None, :] + ci[None] * pr[:, :, None, :]
    k_tau = (jnp.einsum('tgcp,gpd->tgcd', ca_re[:t_fold], bbar_re, precision=hp)
             - jnp.einsum('tgcp,gpd->tgcd', ca_im[:t_fold], bbar_im, precision=hp))
    jj = jnp.arange(t_fold)[:, None]
    tt = jnp.arange(t_fold)[None, :]
    lag = jnp.clip(tt - jj, 0, t_fold - 1)
    toe = jnp.where((tt >= jj)[:, :, None, None, None], k_tau[lag], 0.0)
    k_y = (toe.reshape(t_fold, t_fold, tiles, tg, c_n, c_n).transpose(2, 0, 3, 5, 1, 4)
           .reshape(tiles, S5_ROWS, LANES))
    rev = t_fold - 1 - jnp.arange(t_fold)
    prj, pij = pr[rev], pi[rev]
    pv = jnp.stack([prj[..., None] * bbar_re[None] - pij[..., None] * bbar_im[None],
                    prj[..., None] * bbar_im[None] + pij[..., None] * bbar_re[None]])
    k_v = (pv.reshape(2, t_fold, tiles, tg, p_n, c_n).transpose(2, 1, 3, 5, 0, 4)
           .reshape(tiles, S5_ROWS, LANES))
    qa = jnp.stack([ca_re[1:], -ca_im[1:]])
    k_q = (qa.reshape(2, t_fold, tiles, tg, c_n, p_n).transpose(2, 0, 3, 5, 1, 4)
           .reshape(tiles, S5_ROWS, LANES))
    d_t = d_skip.astype(F32).reshape(tiles, 1, LANES)
    nsteps = max(int(math.log2(folds_per_seq)), 1)
    sr, si = apow(t_fold * (2 ** jnp.arange(nsteps)))
    ap_re = sr.reshape(nsteps, tiles, tg * p_n).transpose(1, 0, 2)
    ap_im = si.reshape(nsteps, tiles, tg * p_n).transpose(1, 0, 2)
    return k_y.astype(BF16), k_v.astype(BF16), k_q.astype(BF16), d_t, ap_re, ap_im


def _glu_kernel(y_ref, w_ref, b_ref, o_ref):
    y = y_ref[...]
    z = 0.5 * y * (1.0 + jnp.tanh(math.sqrt(2.0 / math.pi) * (y + 0.044715 * (y * y * y))))
    a = _dot(z.astype(BF16), w_ref[...]) + b_ref[...]
    o_ref[...] = (z * _sigmoid(a)).astype(o_ref.dtype)


def _glu(y, w, b, layer):
    n, width = y.shape
    tm = min(ROW_TILE, n)
    return pl.pallas_call(
        _glu_kernel,
        grid=(n // tm,),
        in_specs=[pl.BlockSpec((tm, width), lambda i: (i, 0)),
                  pl.BlockSpec((None, width, width), lambda i: (layer, 0, 0)),
                  pl.BlockSpec((None, 1, width), lambda i: (layer, 0, 0))],
        out_specs=pl.BlockSpec((tm, width), lambda i: (i, 0)),
        out_shape=jax.ShapeDtypeStruct((n, width), BF16),
        compiler_params=_cparams(("parallel",)),
        name="s5_glu",
    )(y, w, b)


def _gated_chunk(q, k, v, g, st_ref, head):
    c = q.shape[0]
    row = lax.broadcasted_iota(jnp.int32, (c, LANES), 0)
    ri = lax.broadcasted_iota(jnp.int32, (c, c), 0)
    ci = lax.broadcasted_iota(jnp.int32, (c, c), 1)
    xr = jnp.where(ri > ci, ri ^ ci, 0)
    gc = g
    dist = 1
    while dist < c:
        gc = gc + jnp.where(row >= dist, pltpu.roll(gc, dist, axis=0), 0.0)
        dist *= 2
    a = jnp.where(ri == ci, _dot_nt(q.astype(BF16), k.astype(BF16)), 0.0)
    end = gc
    half = 1
    while half < c:
        first = (row & half) == 0
        ref = jnp.where(first, end, pltpu.roll(end, half, axis=0))
        e = jnp.exp(-jnp.abs(gc - ref))
        qt = (jnp.where(first, 0.0, e) * q).astype(BF16)
        kt = (jnp.where(first, e, 0.0) * k).astype(BF16)
        a = jnp.where(xr >= half, _dot_nt(qt, kt), a)
        end = jnp.where(first, pltpu.roll(end, c - half, axis=0), end)
        half *= 2
    st = st_ref[head]
    vb = v.astype(BF16)
    o = _dot(a.astype(BF16), vb) + _dot_nt((q * jnp.exp(gc)).astype(BF16), st.astype(BF16))
    kd = (k * jnp.exp(end - gc)).astype(BF16)
    st_ref[head] = st * jnp.exp(end[0:1, :]) + _dot_tn(vb, kd)
    return o


def _gla_kernel(q_ref, k_ref, v_ref, gate_ref, xb_ref, wl_ref, wg_ref, bg_ref, nw_ref, o_ref, st_ref):
    @pl.when(pl.program_id(1) == 0)
    def _():
        st_ref[...] = jnp.zeros_like(st_ref)

    glow = _dot(xb_ref[...], wl_ref[...]).astype(BF16)
    for hd in range(GLA_HEADS):
        ks = slice(hd * GLA_DK, (hd + 1) * GLA_DK)
        vs = slice(hd * GLA_DV, (hd + 1) * GLA_DV)
        z = _dot(glow, wg_ref[:, ks]) + bg_ref[:, ks]
        g = (jnp.minimum(z, 0.0) - jnp.log1p(jnp.exp(-jnp.abs(z)))) / GLA_GATE_TAU
        q = q_ref[:, ks] * (GLA_DK ** -0.5)
        o = _gated_chunk(q, k_ref[:, ks], v_ref[:, vs], g, st_ref, hd)
        o = o * lax.rsqrt(jnp.mean(o * o, axis=-1, keepdims=True) + RMS_EPS) * nw_ref[...]
        gate = gate_ref[:, vs]
        o_ref[:, vs] = (o * (gate * _sigmoid(gate))).astype(o_ref.dtype)


def _att_specs(seq):
    c = min(ATT_CHUNK, seq)
    nc = seq // c

    def col(off, width):
        return pl.BlockSpec((c, width), lambda b, i: (b * nc + i, off // width))

    return c, nc, col


def _gla(h, xb, bsz, seq, wl, wg, bg, nw, layer):
    c, nc, col = _att_specs(seq)
    vec = lambda b, i: (layer, 0, 0)
    return pl.pallas_call(
        _gla_kernel,
        grid=(bsz, nc),
        in_specs=[col(COL_QB, GLA_KEY), col(COL_KB, GLA_KEY), col(COL_VB, BRANCH_WIDTH),
                  col(COL_GATEB, BRANCH_WIDTH),
                  pl.BlockSpec((c, D_MODEL), lambda b, i: (b * nc + i, 0)),
                  pl.BlockSpec((None, D_MODEL, LANES), vec),
                  pl.BlockSpec((None, LANES, GLA_KEY), vec),
                  pl.BlockSpec((None, 1, GLA_KEY), vec),
                  pl.BlockSpec((None, 1, GLA_DV), vec)],
        out_specs=pl.BlockSpec((c, BRANCH_WIDTH), lambda b, i: (b * nc + i, 0)),
        out_shape=jax.ShapeDtypeStruct((bsz * seq, BRANCH_WIDTH), BF16),
        scratch_shapes=[pltpu.VMEM((GLA_HEADS, GLA_DV, GLA_DK), F32)],
        compiler_params=_cparams(("parallel", "arbitrary")),
        name="gla",
    )(h, h, h, h, xb, wl, wg, bg, nw)


def _hgrn_kernel(q_ref, f_ref, i_ref, gate_ref, lb_ref, nw_ref, o_ref, st_ref):
    @pl.when(pl.program_id(1) == 0)
    def _():
        st_ref[...] = jnp.zeros_like(st_ref)

    for hd in range(HGRN_HEADS):
        ks = slice(hd * HGRN_EXPAND, (hd + 1) * HGRN_EXPAND)
        vs = slice(hd * HGRN_DV, (hd + 1) * HGRN_DV)
        lb = lb_ref[:, ks]
        f = lb + (1.0 - lb) * _sigmoid(f_ref[:, ks])
        qr = q_ref[:, ks]
        o = _gated_chunk(qr * _sigmoid(qr), 1.0 - f, i_ref[:, vs], jnp.log(f), st_ref, hd)
        o = o * _sigmoid(gate_ref[:, vs])
        o = o * lax.rsqrt(jnp.mean(o * o, axis=-1, keepdims=True) + RMS_EPS) * nw_ref[...]
        o_ref[:, vs] = o.astype(o_ref.dtype)


def _hgrn(h, bsz, seq, lb, nw, layer):
    c, nc, col = _att_specs(seq)
    vec = lambda b, i: (layer, 0, 0)
    return pl.pallas_call(
        _hgrn_kernel,
        grid=(bsz, nc),
        in_specs=[col(COL_QC, HGRN_KEY), col(COL_FC, HGRN_KEY), col(COL_IC, BRANCH_WIDTH),
                  col(COL_GATEC, BRANCH_WIDTH),
                  pl.BlockSpec((None, 1, HGRN_KEY), vec),
                  pl.BlockSpec((None, 1, HGRN_DV), vec)],
        out_specs=pl.BlockSpec((c, BRANCH_WIDTH), lambda b, i: (b * nc + i, 0)),
        out_shape=jax.ShapeDtypeStruct((bsz * seq, BRANCH_WIDTH), BF16),
        scratch_shapes=[pltpu.VMEM((HGRN_HEADS, HGRN_DV, HGRN_EXPAND), F32)],
        compiler_params=_cparams(("parallel", "arbitrary")),
        name="hgrn",
    )(h, h, h, h, lb, nw)


def _zero_first_step(o_ref):
    @pl.when(pl.program_id(1) == 0)
    def _():
        o_ref[...] = jnp.zeros_like(o_ref)


def _norm_last_step(x_ref, lg_ref, lb_ref, o_ref, ob_ref):
    @pl.when(pl.program_id(1) == pl.num_programs(1) - 1)
    def _():
        xn = _layer_norm(DN_ALPHA * x_ref[...] + o_ref[...], lg_ref[...], lb_ref[...])
        o_ref[...] = xn
        ob_ref[...] = xn.astype(BF16)


def _mixer_out_kernel(ya_ref, yb_ref, yc_ref, ga_ref, gb_ref, gc_ref, wup_ref, wout_ref,
                      x_ref, lg_ref, lb_ref, o_ref, ob_ref):
    _zero_first_step(o_ref)
    merged = (_sigmoid(ga_ref[...]) * _dot(ya_ref[...], wup_ref[0])
              + _sigmoid(gb_ref[...]) * _dot(yb_ref[...], wup_ref[1])
              + _sigmoid(gc_ref[...]) * _dot(yc_ref[...], wup_ref[2]))
    o_ref[...] += _dot(merged.astype(BF16), wout_ref[...])
    _norm_last_step(x_ref, lg_ref, lb_ref, o_ref, ob_ref)


def _mixer_out(ya, yb, yc, h, wup, wout, x, ln_g, ln_b, layer):
    n = x.shape[0]
    tm = min(ROW_TILE, n)
    tk = MERGE_TILE

    def gate(branch):
        off = (COL_MERGE + branch * D_MODEL) // tk
        return pl.BlockSpec((tm, tk), lambda i, k: (i, off + k))

    ybs = pl.BlockSpec((tm, BRANCH_WIDTH), lambda i, k: (i, 0))
    row = pl.BlockSpec((tm, D_MODEL), lambda i, k: (i, 0))
    vec = pl.BlockSpec((None, 1, D_MODEL), lambda i, k: (layer, 0, 0))
    return pl.pallas_call(
        _mixer_out_kernel,
        grid=(n // tm, D_MODEL // tk),
        in_specs=[ybs, ybs, ybs, gate(0), gate(1), gate(2),
                  pl.BlockSpec((None, N_BRANCH, BRANCH_WIDTH, tk), lambda i, k: (layer, 0, 0, k)),
                  pl.BlockSpec((None, tk, D_MODEL), lambda i, k: (layer, k, 0)),
                  row, vec, vec],
        out_specs=[row, row],
        out_shape=[jax.ShapeDtypeStruct((n, D_MODEL), F32), jax.ShapeDtypeStruct((n, D_MODEL), BF16)],
        compiler_params=_cparams(("parallel", "arbitrary")),
        name="mixer_out",
    )(ya, yb, yc, h, h, h, wup, wout, x, ln_g, ln_b)


def _mlp_kernel(xb_ref, w1_ref, w2_ref, x_ref, lg_ref, lb_ref, o_ref, ob_ref):
    _zero_first_step(o_ref)
    hid = jnp.maximum(_dot(xb_ref[...], w1_ref[...]), 0.0)
    o_ref[...] += _dot((hid * hid).astype(BF16), w2_ref[...])
    _norm_last_step(x_ref, lg_ref, lb_ref, o_ref, ob_ref)


def _mlp(xb, w1, w2, x, ln_g, ln_b, layer):
    n = x.shape[0]
    tm = min(ROW_TILE, n)
    th = HID_TILE
    row = pl.BlockSpec((tm, D_MODEL), lambda i, k: (i, 0))
    vec = pl.BlockSpec((None, 1, D_MODEL), lambda i, k: (layer, 0, 0))
    return pl.pallas_call(
        _mlp_kernel,
        grid=(n // tm, MLP_HIDDEN // th),
        in_specs=[row,
                  pl.BlockSpec((None, D_MODEL, th), lambda i, k: (layer, 0, k)),
                  pl.BlockSpec((None, th, D_MODEL), lambda i, k: (layer, k, 0)),
                  row, vec, vec],
        out_specs=[row, row],
        out_shape=[jax.ShapeDtypeStruct((n, D_MODEL), F32), jax.ShapeDtypeStruct((n, D_MODEL), BF16)],
        compiler_params=_cparams(("parallel", "arbitrary")),
        name="mlp",
    )(xb, w1, w2, x, ln_g, ln_b)


def _split_w_in(w):
    glow1 = W_IN_GLOW + GLA_GATE_RANK
    main = jnp.concatenate([w[..., :W_IN_GLOW], w[..., glow1:]], axis=-1).astype(BF16)
    pad = jnp.zeros(w.shape[:-1] + (LANES - GLA_GATE_RANK,), w.dtype)
    glow = jnp.concatenate([w[..., W_IN_GLOW:glow1], pad], axis=-1).astype(BF16)
    return main, glow


def _row(v):
    return v.astype(F32)[:, None, :]


def kernel(x, w_in, s5_lam_re, s5_lam_im, s5_log_dt, s5_b_re, s5_b_im, s5_c_re, s5_c_im, s5_d, s5_w_glu, s5_b_glu, gla_w_gate, gla_b_gate, gla_norm_w, hgrn_lb_logits, hgrn_norm_w, w_up, w_out, ln1_g, ln1_b, ln2_g, ln2_b, w_mlp_in, w_mlp_out):
    bsz, seq, d = x.shape
    n = bsz * seq
    depth = w_in.shape[0]
    p = jax.nn.softmax(hgrn_lb_logits.astype(F32), axis=0)
    lb = _row(jnp.cumsum(p, axis=0) - p[0])
    w_in_b, w_glow_b = _split_w_in(w_in)
    s5w = jax.vmap(functools.partial(_s5_weights, folds_per_seq=seq // S5_FOLD))(
        s5_lam_re, s5_lam_im, s5_log_dt, s5_b_re, s5_b_im, s5_c_re, s5_c_im, s5_d)
    w_glu_b = s5_w_glu.astype(BF16)
    b_glu = _row(s5_b_glu)
    wg = jnp.concatenate([gla_w_gate, jnp.zeros((depth, LANES - GLA_GATE_RANK, GLA_KEY), gla_w_gate.dtype)],
                         axis=1).astype(BF16)
    bg, gla_nw, hgrn_nw = _row(gla_b_gate), _row(gla_norm_w), _row(hgrn_norm_w)
    w_up_b, w_out_b = w_up.astype(BF16), w_out.astype(BF16)
    w1_b, w2_b = w_mlp_in.astype(BF16), w_mlp_out.astype(BF16)
    g1, b1, g2, b2 = _row(ln1_g), _row(ln1_b), _row(ln2_g), _row(ln2_b)

    xf = x.reshape(n, d).astype(F32)
    xb = xf.astype(BF16)
    for l in range(depth):
        h = _proj(xb, w_in_b, l)
        ya = _glu(_s5_scan(h, bsz, seq, s5w, l), w_glu_b, b_glu, l)
        yb = _gla(h, xb, bsz, seq, w_glow_b, wg, bg, gla_nw, l)
        yc = _hgrn(h, bsz, seq, lb, hgrn_nw, l)
        xf, xb = _mixer_out(ya, yb, yc, h, w_up_b, w_out_b, xf, g1, b1, l)
        xf, xb = _mlp(xb, w1_b, w2_b, xf, g2, b2, l)
    return xf.reshape(bsz, seq, d).astype(x.dtype)
```

```python
import functools
import math

import jax
import jax.numpy as jnp
from jax import lax
from jax.experimental import pallas as pl
from jax.experimental.pallas import tpu as pltpu

F32 = jnp.float32
BF16 = jnp.bfloat16

D_MODEL = 2048
DEPTH = 4
N_BRANCH = 3
BRANCH_WIDTH = D_MODEL // 2
S5_GROUP = 16
S5_GROUPS = BRANCH_WIDTH // S5_GROUP
S5_STATE = 64
S5_MIN_DECAY = 1e-4
GLA_HEADS = 4
GLA_DV = BRANCH_WIDTH // GLA_HEADS
GLA_DK = GLA_DV // 2
GLA_KEY = GLA_HEADS * GLA_DK
GLA_GATE_RANK = 16
GLA_GATE_TAU = 16.0
HGRN_EXPAND = 128
HGRN_HEADS = BRANCH_WIDTH // HGRN_EXPAND
HGRN_DV = BRANCH_WIDTH // HGRN_HEADS
HGRN_KEY = HGRN_HEADS * HGRN_EXPAND
MLP_HIDDEN = 4 * D_MODEL
DN_ALPHA = (2 * DEPTH) ** 0.25
LN_EPS = 1e-5
RMS_EPS = 1e-6
LOG2E = 1.4426950408889634

LANES = 128
SUBLANES = 8
VMEM_LIMIT_BYTES = 56 * 1024 * 1024

COL_U = 0
COL_QB = COL_U + BRANCH_WIDTH
COL_KB = COL_QB + GLA_KEY
COL_VB = COL_KB + GLA_KEY
COL_GATEB = COL_VB + BRANCH_WIDTH
COL_QC = COL_GATEB + BRANCH_WIDTH
COL_FC = COL_QC + HGRN_KEY
COL_IC = COL_FC + HGRN_KEY
COL_GATEC = COL_IC + BRANCH_WIDTH
COL_MERGE = COL_GATEC + BRANCH_WIDTH
PROJ_WIDTH = COL_MERGE + N_BRANCH * D_MODEL
W_IN_GLOW = BRANCH_WIDTH + 2 * GLA_KEY + BRANCH_WIDTH

S5_FOLD = SUBLANES
S5_TILE_GROUPS = LANES // S5_GROUP
S5_TILES = S5_GROUPS // S5_TILE_GROUPS
S5_TILE_STATE = S5_TILE_GROUPS * S5_STATE
S5_ROWS = S5_FOLD * LANES
ATT_CHUNK = 128
ROW_TILE = 512
PROJ_TM = 2048
PROJ_TN = 1024
MERGE_TM = 1024
MERGE_TILE = 512
HID_TILE = 1024


def _cparams(sem):
    return pltpu.CompilerParams(dimension_semantics=sem, vmem_limit_bytes=VMEM_LIMIT_BYTES)


def _dot(a, b):
    return jnp.dot(a, b, preferred_element_type=F32)


def _dot_nt(a, b):
    return lax.dot_general(a, b, (((1,), (1,)), ((), ())), preferred_element_type=F32)


def _dot_tn(a, b):
    return lax.dot_general(a, b, (((0,), (0,)), ((), ())), preferred_element_type=F32)


def _sigmoid(x):
    return 1.0 / (1.0 + jnp.exp2(x * -LOG2E))


def _layer_norm(v, g, b):
    mu = jnp.mean(v, axis=-1, keepdims=True)
    xc = v - mu
    var = jnp.mean(xc * xc, axis=-1, keepdims=True)
    return xc * lax.rsqrt(var + LN_EPS) * g + b


def _proj_kernel(x_ref, w_ref, o_ref):
    o_ref[...] = _dot(x_ref[...], w_ref[...])


def _proj(xb, w, layer):
    n, k = xb.shape
    width = w.shape[2]
    tm = min(PROJ_TM, n)
    return pl.pallas_call(
        _proj_kernel,
        grid=(n // tm, width // PROJ_TN),
        in_specs=[pl.BlockSpec((tm, k), lambda i, j: (i, 0)),
                  pl.BlockSpec((None, k, PROJ_TN), lambda i, j: (layer, 0, j))],
        out_specs=pl.BlockSpec((tm, PROJ_TN), lambda i, j: (i, j)),
        out_shape=jax.ShapeDtypeStruct((n, width), F32),
        compiler_params=_cparams(("parallel", "arbitrary")),
        name="proj",
    )(xb, w)


def _s5_expand(ky_ref, kv_ref, kq_ref, my_ref, mv_ref, mq_ref):
    kk = lax.broadcasted_iota(jnp.int32, (LANES, S5_ROWS), 0)
    cc = lax.broadcasted_iota(jnp.int32, (LANES, S5_ROWS), 1)
    e_tc = jnp.where(((kk >> 4) == (cc >> 7)) & ((kk & 15) == (cc & 15)), 1.0, 0.0).astype(BF16)
    e_rp = jnp.where(((kk >> 6) == (cc >> 9)) & ((kk & 63) == (cc & 63)), 1.0, 0.0).astype(BF16)
    rr = lax.broadcasted_iota(jnp.int32, (S5_ROWS, S5_ROWS), 0)
    cc = lax.broadcasted_iota(jnp.int32, (S5_ROWS, S5_ROWS), 1)
    g_jgd, g_rgp = (rr >> 4) & 7, (rr >> 6) & 7
    h_thc, h_rhp = (cc >> 4) & 7, (cc >> 6) & 7
    my_ref[...] = jnp.where(g_jgd == h_thc, _dot(ky_ref[...], e_tc), 0.0).astype(BF16)
    mv_ref[...] = jnp.where(g_jgd == h_rhp, _dot(kv_ref[...], e_rp), 0.0).astype(BF16)
    mq_ref[...] = jnp.where(g_rgp == h_thc, _dot(kq_ref[...], e_tc), 0.0).astype(BF16)


def _s5_kernel(u_ref, ky_ref, kv_ref, kq_ref, d_ref, apr_ref, api_ref, y_ref, my_ref, mv_ref, mq_ref):
    @pl.when(pl.program_id(1) == 0)
    def _():
        _s5_expand(ky_ref, kv_ref, kq_ref, my_ref, mv_ref, mq_ref)

    folds = u_ref.shape[0] // S5_FOLD
    us = [u_ref[pl.ds(t, folds, stride=S5_FOLD), :] for t in range(S5_FOLD)]
    ucat = jnp.concatenate([u.astype(BF16) for u in us], axis=1)
    y_intra = _dot(ucat, my_ref[...])
    r = _dot(ucat, mv_ref[...])
    s_re = r[:, :S5_TILE_STATE]
    s_im = r[:, S5_TILE_STATE:]
    fold = lax.broadcasted_iota(jnp.int32, (folds, S5_TILE_STATE), 0)
    step = 0
    dist = 1
    while dist < folds:
        ok = fold >= dist
        sh_re = jnp.where(ok, pltpu.roll(s_re, dist, axis=0), 0.0)
        sh_im = jnp.where(ok, pltpu.roll(s_im, dist, axis=0), 0.0)
        ar = apr_ref[step:step + 1, :]
        ai = api_ref[step:step + 1, :]
        s_re, s_im = s_re + ar * sh_re - ai * sh_im, s_im + ar * sh_im + ai * sh_re
        step += 1
        dist *= 2
    ok = fold >= 1
    p_re = jnp.where(ok, pltpu.roll(s_re, 1, axis=0), 0.0)
    p_im = jnp.where(ok, pltpu.roll(s_im, 1, axis=0), 0.0)
    prev = jnp.concatenate([p_re, p_im], axis=1).astype(BF16)
    y = y_intra + _dot(prev, mq_ref[...])
    for t in range(S5_FOLD):
        y_ref[pl.ds(t, folds, stride=S5_FOLD), :] = y[:, t * LANES:(t + 1) * LANES] + d_ref[...] * us[t]


def _s5_scan(h, bsz, seq, s5w, layer):
    k_y, k_v, k_q, d_t, ap_re, ap_im = s5w
    nsteps = ap_re.shape[2]
    tile = lambda j, b: (layer, j, 0, 0)
    compact = pl.BlockSpec((None, None, S5_ROWS, LANES), tile)
    return pl.pallas_call(
        _s5_kernel,
        grid=(S5_TILES, bsz),
        in_specs=[pl.BlockSpec((seq, LANES), lambda j, b: (b, COL_U // LANES + j)),
                  compact, compact, compact,
                  pl.BlockSpec((None, None, 1, LANES), tile),
                  pl.BlockSpec((None, None, nsteps, S5_TILE_STATE), tile),
                  pl.BlockSpec((None, None, nsteps, S5_TILE_STATE), tile)],
        out_specs=pl.BlockSpec((seq, LANES), lambda j, b: (b, j)),
        out_shape=jax.ShapeDtypeStruct((bsz * seq, BRANCH_WIDTH), F32),
        scratch_shapes=[pltpu.VMEM((S5_ROWS, S5_ROWS), BF16)] * 3,
        compiler_params=_cparams(("parallel", "arbitrary")),
        name="s5_scan",
    )(h, k_y, k_v, k_q, d_t, ap_re, ap_im)


def _s5_weights(lam_re, lam_im, log_dt, b_re, b_im, c_re, c_im, d_skip, folds_per_seq):
    t_fold, g_n, p_n, c_n, tg = S5_FOLD, S5_GROUPS, S5_STATE, S5_GROUP, S5_TILE_GROUPS
    tiles = g_n // tg
    lr = jnp.minimum(lam_re.astype(F32), -S5_MIN_DECAY)
    li = lam_im.astype(F32)
    dt = jnp.exp(log_dt.astype(F32))[:, None]
    mag = jnp.exp(lr * dt)
    abar_re, abar_im = mag * jnp.cos(li * dt), mag * jnp.sin(li * dt)
    den = lr * lr + li * li
    fac_re = ((abar_re - 1.0) * lr + abar_im * li) / den
    fac_im = (abar_im * lr - (abar_re - 1.0) * li) / den
    br, bi = b_re.astype(F32), b_im.astype(F32)
    bbar_re = fac_re[..., None] * br - fac_im[..., None] * bi
    bbar_im = fac_re[..., None] * bi + fac_im[..., None] * br

    def apow(n):
        nn = jnp.asarray(n, F32)[:, None, None]
        m = jnp.exp(nn * (lr * dt)[None])
        return m * jnp.cos(nn * (li * dt)[None]), m * jnp.sin(nn * (li * dt)[None])

    pr, pi = apow(jnp.arange(t_fold + 1))
    cr, ci = c_re.astype(F32), c_im.astype(F32)
    ca_re = cr[None] * pr[:, :, None, :] - ci[None] * pi[:, :, None, :]
    ca_im = cr[None] * pi[:, :, None, :] + ci[None] * pr[:, :, None, :]
    k_tau = jnp.sum(ca_re[:t_fold, :, :, :, None] * bbar_re[None, :, None, :, :]
                    - ca_im[:t_fold, :, :, :, None] * bbar_im[None, :, None, :, :], axis=3)
    jj = jnp.arange(t_fold)[:, None]
    tt = jnp.arange(t_fold)[None, :]
    lag = jnp.clip(tt - jj, 0, t_fold - 1)
    toe = jnp.where((tt >= jj)[:, :, None, None, None], k_tau[lag], 0.0)
    k_y = (toe.reshape(t_fold, t_fold, tiles, tg, c_n, c_n).transpose(2, 0, 3, 5, 1, 4)
           .reshape(tiles, S5_ROWS, LANES))
    rev = t_fold - 1 - jnp.arange(t_fold)
    prj, pij = pr[rev], pi[rev]
    pv = jnp.stack([prj[..., None] * bbar_re[None] - pij[..., None] * bbar_im[None],
                    prj[..., None] * bbar_im[None] + pij[..., None] * bbar_re[None]])
    k_v = (pv.reshape(2, t_fold, tiles, tg, p_n, c_n).transpose(2, 1, 3, 5, 0, 4)
           .reshape(tiles, S5_ROWS, LANES))
    qa = jnp.stack([ca_re[1:], -ca_im[1:]])
    k_q = (qa.reshape(2, t_fold, tiles, tg, c_n, p_n).transpose(2, 0, 3, 5, 1, 4)
           .reshape(tiles, S5_ROWS, LANES))
    d_t = d_skip.astype(F32).reshape(tiles, 1, LANES)
    nsteps = max(int(math.log2(folds_per_seq)), 1)
    sr, si = apow(t_fold * (2 ** jnp.arange(nsteps)))
    ap_re = sr.reshape(nsteps, tiles, tg * p_n).transpose(1, 0, 2)
    ap_im = si.reshape(nsteps, tiles, tg * p_n).transpose(1, 0, 2)
    return k_y.astype(BF16), k_v.astype(BF16), k_q.astype(BF16), d_t, ap_re, ap_im


def _glu_kernel(y_ref, w_ref, b_ref, o_ref):
    y = y_ref[...]
    z = 0.5 * y * (1.0 + jnp.tanh(math.sqrt(2.0 / math.pi) * (y + 0.044715 * (y * y * y))))
    a = _dot(z.astype(BF16), w_ref[...]) + b_ref[...]
    o_ref[...] = (z * _sigmoid(a)).astype(o_ref.dtype)


def _glu(y, w, b, layer):
    n, width = y.shape
    tm = min(ROW_TILE, n)
    return pl.pallas_call(
        _glu_kernel,
        grid=(n // tm,),
        in_specs=[pl.BlockSpec((tm, width), lambda i: (i, 0)),
                  pl.BlockSpec((None, width, width), lambda i: (layer, 0, 0)),
                  pl.BlockSpec((None, 1, width), lambda i: (layer, 0, 0))],
        out_specs=pl.BlockSpec((tm, width), lambda i: (i, 0)),
        out_shape=jax.ShapeDtypeStruct((n, width), BF16),
        compiler_params=_cparams(("parallel",)),
        name="s5_glu",
    )(y, w, b)


def _gated_chunk(q, k, v, g, st_ref):
    c, kw = q.shape
    heads = kw // LANES
    dv = v.shape[1] // heads
    row = lax.broadcasted_iota(jnp.int32, (c, kw), 0)
    ri = lax.broadcasted_iota(jnp.int32, (c, c), 0)
    ci = lax.broadcasted_iota(jnp.int32, (c, c), 1)
    xr = jnp.where(ri > ci, ri ^ ci, 0)
    ksl = [slice(hd * LANES, (hd + 1) * LANES) for hd in range(heads)]
    vsl = [slice(hd * dv, (hd + 1) * dv) for hd in range(heads)]
    tril = jnp.where(ri >= ci, 1.0, 0.0).astype(BF16)
    g_hi = g.astype(BF16)
    g_rest = g - g_hi.astype(F32)
    g_mid = g_rest.astype(BF16)
    g_lo = (g_rest - g_mid.astype(F32)).astype(BF16)
    gc = _dot(tril, g_hi) + _dot(tril, g_mid) + _dot(tril, g_lo)

    def block_row(x, blk, pos):
        x3 = x.reshape(c // blk, blk, kw)
        return jnp.broadcast_to(x3[:, pos:pos + 1, :], x3.shape).reshape(c, kw)

    qb, kb = q.astype(BF16), k.astype(BF16)
    a = [jnp.where(ri == ci, _dot_nt(qb[:, s], kb[:, s]), 0.0) for s in ksl]
    end = gc
    half = 1
    while half < c:
        if half < SUBLANES:
            first = (row & half) == 0
            ref = jnp.where(first, end, pltpu.roll(end, half, axis=0))
            if 2 * half < SUBLANES:
                end = jnp.where(first, pltpu.roll(end, c - half, axis=0), end)
        else:
            ref = block_row(gc, 2 * half, half - 1)
        e = jnp.exp2(jnp.abs(gc - ref) * -LOG2E).astype(BF16)
        eq, ek = e * qb, e * kb
        level = xr >= half
        a = [jnp.where(level, _dot_nt(eq[:, s], ek[:, s]), a_h) for s, a_h in zip(ksl, a)]
        half *= 2
    last = block_row(gc, c, c - 1)
    vb = v.astype(BF16)
    qg = (q * jnp.exp(gc)).astype(BF16)
    kd = (k * jnp.exp(last - gc)).astype(BF16)
    keep = jnp.exp(last[0:1, :])
    outs = []
    for hd in range(heads):
        st = st_ref[hd]
        outs.append(_dot(a[hd].astype(BF16), vb[:, vsl[hd]]) + _dot_nt(qg[:, ksl[hd]], st.astype(BF16)))
        st_ref[hd] = st * keep[:, ksl[hd]] + _dot_tn(vb[:, vsl[hd]], kd[:, ksl[hd]])
    return outs


def _gla_kernel(q_ref, k_ref, v_ref, gate_ref, xb_ref, wl_ref, wg_ref, bg_ref, nw_ref, o_ref, st_ref):
    @pl.when(pl.program_id(1) == 0)
    def _():
        st_ref[...] = jnp.zeros_like(st_ref)

    glow = _dot(xb_ref[...], wl_ref[...]).astype(BF16)
    z = _dot(glow, wg_ref[...]) + bg_ref[...]
    g = (jnp.minimum(z, 0.0) - jnp.log1p(jnp.exp(-jnp.abs(z)))) / GLA_GATE_TAU
    outs = _gated_chunk(q_ref[...] * (GLA_DK ** -0.5), k_ref[...], v_ref[...], g, st_ref)
    for hd, o in enumerate(outs):
        vs = slice(hd * GLA_DV, (hd + 1) * GLA_DV)
        o = o * lax.rsqrt(jnp.mean(o * o, axis=-1, keepdims=True) + RMS_EPS) * nw_ref[...]
        gate = gate_ref[:, vs]
        o_ref[:, vs] = (o * (gate * _sigmoid(gate))).astype(o_ref.dtype)


def _att_specs(seq):
    c = min(ATT_CHUNK, seq)
    nc = seq // c

    def col(off, width):
        return pl.BlockSpec((c, width), lambda b, i: (b * nc + i, off // width))

    return c, nc, col


def _gla(h, xb, bsz, seq, wl, wg, bg, nw, layer):
    c, nc, col = _att_specs(seq)
    vec = lambda b, i: (layer, 0, 0)
    return pl.pallas_call(
        _gla_kernel,
        grid=(bsz, nc),
        in_specs=[col(COL_QB, GLA_KEY), col(COL_KB, GLA_KEY), col(COL_VB, BRANCH_WIDTH),
                  col(COL_GATEB, BRANCH_WIDTH),
                  pl.BlockSpec((c, D_MODEL), lambda b, i: (b * nc + i, 0)),
                  pl.BlockSpec((None, D_MODEL, LANES), vec),
                  pl.BlockSpec((None, LANES, GLA_KEY), vec),
                  pl.BlockSpec((None, 1, GLA_KEY), vec),
                  pl.BlockSpec((None, 1, GLA_DV), vec)],
        out_specs=pl.BlockSpec((c, BRANCH_WIDTH), lambda b, i: (b * nc + i, 0)),
        out_shape=jax.ShapeDtypeStruct((bsz * seq, BRANCH_WIDTH), BF16),
        scratch_shapes=[pltpu.VMEM((GLA_HEADS, GLA_DV, GLA_DK), F32)],
        compiler_params=_cparams(("parallel", "arbitrary")),
        name="gla",
    )(h, h, h, h, xb, wl, wg, bg, nw)


def _hgrn_kernel(q_ref, f_ref, i_ref, gate_ref, lb_ref, nw_ref, o_ref, st_ref):
    @pl.when(pl.program_id(1) == 0)
    def _():
        st_ref[...] = jnp.zeros_like(st_ref)

    lb = lb_ref[...]
    f = lb + (1.0 - lb) * _sigmoid(f_ref[...])
    qr = q_ref[...]
    outs = _gated_chunk(qr * _sigmoid(qr), 1.0 - f, i_ref[...], jnp.log(f), st_ref)
    for hd, o in enumerate(outs):
        vs = slice(hd * HGRN_DV, (hd + 1) * HGRN_DV)
        o = o * _sigmoid(gate_ref[:, vs])
        o = o * lax.rsqrt(jnp.mean(o * o, axis=-1, keepdims=True) + RMS_EPS) * nw_ref[...]
        o_ref[:, vs] = o.astype(o_ref.dtype)


def _hgrn(h, bsz, seq, lb, nw, layer):
    c, nc, col = _att_specs(seq)
    vec = lambda b, i: (layer, 0, 0)
    return pl.pallas_call(
        _hgrn_kernel,
        grid=(bsz, nc),
        in_specs=[col(COL_QC, HGRN_KEY), col(COL_FC, HGRN_KEY), col(COL_IC, BRANCH_WIDTH),
                  col(COL_GATEC, BRANCH_WIDTH),
                  pl.BlockSpec((None, 1, HGRN_KEY), vec),
                  pl.BlockSpec((None, 1, HGRN_DV), vec)],
        out_specs=pl.BlockSpec((c, BRANCH_WIDTH), lambda b, i: (b * nc + i, 0)),
        out_shape=jax.ShapeDtypeStruct((bsz * seq, BRANCH_WIDTH), BF16),
        scratch_shapes=[pltpu.VMEM((HGRN_HEADS, HGRN_DV, HGRN_EXPAND), F32)],
        compiler_params=_cparams(("parallel", "arbitrary")),
        name="hgrn",
    )(h, h, h, h, lb, nw)


def _zero_first_step(o_ref):
    @pl.when(pl.program_id(1) == 0)
    def _():
        o_ref[...] = jnp.zeros_like(o_ref)


def _norm_last_step(x_ref, lg_ref, lb_ref, o_ref, ob_ref):
    @pl.when(pl.program_id(1) == pl.num_programs(1) - 1)
    def _():
        xn = _layer_norm(DN_ALPHA * x_ref[...] + o_ref[...], lg_ref[...], lb_ref[...])
        o_ref[...] = xn
        ob_ref[...] = xn.astype(BF16)


def _merge_kernel(ya_ref, yb_ref, yc_ref, ga_ref, gb_ref, gc_ref, wup_ref, o_ref):
    merged = (_sigmoid(ga_ref[...]) * _dot(ya_ref[...], wup_ref[0])
              + _sigmoid(gb_ref[...]) * _dot(yb_ref[...], wup_ref[1])
              + _sigmoid(gc_ref[...]) * _dot(yc_ref[...], wup_ref[2]))
    o_ref[...] = merged.astype(o_ref.dtype)


def _merge(ya, yb, yc, h, wup, layer):
    n = ya.shape[0]
    tm = min(MERGE_TM, n)
    tn = MERGE_TILE

    def gate(branch):
        off = (COL_MERGE + branch * D_MODEL) // tn
        return pl.BlockSpec((tm, tn), lambda i, j: (i, off + j))

    ybs = pl.BlockSpec((tm, BRANCH_WIDTH), lambda i, j: (i, 0))
    return pl.pallas_call(
        _merge_kernel,
        grid=(n // tm, D_MODEL // tn),
        in_specs=[ybs, ybs, ybs, gate(0), gate(1), gate(2),
                  pl.BlockSpec((None, N_BRANCH, BRANCH_WIDTH, tn), lambda i, j: (layer, 0, 0, j))],
        out_specs=pl.BlockSpec((tm, tn), lambda i, j: (i, j)),
        out_shape=jax.ShapeDtypeStruct((n, D_MODEL), BF16),
        compiler_params=_cparams(("parallel", "arbitrary")),
        name="merge",
    )(ya, yb, yc, h, h, h, wup)


def _mixer_out_kernel(m_ref, wout_ref, x_ref, lg_ref, lb_ref, o_ref, ob_ref):
    xn = _layer_norm(DN_ALPHA * x_ref[...] + _dot(m_ref[...], wout_ref[...]), lg_ref[...], lb_ref[...])
    o_ref[...] = xn
    ob_ref[...] = xn.astype(BF16)


def _mixer_out(merged, wout, x, ln_g, ln_b, layer):
    n = x.shape[0]
    tm = min(ROW_TILE, n)
    row = pl.BlockSpec((tm, D_MODEL), lambda i: (i, 0))
    vec = pl.BlockSpec((None, 1, D_MODEL), lambda i: (layer, 0, 0))
    return pl.pallas_call(
        _mixer_out_kernel,
        grid=(n // tm,),
        in_specs=[row, pl.BlockSpec((None, D_MODEL, D_MODEL), lambda i: (layer, 0, 0)), row, vec, vec],
        out_specs=[row, row],
        out_shape=[jax.ShapeDtypeStruct((n, D_MODEL), F32), jax.ShapeDtypeStruct((n, D_MODEL), BF16)],
        compiler_params=_cparams(("parallel",)),
        name="mixer_out",
    )(merged, wout, x, ln_g, ln_b)


def _mlp_kernel(xb_ref, w1_ref, w2_ref, x_ref, lg_ref, lb_ref, o_ref, ob_ref):
    _zero_first_step(o_ref)
    hid = jnp.maximum(_dot(xb_ref[...], w1_ref[...]), 0.0)
    o_ref[...] += _dot((hid * hid).astype(BF16), w2_ref[...])
    _norm_last_step(x_ref, lg_ref, lb_ref, o_ref, ob_ref)


def _mlp(xb, w1, w2, x, ln_g, ln_b, layer):
    n = x.shape[0]
    tm = min(ROW_TILE, n)
    th = HID_TILE
    row = pl.BlockSpec((tm, D_MODEL), lambda i, k: (i, 0))
    vec = pl.BlockSpec((None, 1, D_MODEL), lambda i, k: (layer, 0, 0))
    return pl.pallas_call(
        _mlp_kernel,
        grid=(n // tm, MLP_HIDDEN // th),
        in_specs=[row,
                  pl.BlockSpec((None, D_MODEL, th), lambda i, k: (layer, 0, k)),
                  pl.BlockSpec((None, th, D_MODEL), lambda i, k: (layer, k, 0)),
                  row, vec, vec],
        out_specs=[row, row],
        out_shape=[jax.ShapeDtypeStruct((n, D_MODEL), F32), jax.ShapeDtypeStruct((n, D_MODEL), BF16)],
        compiler_params=_cparams(("parallel", "arbitrary")),
        name="mlp",
    )(xb, w1, w2, x, ln_g, ln_b)


def _w_in_pack_kernel(w_ref, o_ref):
    o_ref[:, :W_IN_GLOW] = w_ref[:, :W_IN_GLOW].astype(BF16)
    o_ref[:, W_IN_GLOW:] = w_ref[:, W_IN_GLOW + GLA_GATE_RANK:].astype(BF16)


def _split_w_in(w):
    depth, k, width = w.shape
    rows = 128
    main = pl.pallas_call(
        _w_in_pack_kernel,
        grid=(depth, k // rows),
        in_specs=[pl.BlockSpec((None, rows, width), lambda l, i: (l, i, 0))],
        out_specs=pl.BlockSpec((None, rows, PROJ_WIDTH), lambda l, i: (l, i, 0)),
        out_shape=jax.ShapeDtypeStruct((depth, k, PROJ_WIDTH), BF16),
        compiler_params=_cparams(("parallel", "parallel")),
        name="w_in_pack",
    )(w)
    pad = jnp.zeros(w.shape[:-1] + (LANES - GLA_GATE_RANK,), w.dtype)
    glow = jnp.concatenate([w[..., W_IN_GLOW:W_IN_GLOW + GLA_GATE_RANK], pad], axis=-1).astype(BF16)
    return main, glow


def _row(v):
    return v.astype(F32)[:, None, :]


def kernel(x, w_in, s5_lam_re, s5_lam_im, s5_log_dt, s5_b_re, s5_b_im, s5_c_re, s5_c_im, s5_d, s5_w_glu, s5_b_glu, gla_w_gate, gla_b_gate, gla_norm_w, hgrn_lb_logits, hgrn_norm_w, w_up, w_out, ln1_g, ln1_b, ln2_g, ln2_b, w_mlp_in, w_mlp_out):
    bsz, seq, d = x.shape
    n = bsz * seq
    depth = w_in.shape[0]
    p = jax.nn.softmax(hgrn_lb_logits.astype(F32), axis=0)
    lb = _row(jnp.cumsum(p, axis=0) - p[0])
    w_in_b, w_glow_b = _split_w_in(w_in)
    s5w = jax.vmap(functools.partial(_s5_weights, folds_per_seq=seq // S5_FOLD))(
        s5_lam_re, s5_lam_im, s5_log_dt, s5_b_re, s5_b_im, s5_c_re, s5_c_im, s5_d)
    w_glu_b = s5_w_glu.astype(BF16)
    b_glu = _row(s5_b_glu)
    wg = jnp.concatenate([gla_w_gate, jnp.zeros((depth, LANES - GLA_GATE_RANK, GLA_KEY), gla_w_gate.dtype)],
                         axis=1).astype(BF16)
    bg, gla_nw, hgrn_nw = _row(gla_b_gate), _row(gla_norm_w), _row(hgrn_norm_w)
    w_up_b, w_out_b = w_up.astype(BF16), w_out.astype(BF16)
    w1_b, w2_b = w_mlp_in.astype(BF16), w_mlp_out.astype(BF16)
    g1, b1, g2, b2 = _row(ln1_g), _row(ln1_b), _row(ln2_g), _row(ln2_b)

    xf = x.reshape(n, d).astype(F32)
    xb = xf.astype(BF16)
    for l in range(depth):
        h = _proj(xb, w_in_b, l)
        ya = _glu(_s5_scan(h, bsz, seq, s5w, l), w_glu_b, b_glu, l)
        yb = _gla(h, xb, bsz, seq, w_glow_b, wg, bg, gla_nw, l)
        yc = _hgrn(h, bsz, seq, lb, hgrn_nw, l)
        xf, xb = _mixer_out(_merge(ya, yb, yc, h, w_up_b, l), w_out_b, xf, g1, b1, l)
        xf, xb = _mlp(xb, w1_b, w2_b, xf, g2, b2, l)
    return xf.reshape(bsz, seq, d).astype(x.dtype)
```

```python
import functools
import math

import jax
import jax.numpy as jnp
from jax import lax
from jax.experimental import pallas as pl
from jax.experimental.pallas import tpu as pltpu

F32 = jnp.float32
BF16 = jnp.bfloat16

D_MODEL = 2048
DEPTH = 4
N_BRANCH = 3
BRANCH_WIDTH = D_MODEL // 2
S5_GROUP = 16
S5_GROUPS = BRANCH_WIDTH // S5_GROUP
S5_STATE = 64
S5_MIN_DECAY = 1e-4
GLA_HEADS = 4
GLA_DV = BRANCH_WIDTH // GLA_HEADS
GLA_DK = GLA_DV // 2
GLA_KEY = GLA_HEADS * GLA_DK
GLA_GATE_RANK = 16
GLA_GATE_TAU = 16.0
HGRN_EXPAND = 128
HGRN_HEADS = BRANCH_WIDTH // HGRN_EXPAND
HGRN_DV = BRANCH_WIDTH // HGRN_HEADS
HGRN_KEY = HGRN_HEADS * HGRN_EXPAND
MLP_HIDDEN = 4 * D_MODEL
DN_ALPHA = (2 * DEPTH) ** 0.25
LN_EPS = 1e-5
RMS_EPS = 1e-6
LOG2E = 1.4426950408889634

LANES = 128
SUBLANES = 8
VMEM_LIMIT_BYTES = 56 * 1024 * 1024

COL_U = 0
COL_QB = COL_U + BRANCH_WIDTH
COL_KB = COL_QB + GLA_KEY
COL_VB = COL_KB + GLA_KEY
COL_GATEB = COL_VB + BRANCH_WIDTH
COL_QC = COL_GATEB + BRANCH_WIDTH
COL_FC = COL_QC + HGRN_KEY
COL_IC = COL_FC + HGRN_KEY
COL_GATEC = COL_IC + BRANCH_WIDTH
COL_MERGE = COL_GATEC + BRANCH_WIDTH
PROJ_WIDTH = COL_MERGE + N_BRANCH * D_MODEL
W_IN_GLOW = BRANCH_WIDTH + 2 * GLA_KEY + BRANCH_WIDTH

S5_FOLD = SUBLANES
S5_TILE_GROUPS = LANES // S5_GROUP
S5_TILES = S5_GROUPS // S5_TILE_GROUPS
S5_TILE_STATE = S5_TILE_GROUPS * S5_STATE
S5_ROWS = S5_FOLD * LANES
ATT_CHUNK = 128
ROW_TILE = 512
PROJ_TM = 2048
PROJ_TN = 1024
MERGE_TM = 1024
MERGE_TILE = 512
HID_TILE = 1024


def _cparams(sem):
    return pltpu.CompilerParams(dimension_semantics=sem, vmem_limit_bytes=VMEM_LIMIT_BYTES)


def _dot(a, b):
    return jnp.dot(a, b, preferred_element_type=F32)


def _dot_nt(a, b):
    return lax.dot_general(a, b, (((1,), (1,)), ((), ())), preferred_element_type=F32)


def _dot_tn(a, b):
    return lax.dot_general(a, b, (((0,), (0,)), ((), ())), preferred_element_type=F32)


def _sigmoid(x):
    return 1.0 / (1.0 + jnp.exp2(x * -LOG2E))


def _layer_norm(v, g, b):
    mu = jnp.mean(v, axis=-1, keepdims=True)
    xc = v - mu
    var = jnp.mean(xc * xc, axis=-1, keepdims=True)
    return xc * lax.rsqrt(var + LN_EPS) * g + b


def _proj_kernel(x_ref, w_ref, o_ref):
    o_ref[...] = _dot_nt(x_ref[...], w_ref[...])


def _proj(xb, w, layer):
    n, k = xb.shape
    width = w.shape[1]
    tm = min(PROJ_TM, n)
    return pl.pallas_call(
        _proj_kernel,
        grid=(n // tm, width // PROJ_TN),
        in_specs=[pl.BlockSpec((tm, k), lambda i, j: (i, 0)),
                  pl.BlockSpec((None, PROJ_TN, k), lambda i, j: (layer, j, 0))],
        out_specs=pl.BlockSpec((tm, PROJ_TN), lambda i, j: (i, j)),
        out_shape=jax.ShapeDtypeStruct((n, width), F32),
        compiler_params=_cparams(("parallel", "arbitrary")),
        name="proj",
    )(xb, w)


def _s5_expand(ky_ref, kv_ref, kq_ref, my_ref, mv_ref, mq_ref):
    kk = lax.broadcasted_iota(jnp.int32, (LANES, S5_ROWS), 0)
    cc = lax.broadcasted_iota(jnp.int32, (LANES, S5_ROWS), 1)
    e_tc = jnp.where(((kk >> 4) == (cc >> 7)) & ((kk & 15) == (cc & 15)), 1.0, 0.0).astype(BF16)
    e_rp = jnp.where(((kk >> 6) == (cc >> 9)) & ((kk & 63) == (cc & 63)), 1.0, 0.0).astype(BF16)
    rr = lax.broadcasted_iota(jnp.int32, (S5_ROWS, S5_ROWS), 0)
    cc = lax.broadcasted_iota(jnp.int32, (S5_ROWS, S5_ROWS), 1)
    g_jgd, g_rgp = (rr >> 4) & 7, (rr >> 6) & 7
    h_thc, h_rhp = (cc >> 4) & 7, (cc >> 6) & 7
    my_ref[...] = jnp.where(g_jgd == h_thc, _dot(ky_ref[...], e_tc), 0.0).astype(BF16)
    mv_ref[...] = jnp.where(g_jgd == h_rhp, _dot(kv_ref[...], e_rp), 0.0).astype(BF16)
    mq_ref[...] = jnp.where(g_rgp == h_thc, _dot(kq_ref[...], e_tc), 0.0).astype(BF16)


def _scan_steps(s_re, s_im, pos, apr_ref, api_ref, first_step, length):
    step, dist = first_step, 1
    while dist < length:
        ok = pos >= dist
        sh_re = jnp.where(ok, pltpu.roll(s_re, dist, axis=0), 0.0)
        sh_im = jnp.where(ok, pltpu.roll(s_im, dist, axis=0), 0.0)
        ar = apr_ref[step:step + 1, :]
        ai = api_ref[step:step + 1, :]
        s_re, s_im = s_re + ar * sh_re - ai * sh_im, s_im + ar * sh_im + ai * sh_re
        step += 1
        dist *= 2
    return s_re, s_im


def _s5_kernel(u_ref, ky_ref, kv_ref, kq_ref, d_ref, apr_ref, api_ref, pwr_ref, pwi_ref, y_ref,
               my_ref, mv_ref, mq_ref, sre_ref, sim_ref, pre_ref, pim_ref):
    @pl.when(pl.program_id(1) == 0)
    def _():
        _s5_expand(ky_ref, kv_ref, kq_ref, my_ref, mv_ref, mq_ref)

    folds = u_ref.shape[0] // S5_FOLD
    runs = folds // SUBLANES
    us = [u_ref[pl.ds(t, folds, stride=S5_FOLD), :] for t in range(S5_FOLD)]
    ucat = jnp.concatenate([u.astype(BF16) for u in us], axis=1)
    y_intra = _dot(ucat, my_ref[...])
    r = _dot(ucat, mv_ref[...])
    fold = lax.broadcasted_iota(jnp.int32, (folds, S5_TILE_STATE), 0)
    s_re, s_im = _scan_steps(r[:, :S5_TILE_STATE], r[:, S5_TILE_STATE:], fold & (SUBLANES - 1),
                             apr_ref, api_ref, 0, SUBLANES)
    lane_tiles = S5_TILE_STATE // LANES

    def put(ref, j, val):
        for lt in range(lane_tiles):
            ref[lt, pl.ds(j, runs, stride=SUBLANES), :] = val[:, lt * LANES:(lt + 1) * LANES]

    def get(ref, j):
        return jnp.concatenate([ref[lt, pl.ds(j, runs, stride=SUBLANES), :] for lt in range(lane_tiles)], axis=1)

    for lt in range(lane_tiles):
        sre_ref[lt] = s_re[:, lt * LANES:(lt + 1) * LANES]
        sim_ref[lt] = s_im[:, lt * LANES:(lt + 1) * LANES]
    last = SUBLANES - 1
    run = lax.broadcasted_iota(jnp.int32, (runs, S5_TILE_STATE), 0)
    e_re, e_im = _scan_steps(get(sre_ref, last), get(sim_ref, last), run, apr_ref, api_ref,
                             int(math.log2(SUBLANES)), runs)
    c_re = jnp.where(run >= 1, pltpu.roll(e_re, 1, axis=0), 0.0)
    c_im = jnp.where(run >= 1, pltpu.roll(e_im, 1, axis=0), 0.0)
    put(pre_ref, 0, c_re)
    put(pim_ref, 0, c_im)
    for j in range(last):
        l_re, l_im = get(sre_ref, j), get(sim_ref, j)
        br, bi = pwr_ref[j:j + 1, :], pwi_ref[j:j + 1, :]
        put(pre_ref, j + 1, l_re + br * c_re - bi * c_im)
        put(pim_ref, j + 1, l_im + br * c_im + bi * c_re)
    prev = jnp.concatenate([pre_ref[lt] for lt in range(lane_tiles)]
                           + [pim_ref[lt] for lt in range(lane_tiles)], axis=1).astype(BF16)
    y = y_intra + _dot(prev, mq_ref[...])
    for t in range(S5_FOLD):
        y_ref[pl.ds(t, folds, stride=S5_FOLD), :] = y[:, t * LANES:(t + 1) * LANES] + d_ref[...] * us[t]


def _s5_scan(h, bsz, seq, s5w, layer):
    k_y, k_v, k_q, d_t, ap_re, ap_im, pw_re, pw_im = s5w
    nsteps = ap_re.shape[2]
    folds = seq // S5_FOLD
    tile = lambda j, b: (layer, j, 0, 0)
    compact = pl.BlockSpec((None, None, S5_ROWS, LANES), tile)
    steps = pl.BlockSpec((None, None, nsteps, S5_TILE_STATE), tile)
    powers = pl.BlockSpec((None, None, SUBLANES, S5_TILE_STATE), tile)
    return pl.pallas_call(
        _s5_kernel,
        grid=(S5_TILES, bsz),
        in_specs=[pl.BlockSpec((seq, LANES), lambda j, b: (b, COL_U // LANES + j)),
                  compact, compact, compact,
                  pl.BlockSpec((None, None, 1, LANES), tile),
                  steps, steps, powers, powers],
        out_specs=pl.BlockSpec((seq, LANES), lambda j, b: (b, j)),
        out_shape=jax.ShapeDtypeStruct((bsz * seq, BRANCH_WIDTH), F32),
        scratch_shapes=[pltpu.VMEM((S5_ROWS, S5_ROWS), BF16)] * 3
        + [pltpu.VMEM((S5_TILE_STATE // LANES, folds, LANES), F32)] * 4,
        compiler_params=_cparams(("parallel", "arbitrary")),
        name="s5_scan",
    )(h, k_y, k_v, k_q, d_t, ap_re, ap_im, pw_re, pw_im)


def _s5_weights(lam_re, lam_im, log_dt, b_re, b_im, c_re, c_im, d_skip, folds_per_seq):
    t_fold, g_n, p_n, c_n, tg = S5_FOLD, S5_GROUPS, S5_STATE, S5_GROUP, S5_TILE_GROUPS
    tiles = g_n // tg
    lr = jnp.minimum(lam_re.astype(F32), -S5_MIN_DECAY)
    li = lam_im.astype(F32)
    dt = jnp.exp(log_dt.astype(F32))[:, None]
    mag = jnp.exp(lr * dt)
    abar_re, abar_im = mag * jnp.cos(li * dt), mag * jnp.sin(li * dt)
    den = lr * lr + li * li
    fac_re = ((abar_re - 1.0) * lr + abar_im * li) / den
    fac_im = (abar_im * lr - (abar_re - 1.0) * li) / den
    br, bi = b_re.astype(F32), b_im.astype(F32)
    bbar_re = fac_re[..., None] * br - fac_im[..., None] * bi
    bbar_im = fac_re[..., None] * bi + fac_im[..., None] * br

    def apow(n):
        nn = jnp.asarray(n, F32)[:, None, None]
        m = jnp.exp(nn * (lr * dt)[None])
        return m * jnp.cos(nn * (li * dt)[None]), m * jnp.sin(nn * (li * dt)[None])

    pr, pi = apow(jnp.arange(t_fold + 1))
    cr, ci = c_re.astype(F32), c_im.astype(F32)
    ca_re = cr[None] * pr[:, :, None, :] - ci[None] * pi[:, :, None, :]
    ca_im = cr[None] * pi[:, :, None, :] + ci[None] * pr[:, :, None, :]
    k_tau = jnp.sum(ca_re[:t_fold, :, :, :, None] * bbar_re[None, :, None, :, :]
                    - ca_im[:t_fold, :, :, :, None] * bbar_im[None, :, None, :, :], axis=3)
    jj = jnp.arange(t_fold)[:, None]
    tt = jnp.arange(t_fold)[None, :]
    lag = jnp.clip(tt - jj, 0, t_fold - 1)
    toe = jnp.where((tt >= jj)[:, :, None, None, None], k_tau[lag], 0.0)
    k_y = (toe.reshape(t_fold, t_fold, tiles, tg, c_n, c_n).transpose(2, 0, 3, 5, 1, 4)
           .reshape(tiles, S5_ROWS, LANES))
    rev = t_fold - 1 - jnp.arange(t_fold)
    prj, pij = pr[rev], pi[rev]
    pv = jnp.stack([prj[..., None] * bbar_re[None] - pij[..., None] * bbar_im[None],
                    prj[..., None] * bbar_im[None] + pij[..., None] * bbar_re[None]])
    k_v = (pv.reshape(2, t_fold, tiles, tg, p_n, c_n).transpose(2, 1, 3, 5, 0, 4)
           .reshape(tiles, S5_ROWS, LANES))
    qa = jnp.stack([ca_re[1:], -ca_im[1:]])
    k_q = (qa.reshape(2, t_fold, tiles, tg, c_n, p_n).transpose(2, 0, 3, 5, 1, 4)
           .reshape(tiles, S5_ROWS, LANES))
    d_t = d_skip.astype(F32).reshape(tiles, 1, LANES)
    nsteps = max(int(math.log2(folds_per_seq)), 1)
    sr, si = apow(t_fold * (2 ** jnp.arange(nsteps)))
    ap_re = sr.reshape(nsteps, tiles, tg * p_n).transpose(1, 0, 2)
    ap_im = si.reshape(nsteps, tiles, tg * p_n).transpose(1, 0, 2)
    wr, wi = apow(t_fold * (1 + jnp.arange(SUBLANES)))
    pw_re = wr.reshape(SUBLANES, tiles, tg * p_n).transpose(1, 0, 2)
    pw_im = wi.reshape(SUBLANES, tiles, tg * p_n).transpose(1, 0, 2)
    return k_y.astype(BF16), k_v.astype(BF16), k_q.astype(BF16), d_t, ap_re, ap_im, pw_re, pw_im


def _glu_kernel(y_ref, w_ref, b_ref, o_ref):
    y = y_ref[...]
    z = 0.5 * y * (1.0 + jnp.tanh(math.sqrt(2.0 / math.pi) * (y + 0.044715 * (y * y * y))))
    a = _dot(z.astype(BF16), w_ref[...]) + b_ref[...]
    o_ref[...] = (z * _sigmoid(a)).astype(o_ref.dtype)


def _glu(y, w, b, layer):
    n, width = y.shape
    tm = min(ROW_TILE, n)
    return pl.pallas_call(
        _glu_kernel,
        grid=(n // tm,),
        in_specs=[pl.BlockSpec((tm, width), lambda i: (i, 0)),
                  pl.BlockSpec((None, width, width), lambda i: (layer, 0, 0)),
                  pl.BlockSpec((None, 1, width), lambda i: (layer, 0, 0))],
        out_specs=pl.BlockSpec((tm, width), lambda i: (i, 0)),
        out_shape=jax.ShapeDtypeStruct((n, width), BF16),
        compiler_params=_cparams(("parallel",)),
        name="s5_glu",
    )(y, w, b)


def _gated_chunk(q, k, v, g, st_ref):
    c, kw = q.shape
    heads = kw // LANES
    dv = v.shape[1] // heads
    row = lax.broadcasted_iota(jnp.int32, (c, kw), 0)
    ri = lax.broadcasted_iota(jnp.int32, (c, c), 0)
    ci = lax.broadcasted_iota(jnp.int32, (c, c), 1)
    xr = jnp.where(ri > ci, ri ^ ci, 0)
    ksl = [slice(hd * LANES, (hd + 1) * LANES) for hd in range(heads)]
    vsl = [slice(hd * dv, (hd + 1) * dv) for hd in range(heads)]
    tril = jnp.where(ri >= ci, 1.0, 0.0).astype(BF16)
    g_hi = g.astype(BF16)
    g_rest = g - g_hi.astype(F32)
    g_mid = g_rest.astype(BF16)
    g_lo = (g_rest - g_mid.astype(F32)).astype(BF16)
    gc = _dot(tril, g_hi) + _dot(tril, g_mid) + _dot(tril, g_lo)

    def block_row(x, blk, pos):
        x3 = x.reshape(c // blk, blk, kw)
        return jnp.broadcast_to(x3[:, pos:pos + 1, :], x3.shape).reshape(c, kw)

    qb, kb = q.astype(BF16), k.astype(BF16)
    a = [jnp.where(ri == ci, _dot_nt(qb[:, s], kb[:, s]), 0.0) for s in ksl]
    end = gc
    half = 1
    while half < c:
        if half < SUBLANES:
            first = (row & half) == 0
            ref = jnp.where(first, end, pltpu.roll(end, half, axis=0))
            if 2 * half < SUBLANES:
                end = jnp.where(first, pltpu.roll(end, c - half, axis=0), end)
        else:
            ref = block_row(gc, 2 * half, half - 1)
        e = jnp.exp2(jnp.abs(gc - ref) * -LOG2E).astype(BF16)
        eq, ek = e * qb, e * kb
        level = xr >= half
        a = [jnp.where(level, _dot_nt(eq[:, s], ek[:, s]), a_h) for s, a_h in zip(ksl, a)]
        half *= 2
    last = block_row(gc, c, c - 1)
    vb = v.astype(BF16)
    qg = (q * jnp.exp(gc)).astype(BF16)
    kd = (k * jnp.exp(last - gc)).astype(BF16)
    keep = jnp.exp(last[0:1, :])
    outs = []
    for hd in range(heads):
        st = st_ref[hd]
        outs.append(_dot(a[hd].astype(BF16), vb[:, vsl[hd]]) + _dot_nt(qg[:, ksl[hd]], st.astype(BF16)))
        st_ref[hd] = st * keep[:, ksl[hd]] + _dot_tn(vb[:, vsl[hd]], kd[:, ksl[hd]])
    return outs


def _gla_kernel(q_ref, k_ref, v_ref, gate_ref, xb_ref, wl_ref, wg_ref, bg_ref, nw_ref, o_ref, st_ref):
    @pl.when(pl.program_id(1) == 0)
    def _():
        st_ref[...] = jnp.zeros_like(st_ref)

    glow = _dot(xb_ref[...], wl_ref[...]).astype(BF16)
    z = _dot(glow, wg_ref[...]) + bg_ref[...]
    g = (jnp.minimum(z, 0.0) - jnp.log1p(jnp.exp(-jnp.abs(z)))) / GLA_GATE_TAU
    outs = _gated_chunk(q_ref[...] * (GLA_DK ** -0.5), k_ref[...], v_ref[...], g, st_ref)
    for hd, o in enumerate(outs):
        vs = slice(hd * GLA_DV, (hd + 1) * GLA_DV)
        o = o * lax.rsqrt(jnp.mean(o * o, axis=-1, keepdims=True) + RMS_EPS) * nw_ref[...]
        gate = gate_ref[:, vs]
        o_ref[:, vs] = (o * (gate * _sigmoid(gate))).astype(o_ref.dtype)


def _att_specs(seq):
    c = min(ATT_CHUNK, seq)
    nc = seq // c

    def col(off, width):
        return pl.BlockSpec((c, width), lambda b, i: (b * nc + i, off // width))

    return c, nc, col


def _gla(h, xb, bsz, seq, wl, wg, bg, nw, layer):
    c, nc, col = _att_specs(seq)
    vec = lambda b, i: (layer, 0, 0)
    return pl.pallas_call(
        _gla_kernel,
        grid=(bsz, nc),
        in_specs=[col(COL_QB, GLA_KEY), col(COL_KB, GLA_KEY), col(COL_VB, BRANCH_WIDTH),
                  col(COL_GATEB, BRANCH_WIDTH),
                  pl.BlockSpec((c, D_MODEL), lambda b, i: (b * nc + i, 0)),
                  pl.BlockSpec((None, D_MODEL, LANES), vec),
                  pl.BlockSpec((None, LANES, GLA_KEY), vec),
                  pl.BlockSpec((None, 1, GLA_KEY), vec),
                  pl.BlockSpec((None, 1, GLA_DV), vec)],
        out_specs=pl.BlockSpec((c, BRANCH_WIDTH), lambda b, i: (b * nc + i, 0)),
        out_shape=jax.ShapeDtypeStruct((bsz * seq, BRANCH_WIDTH), BF16),
        scratch_shapes=[pltpu.VMEM((GLA_HEADS, GLA_DV, GLA_DK), F32)],
        compiler_params=_cparams(("parallel", "arbitrary")),
        name="gla",
    )(h, h, h, h, xb, wl, wg, bg, nw)


def _hgrn_kernel(q_ref, f_ref, i_ref, gate_ref, lb_ref, nw_ref, o_ref, st_ref):
    @pl.when(pl.program_id(1) == 0)
    def _():
        st_ref[...] = jnp.zeros_like(st_ref)

    lb = lb_ref[...]
    f = lb + (1.0 - lb) * _sigmoid(f_ref[...])
    qr = q_ref[...]
    outs = _gated_chunk(qr * _sigmoid(qr), 1.0 - f, i_ref[...], jnp.log(f), st_ref)
    for hd, o in enumerate(outs):
        vs = slice(hd * HGRN_DV, (hd + 1) * HGRN_DV)
        o = o * _sigmoid(gate_ref[:, vs])
        o = o * lax.rsqrt(jnp.mean(o * o, axis=-1, keepdims=True) + RMS_EPS) * nw_ref[...]
        o_ref[:, vs] = o.astype(o_ref.dtype)


def _hgrn(h, bsz, seq, lb, nw, layer):
    c, nc, col = _att_specs(seq)
    vec = lambda b, i: (layer, 0, 0)
    return pl.pallas_call(
        _hgrn_kernel,
        grid=(bsz, nc),
        in_specs=[col(COL_QC, HGRN_KEY), col(COL_FC, HGRN_KEY), col(COL_IC, BRANCH_WIDTH),
                  col(COL_GATEC, BRANCH_WIDTH),
                  pl.BlockSpec((None, 1, HGRN_KEY), vec),
                  pl.BlockSpec((None, 1, HGRN_DV), vec)],
        out_specs=pl.BlockSpec((c, BRANCH_WIDTH), lambda b, i: (b * nc + i, 0)),
        out_shape=jax.ShapeDtypeStruct((bsz * seq, BRANCH_WIDTH), BF16),
        scratch_shapes=[pltpu.VMEM((HGRN_HEADS, HGRN_DV, HGRN_EXPAND), F32)],
        compiler_params=_cparams(("parallel", "arbitrary")),
        name="hgrn",
    )(h, h, h, h, lb, nw)


def _zero_first_step(o_ref):
    @pl.when(pl.program_id(1) == 0)
    def _():
        o_ref[...] = jnp.zeros_like(o_ref)


def _norm_last_step(x_ref, lg_ref, lb_ref, o_ref, ob_ref):
    @pl.when(pl.program_id(1) == pl.num_programs(1) - 1)
    def _():
        xn = _layer_norm(DN_ALPHA * x_ref[...] + o_ref[...], lg_ref[...], lb_ref[...])
        o_ref[...] = xn
        ob_ref[...] = xn.astype(BF16)


def _merge_kernel(ya_ref, yb_ref, yc_ref, ga_ref, gb_ref, gc_ref, wup_ref, o_ref):
    merged = (_sigmoid(ga_ref[...]) * _dot(ya_ref[...], wup_ref[0])
              + _sigmoid(gb_ref[...]) * _dot(yb_ref[...], wup_ref[1])
              + _sigmoid(gc_ref[...]) * _dot(yc_ref[...], wup_ref[2]))
    o_ref[...] = merged.astype(o_ref.dtype)


def _merge(ya, yb, yc, h, wup, layer):
    n = ya.shape[0]
    tm = min(MERGE_TM, n)
    tn = MERGE_TILE

    def gate(branch):
        off = (COL_MERGE + branch * D_MODEL) // tn
        return pl.BlockSpec((tm, tn), lambda i, j: (i, off + j))

    ybs = pl.BlockSpec((tm, BRANCH_WIDTH), lambda i, j: (i, 0))
    return pl.pallas_call(
        _merge_kernel,
        grid=(n // tm, D_MODEL // tn),
        in_specs=[ybs, ybs, ybs, gate(0), gate(1), gate(2),
                  pl.BlockSpec((None, N_BRANCH, BRANCH_WIDTH, tn), lambda i, j: (layer, 0, 0, j))],
        out_specs=pl.BlockSpec((tm, tn), lambda i, j: (i, j)),
        out_shape=jax.ShapeDtypeStruct((n, D_MODEL), BF16),
        compiler_params=_cparams(("parallel", "arbitrary")),
        name="merge",
    )(ya, yb, yc, h, h, h, wup)


def _mixer_out_kernel(m_ref, wout_ref, x_ref, lg_ref, lb_ref, o_ref, ob_ref):
    xn = _layer_norm(DN_ALPHA * x_ref[...] + _dot(m_ref[...], wout_ref[...]), lg_ref[...], lb_ref[...])
    o_ref[...] = xn
    ob_ref[...] = xn.astype(BF16)


def _mixer_out(merged, wout, x, ln_g, ln_b, layer):
    n = x.shape[0]
    tm = min(ROW_TILE, n)
    row = pl.BlockSpec((tm, D_MODEL), lambda i: (i, 0))
    vec = pl.BlockSpec((None, 1, D_MODEL), lambda i: (layer, 0, 0))
    return pl.pallas_call(
        _mixer_out_kernel,
        grid=(n // tm,),
        in_specs=[row, pl.BlockSpec((None, D_MODEL, D_MODEL), lambda i: (layer, 0, 0)), row, vec, vec],
        out_specs=[row, row],
        out_shape=[jax.ShapeDtypeStruct((n, D_MODEL), F32), jax.ShapeDtypeStruct((n, D_MODEL), BF16)],
        compiler_params=_cparams(("parallel",)),
        name="mixer_out",
    )(merged, wout, x, ln_g, ln_b)


def _mlp_kernel(xb_ref, w1_ref, w2_ref, x_ref, lg_ref, lb_ref, o_ref, ob_ref):
    _zero_first_step(o_ref)
    hid = jnp.maximum(_dot(xb_ref[...], w1_ref[...]), 0.0)
    o_ref[...] += _dot((hid * hid).astype(BF16), w2_ref[...])
    _norm_last_step(x_ref, lg_ref, lb_ref, o_ref, ob_ref)


def _mlp(xb, w1, w2, x, ln_g, ln_b, layer):
    n = x.shape[0]
    tm = min(ROW_TILE, n)
    th = HID_TILE
    row = pl.BlockSpec((tm, D_MODEL), lambda i, k: (i, 0))
    vec = pl.BlockSpec((None, 1, D_MODEL), lambda i, k: (layer, 0, 0))
    return pl.pallas_call(
        _mlp_kernel,
        grid=(n // tm, MLP_HIDDEN // th),
        in_specs=[row,
                  pl.BlockSpec((None, D_MODEL, th), lambda i, k: (layer, 0, k)),
                  pl.BlockSpec((None, th, D_MODEL), lambda i, k: (layer, k, 0)),
                  row, vec, vec],
        out_specs=[row, row],
        out_shape=[jax.ShapeDtypeStruct((n, D_MODEL), F32), jax.ShapeDtypeStruct((n, D_MODEL), BF16)],
        compiler_params=_cparams(("parallel", "arbitrary")),
        name="mlp",
    )(xb, w1, w2, x, ln_g, ln_b)


def _split_w_in(w):
    wt = jnp.swapaxes(w, 1, 2)
    glow1 = W_IN_GLOW + GLA_GATE_RANK
    main = jnp.concatenate([wt[:, :W_IN_GLOW], wt[:, glow1:]], axis=1).astype(BF16)
    pad = jnp.zeros(w.shape[:-1] + (LANES - GLA_GATE_RANK,), w.dtype)
    glow = jnp.concatenate([w[..., W_IN_GLOW:glow1], pad], axis=-1).astype(BF16)
    return main, glow


def _row(v):
    return v.astype(F32)[:, None, :]


def kernel(x, w_in, s5_lam_re, s5_lam_im, s5_log_dt, s5_b_re, s5_b_im, s5_c_re, s5_c_im, s5_d, s5_w_glu, s5_b_glu, gla_w_gate, gla_b_gate, gla_norm_w, hgrn_lb_logits, hgrn_norm_w, w_up, w_out, ln1_g, ln1_b, ln2_g, ln2_b, w_mlp_in, w_mlp_out):
    bsz, seq, d = x.shape
    n = bsz * seq
    depth = w_in.shape[0]
    p = jax.nn.softmax(hgrn_lb_logits.astype(F32), axis=0)
    lb = _row(jnp.cumsum(p, axis=0) - p[0])
    w_in_b, w_glow_b = _split_w_in(w_in)
    s5w = jax.vmap(functools.partial(_s5_weights, folds_per_seq=seq // S5_FOLD))(
        s5_lam_re, s5_lam_im, s5_log_dt, s5_b_re, s5_b_im, s5_c_re, s5_c_im, s5_d)
    w_glu_b = s5_w_glu.astype(BF16)
    b_glu = _row(s5_b_glu)
    wg = jnp.concatenate([gla_w_gate, jnp.zeros((depth, LANES - GLA_GATE_RANK, GLA_KEY), gla_w_gate.dtype)],
                         axis=1).astype(BF16)
    bg, gla_nw, hgrn_nw = _row(gla_b_gate), _row(gla_norm_w), _row(hgrn_norm_w)
    w_up_b, w_out_b = w_up.astype(BF16), w_out.astype(BF16)
    w1_b, w2_b = w_mlp_in.astype(BF16), w_mlp_out.astype(BF16)
    g1, b1, g2, b2 = _row(ln1_g), _row(ln1_b), _row(ln2_g), _row(ln2_b)

    xf = x.reshape(n, d).astype(F32)
    xb = xf.astype(BF16)
    for l in range(depth):
        h = _proj(xb, w_in_b, l)
        ya = _glu(_s5_scan(h, bsz, seq, s5w, l), w_glu_b, b_glu, l)
        yb = _gla(h, xb, bsz, seq, w_glow_b, wg, bg, gla_nw, l)
        yc = _hgrn(h, bsz, seq, lb, hgrn_nw, l)
        xf, xb = _mixer_out(_merge(ya, yb, yc, h, w_up_b, l), w_out_b, xf, g1, b1, l)
        xf, xb = _mlp(xb, w1_b, w2_b, xf, g2, b2, l)
    return xf.reshape(bsz, seq, d).astype(x.dtype)
```

```python
import functools
import math

import jax
import jax.numpy as jnp
from jax import lax
from jax.experimental import pallas as pl
from jax.experimental.pallas import tpu as pltpu

F32 = jnp.float32
BF16 = jnp.bfloat16

D_MODEL = 2048
DEPTH = 4
N_BRANCH = 3
BRANCH_WIDTH = D_MODEL // 2
S5_GROUP = 16
S5_GROUPS = BRANCH_WIDTH // S5_GROUP
S5_STATE = 64
S5_MIN_DECAY = 1e-4
GLA_HEADS = 4
GLA_DV = BRANCH_WIDTH // GLA_HEADS
GLA_DK = GLA_DV // 2
GLA_KEY = GLA_HEADS * GLA_DK
GLA_GATE_RANK = 16
GLA_GATE_TAU = 16.0
HGRN_EXPAND = 128
HGRN_HEADS = BRANCH_WIDTH // HGRN_EXPAND
HGRN_DV = BRANCH_WIDTH // HGRN_HEADS
HGRN_KEY = HGRN_HEADS * HGRN_EXPAND
MLP_HIDDEN = 4 * D_MODEL
DN_ALPHA = (2 * DEPTH) ** 0.25
LN_EPS = 1e-5
RMS_EPS = 1e-6
LOG2E = 1.4426950408889634

LANES = 128
SUBLANES = 8
VMEM_LIMIT_BYTES = 56 * 1024 * 1024

W_IN_GLOW = BRANCH_WIDTH + 2 * GLA_KEY + BRANCH_WIDTH
HEAD_U = 0
HEAD_QB = HEAD_U + BRANCH_WIDTH
HEAD_KB = HEAD_QB + GLA_KEY
HEAD_VB = HEAD_KB + GLA_KEY
HEAD_WIDTH = HEAD_VB + BRANCH_WIDTH
TAIL_GATEB = 0
TAIL_QC = TAIL_GATEB + BRANCH_WIDTH
TAIL_FC = TAIL_QC + HGRN_KEY
TAIL_IC = TAIL_FC + HGRN_KEY
TAIL_GATEC = TAIL_IC + BRANCH_WIDTH
TAIL_WIDTH = TAIL_GATEC + BRANCH_WIDTH
MERGE_WIDTH = N_BRANCH * D_MODEL

S5_FOLD = SUBLANES
S5_TILE_GROUPS = LANES // S5_GROUP
S5_TILES = S5_GROUPS // S5_TILE_GROUPS
S5_TILE_STATE = S5_TILE_GROUPS * S5_STATE
S5_ROWS = S5_FOLD * LANES
ATT_CHUNK = 128
ATT_STEP_CHUNKS = 2
ROW_TILE = 512
PROJ_TM = 2048
PROJ_TN = 1024
MERGE_TM = 1024
MERGE_TILE = 512
HID_TILE = 1024


def _cparams(sem):
    return pltpu.CompilerParams(dimension_semantics=sem, vmem_limit_bytes=VMEM_LIMIT_BYTES)


def _dot(a, b):
    return jnp.dot(a, b, preferred_element_type=F32)


def _dot_nt(a, b):
    return lax.dot_general(a, b, (((1,), (1,)), ((), ())), preferred_element_type=F32)


def _dot_tn(a, b):
    return lax.dot_general(a, b, (((0,), (0,)), ((), ())), preferred_element_type=F32)


def _sigmoid(x):
    return 0.5 * jnp.tanh(0.5 * x) + 0.5


def _sigmoid_rel(x):
    return 1.0 / (1.0 + jnp.exp2(x * -LOG2E))


def _layer_norm(v, g, b):
    mu = jnp.mean(v, axis=-1, keepdims=True)
    xc = v - mu
    var = jnp.mean(xc * xc, axis=-1, keepdims=True)
    return xc * lax.rsqrt(var + LN_EPS) * g + b


def _proj_kernel(x_ref, w_ref, o_ref):
    o_ref[...] = _dot(x_ref[...], w_ref[...]).astype(o_ref.dtype)


def _proj(xb, w, layer, col0, width, out_dtype):
    n, k = xb.shape
    tm = min(PROJ_TM, n)
    return pl.pallas_call(
        _proj_kernel,
        grid=(n // tm, width // PROJ_TN),
        in_specs=[pl.BlockSpec((tm, k), lambda i, j: (i, 0)),
                  pl.BlockSpec((None, k, PROJ_TN), lambda i, j: (layer, 0, col0 // PROJ_TN + j))],
        out_specs=pl.BlockSpec((tm, PROJ_TN), lambda i, j: (i, j)),
        out_shape=jax.ShapeDtypeStruct((n, width), out_dtype),
        compiler_params=_cparams(("parallel", "arbitrary")),
        name="proj",
    )(xb, w)


def _s5_expand(ky_ref, kv_ref, kq_ref, my_ref, mv_ref, mq_ref):
    kk = lax.broadcasted_iota(jnp.int32, (LANES, S5_ROWS), 0)
    cc = lax.broadcasted_iota(jnp.int32, (LANES, S5_ROWS), 1)
    e_tc = jnp.where(((kk >> 4) == (cc >> 7)) & ((kk & 15) == (cc & 15)), 1.0, 0.0).astype(BF16)
    e_rp = jnp.where(((kk >> 6) == (cc >> 9)) & ((kk & 63) == (cc & 63)), 1.0, 0.0).astype(BF16)
    rr = lax.broadcasted_iota(jnp.int32, (S5_ROWS, S5_ROWS), 0)
    cc = lax.broadcasted_iota(jnp.int32, (S5_ROWS, S5_ROWS), 1)
    g_jgd, g_rgp = (rr >> 4) & 7, (rr >> 6) & 7
    h_thc, h_rhp = (cc >> 4) & 7, (cc >> 6) & 7
    my_ref[...] = jnp.where(g_jgd == h_thc, _dot(ky_ref[...], e_tc), 0.0).astype(BF16)
    mv_ref[...] = jnp.where(g_jgd == h_rhp, _dot(kv_ref[...], e_rp), 0.0).astype(BF16)
    mq_ref[...] = jnp.where(g_rgp == h_thc, _dot(kq_ref[...], e_tc), 0.0).astype(BF16)


def _scan_steps(s_re, s_im, pos, apr_ref, api_ref, first_step, length):
    step, dist = first_step, 1
    while dist < length:
        ok = pos >= dist
        sh_re = jnp.where(ok, pltpu.roll(s_re, dist, axis=0), 0.0)
        sh_im = jnp.where(ok, pltpu.roll(s_im, dist, axis=0), 0.0)
        ar = apr_ref[step:step + 1, :]
        ai = api_ref[step:step + 1, :]
        s_re, s_im = s_re + ar * sh_re - ai * sh_im, s_im + ar * sh_im + ai * sh_re
        step += 1
        dist *= 2
    return s_re, s_im


def _s5_kernel(u_ref, ky_ref, kv_ref, kq_ref, d_ref, apr_ref, api_ref, pwr_ref, pwi_ref, y_ref,
               my_ref, mv_ref, mq_ref, sre_ref, sim_ref, pre_ref, pim_ref):
    @pl.when(pl.program_id(1) == 0)
    def _():
        _s5_expand(ky_ref, kv_ref, kq_ref, my_ref, mv_ref, mq_ref)

    folds = u_ref.shape[0] // S5_FOLD
    runs = folds // SUBLANES
    us = [u_ref[pl.ds(t, folds, stride=S5_FOLD), :] for t in range(S5_FOLD)]
    ucat = jnp.concatenate([u.astype(BF16) for u in us], axis=1)
    y_intra = _dot(ucat, my_ref[...])
    r = _dot(ucat, mv_ref[...])
    fold = lax.broadcasted_iota(jnp.int32, (folds, S5_TILE_STATE), 0)
    s_re, s_im = _scan_steps(r[:, :S5_TILE_STATE], r[:, S5_TILE_STATE:], fold & (SUBLANES - 1),
                             apr_ref, api_ref, 0, SUBLANES)
    lane_tiles = S5_TILE_STATE // LANES

    def put(ref, j, val):
        for lt in range(lane_tiles):
            ref[lt, pl.ds(j, runs, stride=SUBLANES), :] = val[:, lt * LANES:(lt + 1) * LANES]

    def get(ref, j):
        return jnp.concatenate([ref[lt, pl.ds(j, runs, stride=SUBLANES), :] for lt in range(lane_tiles)], axis=1)

    for lt in range(lane_tiles):
        sre_ref[lt] = s_re[:, lt * LANES:(lt + 1) * LANES]
        sim_ref[lt] = s_im[:, lt * LANES:(lt + 1) * LANES]
    last = SUBLANES - 1
    run = lax.broadcasted_iota(jnp.int32, (runs, S5_TILE_STATE), 0)
    e_re, e_im = _scan_steps(get(sre_ref, last), get(sim_ref, last), run, apr_ref, api_ref,
                             int(math.log2(SUBLANES)), runs)
    c_re = jnp.where(run >= 1, pltpu.roll(e_re, 1, axis=0), 0.0)
    c_im = jnp.where(run >= 1, pltpu.roll(e_im, 1, axis=0), 0.0)
    put(pre_ref, 0, c_re)
    put(pim_ref, 0, c_im)
    for j in range(last):
        l_re, l_im = get(sre_ref, j), get(sim_ref, j)
        br, bi = pwr_ref[j:j + 1, :], pwi_ref[j:j + 1, :]
        put(pre_ref, j + 1, l_re + br * c_re - bi * c_im)
        put(pim_ref, j + 1, l_im + br * c_im + bi * c_re)
    prev = jnp.concatenate([pre_ref[lt] for lt in range(lane_tiles)]
                           + [pim_ref[lt] for lt in range(lane_tiles)], axis=1).astype(BF16)
    y = y_intra + _dot(prev, mq_ref[...])
    for t in range(S5_FOLD):
        y_ref[pl.ds(t, folds, stride=S5_FOLD), :] = y[:, t * LANES:(t + 1) * LANES] + d_ref[...] * us[t]


def _s5_scan(h, bsz, seq, s5w, layer):
    k_y, k_v, k_q, d_t, ap_re, ap_im, pw_re, pw_im = s5w
    nsteps = ap_re.shape[2]
    folds = seq // S5_FOLD
    tile = lambda j, b: (layer, j, 0, 0)
    compact = pl.BlockSpec((None, None, S5_ROWS, LANES), tile)
    steps = pl.BlockSpec((None, None, nsteps, S5_TILE_STATE), tile)
    powers = pl.BlockSpec((None, None, SUBLANES, S5_TILE_STATE), tile)
    return pl.pallas_call(
        _s5_kernel,
        grid=(S5_TILES, bsz),
        in_specs=[pl.BlockSpec((seq, LANES), lambda j, b: (b, HEAD_U // LANES + j)),
                  compact, compact, compact,
                  pl.BlockSpec((None, None, 1, LANES), tile),
                  steps, steps, powers, powers],
        out_specs=pl.BlockSpec((seq, LANES), lambda j, b: (b, j)),
        out_shape=jax.ShapeDtypeStruct((bsz * seq, BRANCH_WIDTH), F32),
        scratch_shapes=[pltpu.VMEM((S5_ROWS, S5_ROWS), BF16)] * 3
        + [pltpu.VMEM((S5_TILE_STATE // LANES, folds, LANES), F32)] * 4,
        compiler_params=_cparams(("parallel", "arbitrary")),
        name="s5_scan",
    )(h, k_y, k_v, k_q, d_t, ap_re, ap_im, pw_re, pw_im)


def _s5_weights(lam_re, lam_im, log_dt, b_re, b_im, c_re, c_im, d_skip, folds_per_seq):
    t_fold, g_n, p_n, c_n, tg = S5_FOLD, S5_GROUPS, S5_STATE, S5_GROUP, S5_TILE_GROUPS
    tiles = g_n // tg
    lr = jnp.minimum(lam_re.astype(F32), -S5_MIN_DECAY)
    li = lam_im.astype(F32)
    dt = jnp.exp(log_dt.astype(F32))[:, None]
    mag = jnp.exp(lr * dt)
    abar_re, abar_im = mag * jnp.cos(li * dt), mag * jnp.sin(li * dt)
    den = lr * lr + li * li
    fac_re = ((abar_re - 1.0) * lr + abar_im * li) / den
    fac_im = (abar_im * lr - (abar_re - 1.0) * li) / den
    br, bi = b_re.astype(F32), b_im.astype(F32)
    bbar_re = fac_re[..., None] * br - fac_im[..., None] * bi
    bbar_im = fac_re[..., None] * bi + fac_im[..., None] * br

    def apow(n):
        nn = jnp.asarray(n, F32)[:, None, None]
        m = jnp.exp(nn * (lr * dt)[None])
        return m * jnp.cos(nn * (li * dt)[None]), m * jnp.sin(nn * (li * dt)[None])

    pr, pi = apow(jnp.arange(t_fold + 1))
    cr, ci = c_re.astype(F32), c_im.astype(F32)
    ca_re = cr[None] * pr[:, :, None, :] - ci[None] * pi[:, :, None, :]
    ca_im = cr[None] * pi[:, :, None, :] + ci[None] * pr[:, :, None, :]
    k_tau = jnp.sum(ca_re[:t_fold, :, :, :, None] * bbar_re[None, :, None, :, :]
                    - ca_im[:t_fold, :, :, :, None] * bbar_im[None, :, None, :, :], axis=3)
    jj = jnp.arange(t_fold)[:, None]
    tt = jnp.arange(t_fold)[None, :]
    lag = jnp.clip(tt - jj, 0, t_fold - 1)
    toe = jnp.where((tt >= jj)[:, :, None, None, None], k_tau[lag], 0.0)
    k_y = (toe.reshape(t_fold, t_fold, tiles, tg, c_n, c_n).transpose(2, 0, 3, 5, 1, 4)
           .reshape(tiles, S5_ROWS, LANES))
    rev = t_fold - 1 - jnp.arange(t_fold)
    prj, pij = pr[rev], pi[rev]
    pv = jnp.stack([prj[..., None] * bbar_re[None] - pij[..., None] * bbar_im[None],
                    prj[..., None] * bbar_im[None] + pij[..., None] * bbar_re[None]])
    k_v = (pv.reshape(2, t_fold, tiles, tg, p_n, c_n).transpose(2, 1, 3, 5, 0, 4)
           .reshape(tiles, S5_ROWS, LANES))
    qa = jnp.stack([ca_re[1:], -ca_im[1:]])
    k_q = (qa.reshape(2, t_fold, tiles, tg, c_n, p_n).transpose(2, 0, 3, 5, 1, 4)
           .reshape(tiles, S5_ROWS, LANES))
    d_t = d_skip.astype(F32).reshape(tiles, 1, LANES)
    nsteps = max(int(math.log2(folds_per_seq)), 1)
    sr, si = apow(t_fold * (2 ** jnp.arange(nsteps)))
    ap_re = sr.reshape(nsteps, tiles, tg * p_n).transpose(1, 0, 2)
    ap_im = si.reshape(nsteps, tiles, tg * p_n).transpose(1, 0, 2)
    wr, wi = apow(t_fold * (1 + jnp.arange(SUBLANES)))
    pw_re = wr.reshape(SUBLANES, tiles, tg * p_n).transpose(1, 0, 2)
    pw_im = wi.reshape(SUBLANES, tiles, tg * p_n).transpose(1, 0, 2)
    return k_y.astype(BF16), k_v.astype(BF16), k_q.astype(BF16), d_t, ap_re, ap_im, pw_re, pw_im


def _glu_kernel(y_ref, w_ref, b_ref, o_ref):
    y = y_ref[...]
    z = 0.5 * y * (1.0 + jnp.tanh(math.sqrt(2.0 / math.pi) * (y + 0.044715 * (y * y * y))))
    a = _dot(z.astype(BF16), w_ref[...]) + b_ref[...]
    o_ref[...] = (z * _sigmoid(a)).astype(o_ref.dtype)


def _glu(y, w, b, layer):
    n, width = y.shape
    tm = min(ROW_TILE, n)
    return pl.pallas_call(
        _glu_kernel,
        grid=(n // tm,),
        in_specs=[pl.BlockSpec((tm, width), lambda i: (i, 0)),
                  pl.BlockSpec((None, width, width), lambda i: (layer, 0, 0)),
                  pl.BlockSpec((None, 1, width), lambda i: (layer, 0, 0))],
        out_specs=pl.BlockSpec((tm, width), lambda i: (i, 0)),
        out_shape=jax.ShapeDtypeStruct((n, width), BF16),
        compiler_params=_cparams(("parallel",)),
        name="s5_glu",
    )(y, w, b)


def _gated_chunk(q, k, v, g, st_ref):
    c, kw = q.shape
    heads = kw // LANES
    dv = v.shape[1] // heads
    row = lax.broadcasted_iota(jnp.int32, (c, kw), 0)
    ri = lax.broadcasted_iota(jnp.int32, (c, c), 0)
    ci = lax.broadcasted_iota(jnp.int32, (c, c), 1)
    xr = jnp.where(ri > ci, ri ^ ci, 0)
    ksl = [slice(hd * LANES, (hd + 1) * LANES) for hd in range(heads)]
    vsl = [slice(hd * dv, (hd + 1) * dv) for hd in range(heads)]
    tril = jnp.where(ri >= ci, 1.0, 0.0).astype(BF16)
    g_hi = g.astype(BF16)
    g_rest = g - g_hi.astype(F32)
    g_mid = g_rest.astype(BF16)
    g_lo = (g_rest - g_mid.astype(F32)).astype(BF16)
    gc = _dot(tril, g_hi) + _dot(tril, g_mid) + _dot(tril, g_lo)

    def block_row(x, blk, pos):
        x3 = x.reshape(c // blk, blk, kw)
        return jnp.broadcast_to(x3[:, pos:pos + 1, :], x3.shape).reshape(c, kw)

    qb, kb = q.astype(BF16), k.astype(BF16)
    a = [jnp.where(ri == ci, _dot_nt(qb[:, s], kb[:, s]), 0.0) for s in ksl]
    end = gc
    half = 1
    while half < c:
        if half < SUBLANES:
            first = (row & half) == 0
            ref = jnp.where(first, end, pltpu.roll(end, half, axis=0))
            if 2 * half < SUBLANES:
                end = jnp.where(first, pltpu.roll(end, c - half, axis=0), end)
        else:
            ref = block_row(gc, 2 * half, half - 1)
        e = jnp.exp2(jnp.abs(gc - ref) * -LOG2E).astype(BF16)
        eq, ek = e * qb, e * kb
        level = xr >= half
        a = [jnp.where(level, _dot_nt(eq[:, s], ek[:, s]), a_h) for s, a_h in zip(ksl, a)]
        half *= 2
    last = block_row(gc, c, c - 1)
    vb = v.astype(BF16)
    qg = (q * jnp.exp(gc)).astype(BF16)
    kd = (k * jnp.exp(last - gc)).astype(BF16)
    keep = jnp.exp(last[0:1, :])
    outs = []
    for hd in range(heads):
        st = st_ref[hd]
        outs.append(_dot(a[hd].astype(BF16), vb[:, vsl[hd]]) + _dot_nt(qg[:, ksl[hd]], st.astype(BF16)))
        st_ref[hd] = st * keep[:, ksl[hd]] + _dot_tn(vb[:, vsl[hd]], kd[:, ksl[hd]])
    return outs


def _gla_kernel(q_ref, k_ref, v_ref, gate_ref, xb_ref, wl_ref, wg_ref, bg_ref, nw_ref, o_ref, st_ref):
    @pl.when(pl.program_id(1) == 0)
    def _():
        st_ref[...] = jnp.zeros_like(st_ref)

    for rows in _chunk_rows(q_ref.shape[0]):
        glow = _dot(xb_ref[rows, :], wl_ref[...]).astype(BF16)
        z = _dot(glow, wg_ref[...]) + bg_ref[...]
        g = (jnp.minimum(z, 0.0) - jnp.log1p(jnp.exp(-jnp.abs(z)))) / GLA_GATE_TAU
        outs = _gated_chunk(q_ref[rows, :] * (GLA_DK ** -0.5), k_ref[rows, :], v_ref[rows, :], g, st_ref)
        for hd, o in enumerate(outs):
            vs = slice(hd * GLA_DV, (hd + 1) * GLA_DV)
            o = o * lax.rsqrt(jnp.mean(o * o, axis=-1, keepdims=True) + RMS_EPS) * nw_ref[...]
            gate = gate_ref[rows, vs]
            o_ref[rows, vs] = (o * (gate * _sigmoid(gate))).astype(o_ref.dtype)


def _chunk_rows(step_rows):
    c = min(ATT_CHUNK, step_rows)
    return [slice(i * c, (i + 1) * c) for i in range(step_rows // c)]


def _att_specs(seq):
    rows = min(ATT_CHUNK * ATT_STEP_CHUNKS, seq)
    steps = seq // rows

    def col(off, width):
        return pl.BlockSpec((rows, width), lambda b, i: (b * steps + i, off // width))

    return rows, steps, col


def _gla(head, tail, xb, bsz, seq, wl, wg, bg, nw, layer):
    c, nc, col = _att_specs(seq)
    vec = lambda b, i: (layer, 0, 0)
    return pl.pallas_call(
        _gla_kernel,
        grid=(bsz, nc),
        in_specs=[col(HEAD_QB, GLA_KEY), col(HEAD_KB, GLA_KEY), col(HEAD_VB, BRANCH_WIDTH),
                  col(TAIL_GATEB, BRANCH_WIDTH),
                  pl.BlockSpec((c, D_MODEL), lambda b, i: (b * nc + i, 0)),
                  pl.BlockSpec((None, D_MODEL, LANES), vec),
                  pl.BlockSpec((None, LANES, GLA_KEY), vec),
                  pl.BlockSpec((None, 1, GLA_KEY), vec),
                  pl.BlockSpec((None, 1, GLA_DV), vec)],
        out_specs=pl.BlockSpec((c, BRANCH_WIDTH), lambda b, i: (b * nc + i, 0)),
        out_shape=jax.ShapeDtypeStruct((bsz * seq, BRANCH_WIDTH), BF16),
        scratch_shapes=[pltpu.VMEM((GLA_HEADS, GLA_DV, GLA_DK), F32)],
        compiler_params=_cparams(("parallel", "arbitrary")),
        name="gla",
    )(head, head, head, tail, xb, wl, wg, bg, nw)


def _hgrn_kernel(q_ref, f_ref, i_ref, gate_ref, lb_ref, nw_ref, o_ref, st_ref):
    @pl.when(pl.program_id(1) == 0)
    def _():
        st_ref[...] = jnp.zeros_like(st_ref)

    lb = lb_ref[...]
    for rows in _chunk_rows(q_ref.shape[0]):
        f = lb + (1.0 - lb) * _sigmoid_rel(f_ref[rows, :])
        qr = q_ref[rows, :]
        outs = _gated_chunk(qr * _sigmoid(qr), 1.0 - f, i_ref[rows, :], jnp.log(f), st_ref)
        for hd, o in enumerate(outs):
            vs = slice(hd * HGRN_DV, (hd + 1) * HGRN_DV)
            o = o * _sigmoid(gate_ref[rows, vs])
            o = o * lax.rsqrt(jnp.mean(o * o, axis=-1, keepdims=True) + RMS_EPS) * nw_ref[...]
            o_ref[rows, vs] = o.astype(o_ref.dtype)


def _hgrn(tail, bsz, seq, lb, nw, layer):
    c, nc, col = _att_specs(seq)
    vec = lambda b, i: (layer, 0, 0)
    return pl.pallas_call(
        _hgrn_kernel,
        grid=(bsz, nc),
        in_specs=[col(TAIL_QC, HGRN_KEY), col(TAIL_FC, HGRN_KEY), col(TAIL_IC, BRANCH_WIDTH),
                  col(TAIL_GATEC, BRANCH_WIDTH),
                  pl.BlockSpec((None, 1, HGRN_KEY), vec),
                  pl.BlockSpec((None, 1, HGRN_DV), vec)],
        out_specs=pl.BlockSpec((c, BRANCH_WIDTH), lambda b, i: (b * nc + i, 0)),
        out_shape=jax.ShapeDtypeStruct((bsz * seq, BRANCH_WIDTH), BF16),
        scratch_shapes=[pltpu.VMEM((HGRN_HEADS, HGRN_DV, HGRN_EXPAND), F32)],
        compiler_params=_cparams(("parallel", "arbitrary")),
        name="hgrn",
    )(tail, tail, tail, tail, lb, nw)


def _zero_first_step(o_ref):
    @pl.when(pl.program_id(1) == 0)
    def _():
        o_ref[...] = jnp.zeros_like(o_ref)


def _norm_last_step(x_ref, lg_ref, lb_ref, o_ref, ob_ref):
    @pl.when(pl.program_id(1) == pl.num_programs(1) - 1)
    def _():
        xn = _layer_norm(DN_ALPHA * x_ref[...] + o_ref[...], lg_ref[...], lb_ref[...])
        o_ref[...] = xn
        ob_ref[...] = xn.astype(BF16)


def _merge_kernel(ya_ref, yb_ref, yc_ref, ga_ref, gb_ref, gc_ref, wup_ref, o_ref):
    merged = (_sigmoid(ga_ref[...].astype(F32)) * _dot(ya_ref[...], wup_ref[0])
              + _sigmoid(gb_ref[...].astype(F32)) * _dot(yb_ref[...], wup_ref[1])
              + _sigmoid(gc_ref[...].astype(F32)) * _dot(yc_ref[...], wup_ref[2]))
    o_ref[...] = merged.astype(o_ref.dtype)


def _merge(ya, yb, yc, gates, wup, layer):
    n = ya.shape[0]
    tm = min(MERGE_TM, n)
    tn = MERGE_TILE

    def gate(branch):
        off = branch * D_MODEL // tn
        return pl.BlockSpec((tm, tn), lambda i, j: (i, off + j))

    ybs = pl.BlockSpec((tm, BRANCH_WIDTH), lambda i, j: (i, 0))
    return pl.pallas_call(
        _merge_kernel,
        grid=(n // tm, D_MODEL // tn),
        in_specs=[ybs, ybs, ybs, gate(0), gate(1), gate(2),
                  pl.BlockSpec((None, N_BRANCH, BRANCH_WIDTH, tn), lambda i, j: (layer, 0, 0, j))],
        out_specs=pl.BlockSpec((tm, tn), lambda i, j: (i, j)),
        out_shape=jax.ShapeDtypeStruct((n, D_MODEL), BF16),
        compiler_params=_cparams(("parallel", "arbitrary")),
        name="merge",
    )(ya, yb, yc, gates, gates, gates, wup)


def _mixer_out_kernel(m_ref, wout_ref, x_ref, lg_ref, lb_ref, o_ref, ob_ref):
    xn = _layer_norm(DN_ALPHA * x_ref[...] + _dot(m_ref[...], wout_ref[...]), lg_ref[...], lb_ref[...])
    o_ref[...] = xn
    ob_ref[...] = xn.astype(BF16)


def _mixer_out(merged, wout, x, ln_g, ln_b, layer):
    n = x.shape[0]
    tm = min(ROW_TILE, n)
    row = pl.BlockSpec((tm, D_MODEL), lambda i: (i, 0))
    vec = pl.BlockSpec((None, 1, D_MODEL), lambda i: (layer, 0, 0))
    return pl.pallas_call(
        _mixer_out_kernel,
        grid=(n // tm,),
        in_specs=[row, pl.BlockSpec((None, D_MODEL, D_MODEL), lambda i: (layer, 0, 0)), row, vec, vec],
        out_specs=[row, row],
        out_shape=[jax.ShapeDtypeStruct((n, D_MODEL), F32), jax.ShapeDtypeStruct((n, D_MODEL), BF16)],
        compiler_params=_cparams(("parallel",)),
        name="mixer_out",
    )(merged, wout, x, ln_g, ln_b)


def _mlp_kernel(xb_ref, w1_ref, w2_ref, x_ref, lg_ref, lb_ref, o_ref, ob_ref):
    _zero_first_step(o_ref)
    hid = jnp.maximum(_dot(xb_ref[...], w1_ref[...]), 0.0)
    o_ref[...] += _dot((hid * hid).astype(BF16), w2_ref[...])
    _norm_last_step(x_ref, lg_ref, lb_ref, o_ref, ob_ref)


def _mlp(xb, w1, w2, x, ln_g, ln_b, layer):
    n = x.shape[0]
    tm = min(ROW_TILE, n)
    th = HID_TILE
    row = pl.BlockSpec((tm, D_MODEL), lambda i, k: (i, 0))
    vec = pl.BlockSpec((None, 1, D_MODEL), lambda i, k: (layer, 0, 0))
    return pl.pallas_call(
        _mlp_kernel,
        grid=(n // tm, MLP_HIDDEN // th),
        in_specs=[row,
                  pl.BlockSpec((None, D_MODEL, th), lambda i, k: (layer, 0, k)),
                  pl.BlockSpec((None, th, D_MODEL), lambda i, k: (layer, k, 0)),
                  row, vec, vec],
        out_specs=[row, row],
        out_shape=[jax.ShapeDtypeStruct((n, D_MODEL), F32), jax.ShapeDtypeStruct((n, D_MODEL), BF16)],
        compiler_params=_cparams(("parallel", "arbitrary")),
        name="mlp",
    )(xb, w1, w2, x, ln_g, ln_b)


def _split_w_in(w):
    glow1 = W_IN_GLOW + GLA_GATE_RANK
    pad = jnp.zeros(w.shape[:-1] + (LANES - GLA_GATE_RANK,), w.dtype)
    glow = jnp.concatenate([w[..., W_IN_GLOW:glow1], pad], axis=-1).astype(BF16)
    return w[..., :W_IN_GLOW].astype(BF16), w[..., glow1:].astype(BF16), glow


def _row(v):
    return v.astype(F32)[:, None, :]


def kernel(x, w_in, s5_lam_re, s5_lam_im, s5_log_dt, s5_b_re, s5_b_im, s5_c_re, s5_c_im, s5_d, s5_w_glu, s5_b_glu, gla_w_gate, gla_b_gate, gla_norm_w, hgrn_lb_logits, hgrn_norm_w, w_up, w_out, ln1_g, ln1_b, ln2_g, ln2_b, w_mlp_in, w_mlp_out):
    bsz, seq, d = x.shape
    n = bsz * seq
    depth = w_in.shape[0]
    p = jax.nn.softmax(hgrn_lb_logits.astype(F32), axis=0)
    lb = _row(jnp.cumsum(p, axis=0) - p[0])
    w_head_b, w_tail_b, w_glow_b = _split_w_in(w_in)
    s5w = jax.vmap(functools.partial(_s5_weights, folds_per_seq=seq // S5_FOLD))(
        s5_lam_re, s5_lam_im, s5_log_dt, s5_b_re, s5_b_im, s5_c_re, s5_c_im, s5_d)
    w_glu_b = s5_w_glu.astype(BF16)
    b_glu = _row(s5_b_glu)
    wg = jnp.concatenate([gla_w_gate, jnp.zeros((depth, LANES - GLA_GATE_RANK, GLA_KEY), gla_w_gate.dtype)],
                         axis=1).astype(BF16)
    bg, gla_nw, hgrn_nw = _row(gla_b_gate), _row(gla_norm_w), _row(hgrn_norm_w)
    w_up_b, w_out_b = w_up.astype(BF16), w_out.astype(BF16)
    w1_b, w2_b = w_mlp_in.astype(BF16), w_mlp_out.astype(BF16)
    g1, b1, g2, b2 = _row(ln1_g), _row(ln1_b), _row(ln2_g), _row(ln2_b)

    xf = x.reshape(n, d).astype(F32)
    xb = xf.astype(BF16)
    for l in range(depth):
        head = _proj(xb, w_head_b, l, 0, HEAD_WIDTH, F32)
        tail = _proj(xb, w_tail_b, l, 0, TAIL_WIDTH, F32)
        gates = _proj(xb, w_tail_b, l, TAIL_WIDTH, MERGE_WIDTH, BF16)
        ya = _glu(_s5_scan(head, bsz, seq, s5w, l), w_glu_b, b_glu, l)
        yb = _gla(head, tail, xb, bsz, seq, w_glow_b, wg, bg, gla_nw, l)
        yc = _hgrn(tail, bsz, seq, lb, hgrn_nw, l)
        xf, xb = _mixer_out(_merge(ya, yb, yc, gates, w_up_b, l), w_out_b, xf, g1, b1, l)
        xf, xb = _mlp(xb, w1_b, w2_b, xf, g2, b2, l)
    return xf.reshape(bsz, seq, d).astype(x.dtype)
```

```python
import functools
import math

import jax
import jax.numpy as jnp
from jax import lax
from jax.experimental import pallas as pl
from jax.experimental.pallas import tpu as pltpu

F32 = jnp.float32
BF16 = jnp.bfloat16

D_MODEL = 2048
DEPTH = 4
N_BRANCH = 3
BRANCH_WIDTH = D_MODEL // 2
S5_GROUP = 16
S5_GROUPS = BRANCH_WIDTH // S5_GROUP
S5_STATE = 64
S5_MIN_DECAY = 1e-4
GLA_HEADS = 4
GLA_DV = BRANCH_WIDTH // GLA_HEADS
GLA_DK = GLA_DV // 2
GLA_KEY = GLA_HEADS * GLA_DK
GLA_GATE_RANK = 16
GLA_GATE_TAU = 16.0
HGRN_EXPAND = 128
HGRN_HEADS = BRANCH_WIDTH // HGRN_EXPAND
HGRN_DV = BRANCH_WIDTH // HGRN_HEADS
HGRN_KEY = HGRN_HEADS * HGRN_EXPAND
MLP_HIDDEN = 4 * D_MODEL
DN_ALPHA = (2 * DEPTH) ** 0.25
LN_EPS = 1e-5
RMS_EPS = 1e-6
LOG2E = 1.4426950408889634

LANES = 128
SUBLANES = 8
VMEM_LIMIT_BYTES = 56 * 1024 * 1024

W_IN_GLOW = BRANCH_WIDTH + 2 * GLA_KEY + BRANCH_WIDTH
W_IN_GATEB = W_IN_GLOW + GLA_GATE_RANK
W_IN_HGRN = W_IN_GATEB + BRANCH_WIDTH
W_IN_MERGE = W_IN_HGRN + 2 * HGRN_KEY + 2 * BRANCH_WIDTH
GLA_U = 0
GLA_Q = GLA_U + BRANCH_WIDTH
GLA_K = GLA_Q + GLA_KEY
GLA_V = GLA_K + GLA_KEY
GLA_GATE = GLA_V + BRANCH_WIDTH
GLA_GLOW = GLA_GATE + BRANCH_WIDTH
GLA_COLS = GLA_GLOW + LANES
HGRN_Q = 0
HGRN_F = HGRN_Q + HGRN_KEY
HGRN_I = HGRN_F + HGRN_KEY
HGRN_GATE = HGRN_I + BRANCH_WIDTH
HGRN_COLS = HGRN_GATE + BRANCH_WIDTH
MERGE_WIDTH = N_BRANCH * D_MODEL

S5_FOLD = SUBLANES
S5_TILE_GROUPS = LANES // S5_GROUP
S5_TILES = S5_GROUPS // S5_TILE_GROUPS
S5_TILE_STATE = S5_TILE_GROUPS * S5_STATE
S5_ROWS = S5_FOLD * LANES
ATT_CHUNK = 128
ATT_STEP_ROWS = 512
PROJ_PIECE = 256
PROJ_PIECE_ROWS = 256
ROW_TILE = 512
PROJ_TM = 2048
PROJ_TN = 1024
MERGE_TM = 1024
MERGE_TILE = 512
HID_TILE = 1024


def _cparams(sem):
    return pltpu.CompilerParams(dimension_semantics=sem, vmem_limit_bytes=VMEM_LIMIT_BYTES)


def _dot(a, b):
    return jnp.dot(a, b, preferred_element_type=F32)


def _dot_nt(a, b):
    return lax.dot_general(a, b, (((1,), (1,)), ((), ())), preferred_element_type=F32)


def _dot_tn(a, b):
    return lax.dot_general(a, b, (((0,), (0,)), ((), ())), preferred_element_type=F32)


def _sigmoid(x):
    return 0.5 * jnp.tanh(0.5 * x) + 0.5


def _sigmoid_rel(x):
    return 1.0 / (1.0 + jnp.exp2(x * -LOG2E))


def _layer_norm(v, g, b):
    mu = jnp.mean(v, axis=-1, keepdims=True)
    xc = v - mu
    var = jnp.mean(xc * xc, axis=-1, keepdims=True)
    return xc * lax.rsqrt(var + LN_EPS) * g + b


def _proj_kernel(x_ref, w_ref, o_ref):
    o_ref[...] = _dot(x_ref[...], w_ref[...]).astype(o_ref.dtype)


def _proj(xb, w, layer, col0, width, out_dtype):
    n, k = xb.shape
    tm = min(PROJ_TM, n)
    return pl.pallas_call(
        _proj_kernel,
        grid=(n // tm, width // PROJ_TN),
        in_specs=[pl.BlockSpec((tm, k), lambda i, j: (i, 0)),
                  pl.BlockSpec((None, k, PROJ_TN), lambda i, j: (layer, 0, col0 // PROJ_TN + j))],
        out_specs=pl.BlockSpec((tm, PROJ_TN), lambda i, j: (i, j)),
        out_shape=jax.ShapeDtypeStruct((n, width), out_dtype),
        compiler_params=_cparams(("parallel", "arbitrary")),
        name="proj",
    )(xb, w)


def _s5_expand(ky_ref, kv_ref, kq_ref, my_ref, mv_ref, mq_ref):
    kk = lax.broadcasted_iota(jnp.int32, (LANES, S5_ROWS), 0)
    cc = lax.broadcasted_iota(jnp.int32, (LANES, S5_ROWS), 1)
    e_tc = jnp.where(((kk >> 4) == (cc >> 7)) & ((kk & 15) == (cc & 15)), 1.0, 0.0).astype(BF16)
    e_rp = jnp.where(((kk >> 6) == (cc >> 9)) & ((kk & 63) == (cc & 63)), 1.0, 0.0).astype(BF16)
    rr = lax.broadcasted_iota(jnp.int32, (S5_ROWS, S5_ROWS), 0)
    cc = lax.broadcasted_iota(jnp.int32, (S5_ROWS, S5_ROWS), 1)
    g_jgd, g_rgp = (rr >> 4) & 7, (rr >> 6) & 7
    h_thc, h_rhp = (cc >> 4) & 7, (cc >> 6) & 7
    my_ref[...] = jnp.where(g_jgd == h_thc, _dot(ky_ref[...], e_tc), 0.0).astype(BF16)
    mv_ref[...] = jnp.where(g_jgd == h_rhp, _dot(kv_ref[...], e_rp), 0.0).astype(BF16)
    mq_ref[...] = jnp.where(g_rgp == h_thc, _dot(kq_ref[...], e_tc), 0.0).astype(BF16)


def _scan_steps(s_re, s_im, pos, apr_ref, api_ref, first_step, length):
    step, dist = first_step, 1
    while dist < length:
        ok = pos >= dist
        sh_re = jnp.where(ok, pltpu.roll(s_re, dist, axis=0), 0.0)
        sh_im = jnp.where(ok, pltpu.roll(s_im, dist, axis=0), 0.0)
        ar = apr_ref[step:step + 1, :]
        ai = api_ref[step:step + 1, :]
        s_re, s_im = s_re + ar * sh_re - ai * sh_im, s_im + ar * sh_im + ai * sh_re
        step += 1
        dist *= 2
    return s_re, s_im


def _s5_kernel(u_ref, ky_ref, kv_ref, kq_ref, d_ref, apr_ref, api_ref, pwr_ref, pwi_ref, y_ref,
               my_ref, mv_ref, mq_ref, sre_ref, sim_ref, pre_ref, pim_ref):
    @pl.when(pl.program_id(1) == 0)
    def _():
        _s5_expand(ky_ref, kv_ref, kq_ref, my_ref, mv_ref, mq_ref)

    folds = u_ref.shape[0] // S5_FOLD
    runs = folds // SUBLANES
    us = [u_ref[pl.ds(t, folds, stride=S5_FOLD), :] for t in range(S5_FOLD)]
    ucat = jnp.concatenate([u.astype(BF16) for u in us], axis=1)
    y_intra = _dot(ucat, my_ref[...])
    r = _dot(ucat, mv_ref[...])
    fold = lax.broadcasted_iota(jnp.int32, (folds, S5_TILE_STATE), 0)
    s_re, s_im = _scan_steps(r[:, :S5_TILE_STATE], r[:, S5_TILE_STATE:], fold & (SUBLANES - 1),
                             apr_ref, api_ref, 0, SUBLANES)
    lane_tiles = S5_TILE_STATE // LANES

    def put(ref, j, val):
        for lt in range(lane_tiles):
            ref[lt, pl.ds(j, runs, stride=SUBLANES), :] = val[:, lt * LANES:(lt + 1) * LANES]

    def get(ref, j):
        return jnp.concatenate([ref[lt, pl.ds(j, runs, stride=SUBLANES), :] for lt in range(lane_tiles)], axis=1)

    for lt in range(lane_tiles):
        sre_ref[lt] = s_re[:, lt * LANES:(lt + 1) * LANES]
        sim_ref[lt] = s_im[:, lt * LANES:(lt + 1) * LANES]
    last = SUBLANES - 1
    run = lax.broadcasted_iota(jnp.int32, (runs, S5_TILE_STATE), 0)
    e_re, e_im = _scan_steps(get(sre_ref, last), get(sim_ref, last), run, apr_ref, api_ref,
                             int(math.log2(SUBLANES)), runs)
    c_re = jnp.where(run >= 1, pltpu.roll(e_re, 1, axis=0), 0.0)
    c_im = jnp.where(run >= 1, pltpu.roll(e_im, 1, axis=0), 0.0)
    put(pre_ref, 0, c_re)
    put(pim_ref, 0, c_im)
    for j in range(last):
        l_re, l_im = get(sre_ref, j), get(sim_ref, j)
        br, bi = pwr_ref[j:j + 1, :], pwi_ref[j:j + 1, :]
        put(pre_ref, j + 1, l_re + br * c_re - bi * c_im)
        put(pim_ref, j + 1, l_im + br * c_im + bi * c_re)
    prev = jnp.concatenate([pre_ref[lt] for lt in range(lane_tiles)]
                           + [pim_ref[lt] for lt in range(lane_tiles)], axis=1).astype(BF16)
    y = y_intra + _dot(prev, mq_ref[...])
    for t in range(S5_FOLD):
        y_ref[pl.ds(t, folds, stride=S5_FOLD), :] = y[:, t * LANES:(t + 1) * LANES] + d_ref[...] * us[t]


def _s5_scan(h, bsz, seq, s5w, layer):
    k_y, k_v, k_q, d_t, ap_re, ap_im, pw_re, pw_im = s5w
    nsteps = ap_re.shape[2]
    folds = seq // S5_FOLD
    tile = lambda j, b: (layer, j, 0, 0)
    compact = pl.BlockSpec((None, None, S5_ROWS, LANES), tile)
    steps = pl.BlockSpec((None, None, nsteps, S5_TILE_STATE), tile)
    powers = pl.BlockSpec((None, None, SUBLANES, S5_TILE_STATE), tile)
    return pl.pallas_call(
        _s5_kernel,
        grid=(S5_TILES, bsz),
        in_specs=[pl.BlockSpec((seq, LANES), lambda j, b: (b, j)),
                  compact, compact, compact,
                  pl.BlockSpec((None, None, 1, LANES), tile),
                  steps, steps, powers, powers],
        out_specs=pl.BlockSpec((seq, LANES), lambda j, b: (b, j)),
        out_shape=jax.ShapeDtypeStruct((bsz * seq, BRANCH_WIDTH), F32),
        scratch_shapes=[pltpu.VMEM((S5_ROWS, S5_ROWS), BF16)] * 3
        + [pltpu.VMEM((S5_TILE_STATE // LANES, folds, LANES), F32)] * 4,
        compiler_params=_cparams(("parallel", "arbitrary")),
        name="s5_scan",
    )(h, k_y, k_v, k_q, d_t, ap_re, ap_im, pw_re, pw_im)


def _s5_weights(lam_re, lam_im, log_dt, b_re, b_im, c_re, c_im, d_skip, folds_per_seq):
    t_fold, g_n, p_n, c_n, tg = S5_FOLD, S5_GROUPS, S5_STATE, S5_GROUP, S5_TILE_GROUPS
    tiles = g_n // tg
    lr = jnp.minimum(lam_re.astype(F32), -S5_MIN_DECAY)
    li = lam_im.astype(F32)
    dt = jnp.exp(log_dt.astype(F32))[:, None]
    mag = jnp.exp(lr * dt)
    abar_re, abar_im = mag * jnp.cos(li * dt), mag * jnp.sin(li * dt)
    den = lr * lr + li * li
    fac_re = ((abar_re - 1.0) * lr + abar_im * li) / den
    fac_im = (abar_im * lr - (abar_re - 1.0) * li) / den
    br, bi = b_re.astype(F32), b_im.astype(F32)
    bbar_re = fac_re[..., None] * br - fac_im[..., None] * bi
    bbar_im = fac_re[..., None] * bi + fac_im[..., None] * br

    def apow(n):
        nn = jnp.asarray(n, F32)[:, None, None]
        m = jnp.exp(nn * (lr * dt)[None])
        return m * jnp.cos(nn * (li * dt)[None]), m * jnp.sin(nn * (li * dt)[None])

    pr, pi = apow(jnp.arange(t_fold + 1))
    cr, ci = c_re.astype(F32), c_im.astype(F32)
    ca_re = cr[None] * pr[:, :, None, :] - ci[None] * pi[:, :, None, :]
    ca_im = cr[None] * pi[:, :, None, :] + ci[None] * pr[:, :, None, :]
    k_tau = jnp.sum(ca_re[:t_fold, :, :, :, None] * bbar_re[None, :, None, :, :]
                    - ca_im[:t_fold, :, :, :, None] * bbar_im[None, :, None, :, :], axis=3)
    jj = jnp.arange(t_fold)[:, None]
    tt = jnp.arange(t_fold)[None, :]
    lag = jnp.clip(tt - jj, 0, t_fold - 1)
    toe = jnp.where((tt >= jj)[:, :, None, None, None], k_tau[lag], 0.0)
    k_y = (toe.reshape(t_fold, t_fold, tiles, tg, c_n, c_n).transpose(2, 0, 3, 5, 1, 4)
           .reshape(tiles, S5_ROWS, LANES))
    rev = t_fold - 1 - jnp.arange(t_fold)
    prj, pij = pr[rev], pi[rev]
    pv = jnp.stack([prj[..., None] * bbar_re[None] - pij[..., None] * bbar_im[None],
                    prj[..., None] * bbar_im[None] + pij[..., None] * bbar_re[None]])
    k_v = (pv.reshape(2, t_fold, tiles, tg, p_n, c_n).transpose(2, 1, 3, 5, 0, 4)
           .reshape(tiles, S5_ROWS, LANES))
    qa = jnp.stack([ca_re[1:], -ca_im[1:]])
    k_q = (qa.reshape(2, t_fold, tiles, tg, c_n, p_n).transpose(2, 0, 3, 5, 1, 4)
           .reshape(tiles, S5_ROWS, LANES))
    d_t = d_skip.astype(F32).reshape(tiles, 1, LANES)
    nsteps = max(int(math.log2(folds_per_seq)), 1)
    sr, si = apow(t_fold * (2 ** jnp.arange(nsteps)))
    ap_re = sr.reshape(nsteps, tiles, tg * p_n).transpose(1, 0, 2)
    ap_im = si.reshape(nsteps, tiles, tg * p_n).transpose(1, 0, 2)
    wr, wi = apow(t_fold * (1 + jnp.arange(SUBLANES)))
    pw_re = wr.reshape(SUBLANES, tiles, tg * p_n).transpose(1, 0, 2)
    pw_im = wi.reshape(SUBLANES, tiles, tg * p_n).transpose(1, 0, 2)
    return k_y.astype(BF16), k_v.astype(BF16), k_q.astype(BF16), d_t, ap_re, ap_im, pw_re, pw_im


def _glu_kernel(y_ref, w_ref, b_ref, o_ref):
    y = y_ref[...]
    z = 0.5 * y * (1.0 + jnp.tanh(math.sqrt(2.0 / math.pi) * (y + 0.044715 * (y * y * y))))
    a = _dot(z.astype(BF16), w_ref[...]) + b_ref[...]
    o_ref[...] = (z * _sigmoid(a)).astype(o_ref.dtype)


def _glu(y, w, b, layer):
    n, width = y.shape
    tm = min(ROW_TILE, n)
    return pl.pallas_call(
        _glu_kernel,
        grid=(n // tm,),
        in_specs=[pl.BlockSpec((tm, width), lambda i: (i, 0)),
                  pl.BlockSpec((None, width, width), lambda i: (layer, 0, 0)),
                  pl.BlockSpec((None, 1, width), lambda i: (layer, 0, 0))],
        out_specs=pl.BlockSpec((tm, width), lambda i: (i, 0)),
        out_shape=jax.ShapeDtypeStruct((n, width), BF16),
        compiler_params=_cparams(("parallel",)),
        name="s5_glu",
    )(y, w, b)


ATT_STAGES = 9


def _gated_chunks(q, k, v, g, st_ref, between_stages):
    r, kw = q.shape
    c = min(ATT_CHUNK, r)
    chunks = [slice(i * c, (i + 1) * c) for i in range(r // c)]
    heads = kw // LANES
    dv = v.shape[1] // heads
    row = lax.broadcasted_iota(jnp.int32, (r, kw), 0)
    ri = lax.broadcasted_iota(jnp.int32, (c, c), 0)
    ci = lax.broadcasted_iota(jnp.int32, (c, c), 1)
    xr = jnp.where(ri > ci, ri ^ ci, 0)
    ksl = [slice(hd * LANES, (hd + 1) * LANES) for hd in range(heads)]
    vsl = [slice(hd * dv, (hd + 1) * dv) for hd in range(heads)]
    pairs = [(rows, s) for rows in chunks for s in ksl]
    tril = jnp.where(ri >= ci, 1.0, 0.0).astype(BF16)
    g_hi = g.astype(BF16)
    g_rest = g - g_hi.astype(F32)
    g_mid = g_rest.astype(BF16)
    g_lo = (g_rest - g_mid.astype(F32)).astype(BF16)
    gc = jnp.concatenate([_dot(tril, g_hi[rows]) + _dot(tril, g_mid[rows]) + _dot(tril, g_lo[rows])
                          for rows in chunks], axis=0)

    def block_row(x, blk, pos):
        x3 = x.reshape(r // blk, blk, kw)
        return jnp.broadcast_to(x3[:, pos:pos + 1, :], x3.shape).reshape(r, kw)

    between_stages()
    qb, kb = q.astype(BF16), k.astype(BF16)
    a = [jnp.where(ri == ci, _dot_nt(qb[rows, s], kb[rows, s]), 0.0) for rows, s in pairs]
    end = gc
    half = 1
    while half < c:
        between_stages()
        if half < SUBLANES:
            first = (row & half) == 0
            ref = jnp.where(first, end, pltpu.roll(end, half, axis=0))
            if 2 * half < SUBLANES:
                end = jnp.where(first, pltpu.roll(end, r - half, axis=0), end)
        else:
            ref = block_row(gc, 2 * half, half - 1)
        e = jnp.exp2(jnp.abs(gc - ref) * -LOG2E).astype(BF16)
        eq, ek = e * qb, e * kb
        level = xr >= half
        a = [jnp.where(level, _dot_nt(eq[rows, s], ek[rows, s]), a_p) for (rows, s), a_p in zip(pairs, a)]
        half *= 2
    between_stages()
    last = block_row(gc, c, c - 1)
    vb = v.astype(BF16)
    qg = (q * jnp.exp(gc)).astype(BF16)
    kd = (k * jnp.exp(last - gc)).astype(BF16)
    outs = []
    for hd in range(heads):
        st = st_ref[hd]
        o_hd = []
        for ic, rows in enumerate(chunks):
            o_hd.append(_dot(a[ic * heads + hd].astype(BF16), vb[rows, vsl[hd]])
                        + _dot_nt(qg[rows, ksl[hd]], st.astype(BF16)))
            keep = jnp.exp(last[rows.start:rows.start + 1, ksl[hd]])
            st = st * keep + _dot_tn(vb[rows, vsl[hd]], kd[rows, ksl[hd]])
        st_ref[hd] = st
        outs.append(jnp.concatenate(o_hd, axis=0))
    return outs


def _chunk_rows(step_rows):
    c = min(ATT_CHUNK, step_rows)
    return [slice(i * c, (i + 1) * c) for i in range(step_rows // c)]


def _begin_mixer_step(x0_ref, xn_ref, w_ref, cur_ref, nxt_ref, st_ref, steps_per_seq):
    step = pl.program_id(0)

    @pl.when(step == 0)
    def _():
        cur_ref[...] = _dot(x0_ref[...], w_ref[...])

    @pl.when(step % steps_per_seq == 0)
    def _():
        st_ref[...] = jnp.zeros_like(st_ref)

    rows, cols = cur_ref.shape
    pieces = [(slice(r0, r0 + PROJ_PIECE_ROWS), slice(c0, min(c0 + PROJ_PIECE, cols)))
              for c0 in range(0, cols, PROJ_PIECE) for r0 in range(0, rows, PROJ_PIECE_ROWS)]
    slots = ATT_STAGES * len(_chunk_rows(rows))
    calls = [0]

    def between_stages():
        slot = calls[0]
        calls[0] += 1
        for p, (rs, cs) in enumerate(pieces):
            if (p * slots) // len(pieces) == slot:
                nxt_ref[rs, cs] = _dot(xn_ref[rs, :], w_ref[:, cs])

    return between_stages


def _gla_kernel(x0_ref, xn_ref, w_ref, wg_ref, bg_ref, nw_ref, u_ref, o_ref, cur_ref, nxt_ref, st_ref, *,
                steps_per_seq):
    between_stages = _begin_mixer_step(x0_ref, xn_ref, w_ref, cur_ref, nxt_ref, st_ref, steps_per_seq)
    u_ref[...] = cur_ref[:, GLA_U:GLA_U + BRANCH_WIDTH]
    for rows in _chunk_rows(cur_ref.shape[0]):
        glow = cur_ref[rows, GLA_GLOW:GLA_GLOW + LANES].astype(BF16)
        z = _dot(glow, wg_ref[...]) + bg_ref[...]
        g = (jnp.minimum(z, 0.0) - jnp.log1p(jnp.exp(-jnp.abs(z)))) / GLA_GATE_TAU
        outs = _gated_chunks(cur_ref[rows, GLA_Q:GLA_Q + GLA_KEY] * (GLA_DK ** -0.5),
                             cur_ref[rows, GLA_K:GLA_K + GLA_KEY],
                             cur_ref[rows, GLA_V:GLA_V + BRANCH_WIDTH], g, st_ref, between_stages)
        for hd, o in enumerate(outs):
            vs = slice(hd * GLA_DV, (hd + 1) * GLA_DV)
            o = o * lax.rsqrt(jnp.mean(o * o, axis=-1, keepdims=True) + RMS_EPS) * nw_ref[...]
            gate = cur_ref[rows, GLA_GATE + hd * GLA_DV:GLA_GATE + (hd + 1) * GLA_DV]
            o_ref[rows, vs] = (o * (gate * _sigmoid(gate))).astype(o_ref.dtype)
    cur_ref[...] = nxt_ref[...]


def _mixer_specs(xb, w, seq, layer):
    n, d = xb.shape
    rows = min(ATT_STEP_ROWS, seq)
    steps = n // rows
    x0 = pl.BlockSpec((rows, d), lambda i: (0, 0))
    xn = pl.BlockSpec((rows, d), lambda i: (jnp.minimum(i + 1, steps - 1), 0))
    wspec = pl.BlockSpec((None,) + w.shape[1:], lambda i: (layer, 0, 0), pipeline_mode=pl.Buffered(1))
    out = pl.BlockSpec((rows, BRANCH_WIDTH), lambda i: (i, 0))
    proj = pltpu.VMEM((rows, w.shape[2]), F32)
    return rows, steps, x0, xn, wspec, out, proj


def _gla(xb, w, wg, bg, nw, seq, layer):
    rows, steps, x0, xn, wspec, out, proj = _mixer_specs(xb, w, seq, layer)
    vec = lambda i: (layer, 0, 0)
    return pl.pallas_call(
        functools.partial(_gla_kernel, steps_per_seq=seq // rows),
        grid=(steps,),
        in_specs=[x0, xn, wspec,
                  pl.BlockSpec((None, LANES, GLA_KEY), vec),
                  pl.BlockSpec((None, 1, GLA_KEY), vec),
                  pl.BlockSpec((None, 1, GLA_DV), vec)],
        out_specs=[out, out],
        out_shape=[jax.ShapeDtypeStruct((xb.shape[0], BRANCH_WIDTH), F32),
                   jax.ShapeDtypeStruct((xb.shape[0], BRANCH_WIDTH), BF16)],
        scratch_shapes=[proj, proj, pltpu.VMEM((GLA_HEADS, GLA_DV, GLA_DK), F32)],
        compiler_params=_cparams(("arbitrary",)),
        name="gla",
    )(xb, xb, w, wg, bg, nw)


def _hgrn_kernel(x0_ref, xn_ref, w_ref, lb_ref, nw_ref, o_ref, cur_ref, nxt_ref, st_ref, *, steps_per_seq):
    between_stages = _begin_mixer_step(x0_ref, xn_ref, w_ref, cur_ref, nxt_ref, st_ref, steps_per_seq)
    lb = lb_ref[...]
    for rows in _chunk_rows(cur_ref.shape[0]):
        f = lb + (1.0 - lb) * _sigmoid_rel(cur_ref[rows, HGRN_F:HGRN_F + HGRN_KEY])
        qr = cur_ref[rows, HGRN_Q:HGRN_Q + HGRN_KEY]
        outs = _gated_chunks(qr * _sigmoid(qr), 1.0 - f, cur_ref[rows, HGRN_I:HGRN_I + BRANCH_WIDTH],
                             jnp.log(f), st_ref, between_stages)
        for hd, o in enumerate(outs):
            vs = slice(hd * HGRN_DV, (hd + 1) * HGRN_DV)
            o = o * _sigmoid(cur_ref[rows, HGRN_GATE + hd * HGRN_DV:HGRN_GATE + (hd + 1) * HGRN_DV])
            o = o * lax.rsqrt(jnp.mean(o * o, axis=-1, keepdims=True) + RMS_EPS) * nw_ref[...]
            o_ref[rows, vs] = o.astype(o_ref.dtype)
    cur_ref[...] = nxt_ref[...]


def _hgrn(xb, w, lb, nw, seq, layer):
    rows, steps, x0, xn, wspec, out, proj = _mixer_specs(xb, w, seq, layer)
    vec = lambda i: (layer, 0, 0)
    return pl.pallas_call(
        functools.partial(_hgrn_kernel, steps_per_seq=seq // rows),
        grid=(steps,),
        in_specs=[x0, xn, wspec,
                  pl.BlockSpec((None, 1, HGRN_KEY), vec),
                  pl.BlockSpec((None, 1, HGRN_DV), vec)],
        out_specs=out,
        out_shape=jax.ShapeDtypeStruct((xb.shape[0], BRANCH_WIDTH), BF16),
        scratch_shapes=[proj, proj, pltpu.VMEM((HGRN_HEADS, HGRN_DV, HGRN_EXPAND), F32)],
        compiler_params=_cparams(("arbitrary",)),
        name="hgrn",
    )(xb, xb, w, lb, nw)


def _zero_first_step(o_ref):
    @pl.when(pl.program_id(1) == 0)
    def _():
        o_ref[...] = jnp.zeros_like(o_ref)


def _norm_last_step(x_ref, lg_ref, lb_ref, o_ref, ob_ref):
    @pl.when(pl.program_id(1) == pl.num_programs(1) - 1)
    def _():
        xn = _layer_norm(DN_ALPHA * x_ref[...] + o_ref[...], lg_ref[...], lb_ref[...])
        o_ref[...] = xn
        ob_ref[...] = xn.astype(BF16)


def _merge_kernel(ya_ref, yb_ref, yc_ref, ga_ref, gb_ref, gc_ref, wup_ref, o_ref):
    merged = (_sigmoid(ga_ref[...].astype(F32)) * _dot(ya_ref[...], wup_ref[0])
              + _sigmoid(gb_ref[...].astype(F32)) * _dot(yb_ref[...], wup_ref[1])
              + _sigmoid(gc_ref[...].astype(F32)) * _dot(yc_ref[...], wup_ref[2]))
    o_ref[...] = merged.astype(o_ref.dtype)


def _merge(ya, yb, yc, gates, wup, layer):
    n = ya.shape[0]
    tm = min(MERGE_TM, n)
    tn = MERGE_TILE

    def gate(branch):
        off = branch * D_MODEL // tn
        return pl.BlockSpec((tm, tn), lambda i, j: (i, off + j))

    ybs = pl.BlockSpec((tm, BRANCH_WIDTH), lambda i, j: (i, 0))
    return pl.pallas_call(
        _merge_kernel,
        grid=(n // tm, D_MODEL // tn),
        in_specs=[ybs, ybs, ybs, gate(0), gate(1), gate(2),
                  pl.BlockSpec((None, N_BRANCH, BRANCH_WIDTH, tn), lambda i, j: (layer, 0, 0, j))],
        out_specs=pl.BlockSpec((tm, tn), lambda i, j: (i, j)),
        out_shape=jax.ShapeDtypeStruct((n, D_MODEL), BF16),
        compiler_params=_cparams(("parallel", "arbitrary")),
        name="merge",
    )(ya, yb, yc, gates, gates, gates, wup)


def _mixer_out_kernel(m_ref, wout_ref, x_ref, lg_ref, lb_ref, o_ref, ob_ref):
    xn = _layer_norm(DN_ALPHA * x_ref[...] + _dot(m_ref[...], wout_ref[...]), lg_ref[...], lb_ref[...])
    o_ref[...] = xn
    ob_ref[...] = xn.astype(BF16)


def _mixer_out(merged, wout, x, ln_g, ln_b, layer):
    n = x.shape[0]
    tm = min(ROW_TILE, n)
    row = pl.BlockSpec((tm, D_MODEL), lambda i: (i, 0))
    vec = pl.BlockSpec((None, 1, D_MODEL), lambda i: (layer, 0, 0))
    return pl.pallas_call(
        _mixer_out_kernel,
        grid=(n // tm,),
        in_specs=[row, pl.BlockSpec((None, D_MODEL, D_MODEL), lambda i: (layer, 0, 0)), row, vec, vec],
        out_specs=[row, row],
        out_shape=[jax.ShapeDtypeStruct((n, D_MODEL), F32), jax.ShapeDtypeStruct((n, D_MODEL), BF16)],
        compiler_params=_cparams(("parallel",)),
        name="mixer_out",
    )(merged, wout, x, ln_g, ln_b)


def _mlp_kernel(xb_ref, w1_ref, w2_ref, x_ref, lg_ref, lb_ref, o_ref, ob_ref):
    _zero_first_step(o_ref)
    hid = jnp.maximum(_dot(xb_ref[...], w1_ref[...]), 0.0)
    o_ref[...] += _dot((hid * hid).astype(BF16), w2_ref[...])
    _norm_last_step(x_ref, lg_ref, lb_ref, o_ref, ob_ref)


def _mlp(xb, w1, w2, x, ln_g, ln_b, layer):
    n = x.shape[0]
    tm = min(ROW_TILE, n)
    th = HID_TILE
    row = pl.BlockSpec((tm, D_MODEL), lambda i, k: (i, 0))
    vec = pl.BlockSpec((None, 1, D_MODEL), lambda i, k: (layer, 0, 0))
    return pl.pallas_call(
        _mlp_kernel,
        grid=(n // tm, MLP_HIDDEN // th),
        in_specs=[row,
                  pl.BlockSpec((None, D_MODEL, th), lambda i, k: (layer, 0, k)),
                  pl.BlockSpec((None, th, D_MODEL), lambda i, k: (layer, k, 0)),
                  row, vec, vec],
        out_specs=[row, row],
        out_shape=[jax.ShapeDtypeStruct((n, D_MODEL), F32), jax.ShapeDtypeStruct((n, D_MODEL), BF16)],
        compiler_params=_cparams(("parallel", "arbitrary")),
        name="mlp",
    )(xb, w1, w2, x, ln_g, ln_b)


def _split_w_in(w):
    pad = jnp.zeros(w.shape[:-1] + (LANES - GLA_GATE_RANK,), w.dtype)
    w_gla = jnp.concatenate([w[..., :W_IN_GLOW], w[..., W_IN_GATEB:W_IN_HGRN], w[..., W_IN_GLOW:W_IN_GATEB], pad],
                            axis=-1).astype(BF16)
    return w_gla, w[..., W_IN_HGRN:W_IN_MERGE].astype(BF16), w[..., W_IN_MERGE:].astype(BF16)


def _row(v):
    return v.astype(F32)[:, None, :]


def kernel(x, w_in, s5_lam_re, s5_lam_im, s5_log_dt, s5_b_re, s5_b_im, s5_c_re, s5_c_im, s5_d, s5_w_glu, s5_b_glu, gla_w_gate, gla_b_gate, gla_norm_w, hgrn_lb_logits, hgrn_norm_w, w_up, w_out, ln1_g, ln1_b, ln2_g, ln2_b, w_mlp_in, w_mlp_out):
    bsz, seq, d = x.shape
    n = bsz * seq
    depth = w_in.shape[0]
    p = jax.nn.softmax(hgrn_lb_logits.astype(F32), axis=0)
    lb = _row(jnp.cumsum(p, axis=0) - p[0])
    w_gla_b, w_hgrn_b, w_gates_b = _split_w_in(w_in)
    s5w = jax.vmap(functools.partial(_s5_weights, folds_per_seq=seq // S5_FOLD))(
        s5_lam_re, s5_lam_im, s5_log_dt, s5_b_re, s5_b_im, s5_c_re, s5_c_im, s5_d)
    w_glu_b = s5_w_glu.astype(BF16)
    b_glu = _row(s5_b_glu)
    wg = jnp.concatenate([gla_w_gate, jnp.zeros((depth, LANES - GLA_GATE_RANK, GLA_KEY), gla_w_gate.dtype)],
                         axis=1).astype(BF16)
    bg, gla_nw, hgrn_nw = _row(gla_b_gate), _row(gla_norm_w), _row(hgrn_norm_w)
    w_up_b, w_out_b = w_up.astype(BF16), w_out.astype(BF16)
    w1_b, w2_b = w_mlp_in.astype(BF16), w_mlp_out.astype(BF16)
    g1, b1, g2, b2 = _row(ln1_g), _row(ln1_b), _row(ln2_g), _row(ln2_b)

    xf = x.reshape(n, d).astype(F32)
    xb = xf.astype(BF16)
    for l in range(depth):
        gates = _proj(xb, w_gates_b, l, 0, MERGE_WIDTH, BF16)
        u, yb = _gla(xb, w_gla_b, wg, bg, gla_nw, seq, l)
        ya = _glu(_s5_scan(u, bsz, seq, s5w, l), w_glu_b, b_glu, l)
        yc = _hgrn(xb, w_hgrn_b, lb, hgrn_nw, seq, l)
        xf, xb = _mixer_out(_merge(ya, yb, yc, gates, w_up_b, l), w_out_b, xf, g1, b1, l)
        xf, xb = _mlp(xb, w1_b, w2_b, xf, g2, b2, l)
    return xf.reshape(bsz, seq, d).astype(x.dtype)
```

```python
import functools
import math

import jax
import jax.numpy as jnp
from jax import lax
from jax.experimental import pallas as pl
from jax.experimental.pallas import tpu as pltpu

F32 = jnp.float32
BF16 = jnp.bfloat16

D_MODEL = 2048
DEPTH = 4
N_BRANCH = 3
BRANCH_WIDTH = D_MODEL // 2
S5_GROUP = 16
S5_GROUPS = BRANCH_WIDTH // S5_GROUP
S5_STATE = 64
S5_MIN_DECAY = 1e-4
GLA_HEADS = 4
GLA_DV = BRANCH_WIDTH // GLA_HEADS
GLA_DK = GLA_DV // 2
GLA_KEY = GLA_HEADS * GLA_DK
GLA_GATE_RANK = 16
GLA_GATE_TAU = 16.0
HGRN_EXPAND = 128
HGRN_HEADS = BRANCH_WIDTH // HGRN_EXPAND
HGRN_DV = BRANCH_WIDTH // HGRN_HEADS
HGRN_KEY = HGRN_HEADS * HGRN_EXPAND
MLP_HIDDEN = 4 * D_MODEL
DN_ALPHA = (2 * DEPTH) ** 0.25
LN_EPS = 1e-5
RMS_EPS = 1e-6
LOG2E = 1.4426950408889634

LANES = 128
SUBLANES = 8
VMEM_LIMIT_BYTES = 56 * 1024 * 1024

W_IN_GLOW = BRANCH_WIDTH + 2 * GLA_KEY + BRANCH_WIDTH
W_IN_GATEB = W_IN_GLOW + GLA_GATE_RANK
W_IN_HGRN = W_IN_GATEB + BRANCH_WIDTH
W_IN_MERGE = W_IN_HGRN + 2 * HGRN_KEY + 2 * BRANCH_WIDTH
GLA_U = 0
GLA_Q = GLA_U + BRANCH_WIDTH
GLA_K = GLA_Q + GLA_KEY
GLA_V = GLA_K + GLA_KEY
GLA_GATE = GLA_V + BRANCH_WIDTH
GLA_GLOW = GLA_GATE + BRANCH_WIDTH
GLA_COLS = GLA_GLOW + LANES
HGRN_Q = 0
HGRN_F = HGRN_Q + HGRN_KEY
HGRN_I = HGRN_F + HGRN_KEY
HGRN_GATE = HGRN_I + BRANCH_WIDTH
HGRN_COLS = HGRN_GATE + BRANCH_WIDTH
MERGE_WIDTH = N_BRANCH * D_MODEL

S5_FOLD = SUBLANES
S5_TILE_GROUPS = LANES // S5_GROUP
S5_TILES = S5_GROUPS // S5_TILE_GROUPS
S5_TILE_STATE = S5_TILE_GROUPS * S5_STATE
S5_ROWS = S5_FOLD * LANES
ATT_CHUNK = 128
ATT_STEP_ROWS = 512
PROJ_PIECE = 256
PROJ_PIECE_ROWS = 256
ROW_TILE = 512
PROJ_TM = 2048
PROJ_TN = 1024
MERGE_TM = 1024
MERGE_TILE = 1024
HID_TILE = 1024


def _cparams(sem):
    return pltpu.CompilerParams(dimension_semantics=sem, vmem_limit_bytes=VMEM_LIMIT_BYTES)


def _dot(a, b):
    return jnp.dot(a, b, preferred_element_type=F32)


def _dot_nt(a, b):
    return lax.dot_general(a, b, (((1,), (1,)), ((), ())), preferred_element_type=F32)


def _dot_tn(a, b):
    return lax.dot_general(a, b, (((0,), (0,)), ((), ())), preferred_element_type=F32)


def _sigmoid(x):
    return 0.5 * jnp.tanh(0.5 * x) + 0.5


def _sigmoid_rel(x):
    return 1.0 / (1.0 + jnp.exp2(x * -LOG2E))


def _layer_norm(v, g, b):
    mu = jnp.mean(v, axis=-1, keepdims=True)
    xc = v - mu
    var = jnp.mean(xc * xc, axis=-1, keepdims=True)
    return xc * lax.rsqrt(var + LN_EPS) * g + b


def _proj_kernel(x_ref, w_ref, o_ref):
    o_ref[...] = _dot(x_ref[...], w_ref[...]).astype(o_ref.dtype)


def _proj(xb, w, layer, col0, width, out_dtype):
    n, k = xb.shape
    tm = min(PROJ_TM, n)
    return pl.pallas_call(
        _proj_kernel,
        grid=(n // tm, width // PROJ_TN),
        in_specs=[pl.BlockSpec((tm, k), lambda i, j: (i, 0)),
                  pl.BlockSpec((None, k, PROJ_TN), lambda i, j: (layer, 0, col0 // PROJ_TN + j))],
        out_specs=pl.BlockSpec((tm, PROJ_TN), lambda i, j: (i, j)),
        out_shape=jax.ShapeDtypeStruct((n, width), out_dtype),
        compiler_params=_cparams(("parallel", "arbitrary")),
        name="proj",
    )(xb, w)


def _s5_expand(ky_ref, kv_ref, kq_ref, my_ref, mv_ref, mq_ref):
    kk = lax.broadcasted_iota(jnp.int32, (LANES, S5_ROWS), 0)
    cc = lax.broadcasted_iota(jnp.int32, (LANES, S5_ROWS), 1)
    e_tc = jnp.where(((kk >> 4) == (cc >> 7)) & ((kk & 15) == (cc & 15)), 1.0, 0.0).astype(BF16)
    e_rp = jnp.where(((kk >> 6) == (cc >> 9)) & ((kk & 63) == (cc & 63)), 1.0, 0.0).astype(BF16)
    rr = lax.broadcasted_iota(jnp.int32, (S5_ROWS, S5_ROWS), 0)
    cc = lax.broadcasted_iota(jnp.int32, (S5_ROWS, S5_ROWS), 1)
    g_jgd, g_rgp = (rr >> 4) & 7, (rr >> 6) & 7
    h_thc, h_rhp = (cc >> 4) & 7, (cc >> 6) & 7
    my_ref[...] = jnp.where(g_jgd == h_thc, _dot(ky_ref[...], e_tc), 0.0).astype(BF16)
    mv_ref[...] = jnp.where(g_jgd == h_rhp, _dot(kv_ref[...], e_rp), 0.0).astype(BF16)
    mq_ref[...] = jnp.where(g_rgp == h_thc, _dot(kq_ref[...], e_tc), 0.0).astype(BF16)


def _scan_steps(s_re, s_im, pos, apr_ref, api_ref, first_step, length):
    step, dist = first_step, 1
    while dist < length:
        ok = pos >= dist
        sh_re = jnp.where(ok, pltpu.roll(s_re, dist, axis=0), 0.0)
        sh_im = jnp.where(ok, pltpu.roll(s_im, dist, axis=0), 0.0)
        ar = apr_ref[step:step + 1, :]
        ai = api_ref[step:step + 1, :]
        s_re, s_im = s_re + ar * sh_re - ai * sh_im, s_im + ar * sh_im + ai * sh_re
        step += 1
        dist *= 2
    return s_re, s_im


def _s5_kernel(u_ref, ky_ref, kv_ref, kq_ref, d_ref, apr_ref, api_ref, pwr_ref, pwi_ref, y_ref,
               my_ref, mv_ref, mq_ref, sre_ref, sim_ref, pre_ref, pim_ref):
    @pl.when(pl.program_id(1) == 0)
    def _():
        _s5_expand(ky_ref, kv_ref, kq_ref, my_ref, mv_ref, mq_ref)

    folds = u_ref.shape[0] // S5_FOLD
    runs = folds // SUBLANES
    us = [u_ref[pl.ds(t, folds, stride=S5_FOLD), :] for t in range(S5_FOLD)]
    ucat = jnp.concatenate([u.astype(BF16) for u in us], axis=1)
    r = _dot(ucat, mv_ref[...])
    y_intra = _dot(ucat, my_ref[...])
    fold = lax.broadcasted_iota(jnp.int32, (folds, S5_TILE_STATE), 0)
    s_re, s_im = _scan_steps(r[:, :S5_TILE_STATE], r[:, S5_TILE_STATE:], fold & (SUBLANES - 1),
                             apr_ref, api_ref, 0, SUBLANES)
    lane_tiles = S5_TILE_STATE // LANES

    def put(ref, j, val):
        for lt in range(lane_tiles):
            ref[lt, pl.ds(j, runs, stride=SUBLANES), :] = val[:, lt * LANES:(lt + 1) * LANES]

    def get(ref, j):
        return jnp.concatenate([ref[lt, pl.ds(j, runs, stride=SUBLANES), :] for lt in range(lane_tiles)], axis=1)

    for lt in range(lane_tiles):
        sre_ref[lt] = s_re[:, lt * LANES:(lt + 1) * LANES]
        sim_ref[lt] = s_im[:, lt * LANES:(lt + 1) * LANES]
    last = SUBLANES - 1
    run = lax.broadcasted_iota(jnp.int32, (runs, S5_TILE_STATE), 0)
    e_re, e_im = _scan_steps(get(sre_ref, last), get(sim_ref, last), run, apr_ref, api_ref,
                             int(math.log2(SUBLANES)), runs)
    c_re = jnp.where(run >= 1, pltpu.roll(e_re, 1, axis=0), 0.0)
    c_im = jnp.where(run >= 1, pltpu.roll(e_im, 1, axis=0), 0.0)
    put(pre_ref, 0, c_re)
    put(pim_ref, 0, c_im)
    for j in range(last):
        l_re, l_im = get(sre_ref, j), get(sim_ref, j)
        br, bi = pwr_ref[j:j + 1, :], pwi_ref[j:j + 1, :]
        put(pre_ref, j + 1, l_re + br * c_re - bi * c_im)
        put(pim_ref, j + 1, l_im + br * c_im + bi * c_re)
    prev = jnp.concatenate([pre_ref[lt] for lt in range(lane_tiles)]
                           + [pim_ref[lt] for lt in range(lane_tiles)], axis=1).astype(BF16)
    y = y_intra + _dot(prev, mq_ref[...])
    for t in range(S5_FOLD):
        y_ref[pl.ds(t, folds, stride=S5_FOLD), :] = y[:, t * LANES:(t + 1) * LANES] + d_ref[...] * us[t]


def _s5_scan(h, bsz, seq, s5w, layer):
    k_y, k_v, k_q, d_t, ap_re, ap_im, pw_re, pw_im = s5w
    nsteps = ap_re.shape[2]
    folds = seq // S5_FOLD
    tile = lambda j, b: (layer, j, 0, 0)
    compact = pl.BlockSpec((None, None, S5_ROWS, LANES), tile)
    steps = pl.BlockSpec((None, None, nsteps, S5_TILE_STATE), tile)
    powers = pl.BlockSpec((None, None, SUBLANES, S5_TILE_STATE), tile)
    return pl.pallas_call(
        _s5_kernel,
        grid=(S5_TILES, bsz),
        in_specs=[pl.BlockSpec((seq, LANES), lambda j, b: (b, j)),
                  compact, compact, compact,
                  pl.BlockSpec((None, None, 1, LANES), tile),
                  steps, steps, powers, powers],
        out_specs=pl.BlockSpec((seq, LANES), lambda j, b: (b, j)),
        out_shape=jax.ShapeDtypeStruct((bsz * seq, BRANCH_WIDTH), F32),
        scratch_shapes=[pltpu.VMEM((S5_ROWS, S5_ROWS), BF16)] * 3
        + [pltpu.VMEM((S5_TILE_STATE // LANES, folds, LANES), F32)] * 4,
        compiler_params=_cparams(("parallel", "arbitrary")),
        name="s5_scan",
    )(h, k_y, k_v, k_q, d_t, ap_re, ap_im, pw_re, pw_im)


def _s5_weights(lam_re, lam_im, log_dt, b_re, b_im, c_re, c_im, d_skip, folds_per_seq):
    t_fold, g_n, p_n, c_n, tg = S5_FOLD, S5_GROUPS, S5_STATE, S5_GROUP, S5_TILE_GROUPS
    tiles = g_n // tg
    lr = jnp.minimum(lam_re.astype(F32), -S5_MIN_DECAY)
    li = lam_im.astype(F32)
    dt = jnp.exp(log_dt.astype(F32))[:, None]
    mag = jnp.exp(lr * dt)
    abar_re, abar_im = mag * jnp.cos(li * dt), mag * jnp.sin(li * dt)
    den = lr * lr + li * li
    fac_re = ((abar_re - 1.0) * lr + abar_im * li) / den
    fac_im = (abar_im * lr - (abar_re - 1.0) * li) / den
    br, bi = b_re.astype(F32), b_im.astype(F32)
    bbar_re = fac_re[..., None] * br - fac_im[..., None] * bi
    bbar_im = fac_re[..., None] * bi + fac_im[..., None] * br

    def apow(n):
        nn = jnp.asarray(n, F32)[:, None, None]
        m = jnp.exp(nn * (lr * dt)[None])
        return m * jnp.cos(nn * (li * dt)[None]), m * jnp.sin(nn * (li * dt)[None])

    pr, pi = apow(jnp.arange(t_fold + 1))
    cr, ci = c_re.astype(F32), c_im.astype(F32)
    ca_re = cr[None] * pr[:, :, None, :] - ci[None] * pi[:, :, None, :]
    ca_im = cr[None] * pi[:, :, None, :] + ci[None] * pr[:, :, None, :]
    k_tau = jnp.sum(ca_re[:t_fold, :, :, :, None] * bbar_re[None, :, None, :, :]
                    - ca_im[:t_fold, :, :, :, None] * bbar_im[None, :, None, :, :], axis=3)
    jj = jnp.arange(t_fold)[:, None]
    tt = jnp.arange(t_fold)[None, :]
    lag = jnp.clip(tt - jj, 0, t_fold - 1)
    toe = jnp.where((tt >= jj)[:, :, None, None, None], k_tau[lag], 0.0)
    k_y = (toe.reshape(t_fold, t_fold, tiles, tg, c_n, c_n).transpose(2, 0, 3, 5, 1, 4)
           .reshape(tiles, S5_ROWS, LANES))
    rev = t_fold - 1 - jnp.arange(t_fold)
    prj, pij = pr[rev], pi[rev]
    pv = jnp.stack([prj[..., None] * bbar_re[None] - pij[..., None] * bbar_im[None],
                    prj[..., None] * bbar_im[None] + pij[..., None] * bbar_re[None]])
    k_v = (pv.reshape(2, t_fold, tiles, tg, p_n, c_n).transpose(2, 1, 3, 5, 0, 4)
           .reshape(tiles, S5_ROWS, LANES))
    qa = jnp.stack([ca_re[1:], -ca_im[1:]])
    k_q = (qa.reshape(2, t_fold, tiles, tg, c_n, p_n).transpose(2, 0, 3, 5, 1, 4)
           .reshape(tiles, S5_ROWS, LANES))
    d_t = d_skip.astype(F32).reshape(tiles, 1, LANES)
    nsteps = max(int(math.log2(folds_per_seq)), 1)
    sr, si = apow(t_fold * (2 ** jnp.arange(nsteps)))
    ap_re = sr.reshape(nsteps, tiles, tg * p_n).transpose(1, 0, 2)
    ap_im = si.reshape(nsteps, tiles, tg * p_n).transpose(1, 0, 2)
    wr, wi = apow(t_fold * (1 + jnp.arange(SUBLANES)))
    pw_re = wr.reshape(SUBLANES, tiles, tg * p_n).transpose(1, 0, 2)
    pw_im = wi.reshape(SUBLANES, tiles, tg * p_n).transpose(1, 0, 2)
    return k_y.astype(BF16), k_v.astype(BF16), k_q.astype(BF16), d_t, ap_re, ap_im, pw_re, pw_im


def _glu_kernel(y_ref, w_ref, b_ref, o_ref):
    y = y_ref[...]
    z = 0.5 * y * (1.0 + jnp.tanh(math.sqrt(2.0 / math.pi) * (y + 0.044715 * (y * y * y))))
    a = _dot(z.astype(BF16), w_ref[...]) + b_ref[...]
    o_ref[...] = (z * _sigmoid(a)).astype(o_ref.dtype)


def _glu(y, w, b, layer):
    n, width = y.shape
    tm = min(ROW_TILE, n)
    return pl.pallas_call(
        _glu_kernel,
        grid=(n // tm,),
        in_specs=[pl.BlockSpec((tm, width), lambda i: (i, 0)),
                  pl.BlockSpec((None, width, width), lambda i: (layer, 0, 0)),
                  pl.BlockSpec((None, 1, width), lambda i: (layer, 0, 0))],
        out_specs=pl.BlockSpec((tm, width), lambda i: (i, 0)),
        out_shape=jax.ShapeDtypeStruct((n, width), BF16),
        compiler_params=_cparams(("parallel",)),
        name="s5_glu",
    )(y, w, b)


ATT_STAGES = 9


def _gated_chunks(q, k, v, g, st_ref, between_stages):
    r, kw = q.shape
    c = min(ATT_CHUNK, r)
    chunks = [slice(i * c, (i + 1) * c) for i in range(r // c)]
    heads = kw // LANES
    dv = v.shape[1] // heads
    row = lax.broadcasted_iota(jnp.int32, (r, kw), 0)
    ri = lax.broadcasted_iota(jnp.int32, (c, c), 0)
    ci = lax.broadcasted_iota(jnp.int32, (c, c), 1)
    xr = jnp.where(ri > ci, ri ^ ci, 0)
    ksl = [slice(hd * LANES, (hd + 1) * LANES) for hd in range(heads)]
    vsl = [slice(hd * dv, (hd + 1) * dv) for hd in range(heads)]
    pairs = [(rows, s) for rows in chunks for s in ksl]
    tril = jnp.where(ri >= ci, 1.0, 0.0).astype(BF16)
    g_hi = g.astype(BF16)
    g_rest = g - g_hi.astype(F32)
    g_mid = g_rest.astype(BF16)
    g_lo = (g_rest - g_mid.astype(F32)).astype(BF16)
    gc = jnp.concatenate([_dot(tril, g_hi[rows]) + _dot(tril, g_mid[rows]) + _dot(tril, g_lo[rows])
                          for rows in chunks], axis=0)

    def block_row(x, blk, pos):
        x3 = x.reshape(r // blk, blk, kw)
        return jnp.broadcast_to(x3[:, pos:pos + 1, :], x3.shape).reshape(r, kw)

    between_stages()
    qb, kb = q.astype(BF16), k.astype(BF16)
    a = [jnp.where(ri == ci, _dot_nt(qb[rows, s], kb[rows, s]), 0.0) for rows, s in pairs]
    end = gc
    half = 1
    while half < c:
        between_stages()
        if half < SUBLANES:
            first = (row & half) == 0
            ref = jnp.where(first, end, pltpu.roll(end, half, axis=0))
            if 2 * half < SUBLANES:
                end = jnp.where(first, pltpu.roll(end, r - half, axis=0), end)
        else:
            ref = block_row(gc, 2 * half, half - 1)
        e = jnp.exp2(jnp.abs(gc - ref) * -LOG2E).astype(BF16)
        eq, ek = e * qb, e * kb
        level = xr >= half
        a = [jnp.where(level, _dot_nt(eq[rows, s], ek[rows, s]), a_p) for (rows, s), a_p in zip(pairs, a)]
        half *= 2
    between_stages()
    last = block_row(gc, c, c - 1)
    vb = v.astype(BF16)
    qg = (q * jnp.exp(gc)).astype(BF16)
    kd = (k * jnp.exp(last - gc)).astype(BF16)
    outs = []
    for hd in range(heads):
        st = st_ref[hd]
        o_hd = []
        for ic, rows in enumerate(chunks):
            o_hd.append(_dot(a[ic * heads + hd].astype(BF16), vb[rows, vsl[hd]])
                        + _dot_nt(qg[rows, ksl[hd]], st.astype(BF16)))
            keep = jnp.exp(last[rows.start:rows.start + 1, ksl[hd]])
            st = st * keep + _dot_tn(vb[rows, vsl[hd]], kd[rows, ksl[hd]])
        st_ref[hd] = st
        outs.append(jnp.concatenate(o_hd, axis=0))
    return outs


def _chunk_rows(step_rows):
    c = min(ATT_CHUNK, step_rows)
    return [slice(i * c, (i + 1) * c) for i in range(step_rows // c)]


def _begin_mixer_step(x0_ref, xn_ref, w_ref, cur_ref, nxt_ref, st_ref, steps_per_seq):
    step = pl.program_id(0)

    @pl.when(step == 0)
    def _():
        cur_ref[...] = _dot(x0_ref[...], w_ref[...])

    @pl.when(step % steps_per_seq == 0)
    def _():
        st_ref[...] = jnp.zeros_like(st_ref)

    rows, cols = cur_ref.shape
    pieces = [(slice(r0, r0 + PROJ_PIECE_ROWS), slice(c0, min(c0 + PROJ_PIECE, cols)))
              for c0 in range(0, cols, PROJ_PIECE) for r0 in range(0, rows, PROJ_PIECE_ROWS)]
    slots = ATT_STAGES * len(_chunk_rows(rows))
    calls = [0]

    def between_stages():
        slot = calls[0]
        calls[0] += 1
        for p, (rs, cs) in enumerate(pieces):
            if (p * slots) // len(pieces) == slot:
                nxt_ref[rs, cs] = _dot(xn_ref[rs, :], w_ref[:, cs])

    return between_stages


def _gla_kernel(x0_ref, xn_ref, w_ref, wg_ref, bg_ref, nw_ref, u_ref, o_ref, cur_ref, nxt_ref, st_ref, *,
                steps_per_seq):
    between_stages = _begin_mixer_step(x0_ref, xn_ref, w_ref, cur_ref, nxt_ref, st_ref, steps_per_seq)
    u_ref[...] = cur_ref[:, GLA_U:GLA_U + BRANCH_WIDTH]
    for rows in _chunk_rows(cur_ref.shape[0]):
        glow = cur_ref[rows, GLA_GLOW:GLA_GLOW + LANES].astype(BF16)
        z = _dot(glow, wg_ref[...]) + bg_ref[...]
        g = (jnp.minimum(z, 0.0) - jnp.log1p(jnp.exp(-jnp.abs(z)))) / GLA_GATE_TAU
        outs = _gated_chunks(cur_ref[rows, GLA_Q:GLA_Q + GLA_KEY] * (GLA_DK ** -0.5),
                             cur_ref[rows, GLA_K:GLA_K + GLA_KEY],
                             cur_ref[rows, GLA_V:GLA_V + BRANCH_WIDTH], g, st_ref, between_stages)
        for hd, o in enumerate(outs):
            vs = slice(hd * GLA_DV, (hd + 1) * GLA_DV)
            o = o * lax.rsqrt(jnp.mean(o * o, axis=-1, keepdims=True) + RMS_EPS) * nw_ref[...]
            gate = cur_ref[rows, GLA_GATE + hd * GLA_DV:GLA_GATE + (hd + 1) * GLA_DV]
            o_ref[rows, vs] = (o * (gate * _sigmoid(gate))).astype(o_ref.dtype)
    cur_ref[...] = nxt_ref[...]


def _mixer_specs(xb, w, seq, layer):
    n, d = xb.shape
    rows = min(ATT_STEP_ROWS, seq)
    steps = n // rows
    x0 = pl.BlockSpec((rows, d), lambda i: (0, 0))
    xn = pl.BlockSpec((rows, d), lambda i: (jnp.minimum(i + 1, steps - 1), 0))
    wspec = pl.BlockSpec((None,) + w.shape[1:], lambda i: (layer, 0, 0), pipeline_mode=pl.Buffered(1))
    out = pl.BlockSpec((rows, BRANCH_WIDTH), lambda i: (i, 0))
    proj = pltpu.VMEM((rows, w.shape[2]), F32)
    return rows, steps, x0, xn, wspec, out, proj


def _gla(xb, w, wg, bg, nw, seq, layer):
    rows, steps, x0, xn, wspec, out, proj = _mixer_specs(xb, w, seq, layer)
    vec = lambda i: (layer, 0, 0)
    return pl.pallas_call(
        functools.partial(_gla_kernel, steps_per_seq=seq // rows),
        grid=(steps,),
        in_specs=[x0, xn, wspec,
                  pl.BlockSpec((None, LANES, GLA_KEY), vec),
                  pl.BlockSpec((None, 1, GLA_KEY), vec),
                  pl.BlockSpec((None, 1, GLA_DV), vec)],
        out_specs=[out, out],
        out_shape=[jax.ShapeDtypeStruct((xb.shape[0], BRANCH_WIDTH), F32),
                   jax.ShapeDtypeStruct((xb.shape[0], BRANCH_WIDTH), BF16)],
        scratch_shapes=[proj, proj, pltpu.VMEM((GLA_HEADS, GLA_DV, GLA_DK), F32)],
        compiler_params=_cparams(("arbitrary",)),
        name="gla",
    )(xb, xb, w, wg, bg, nw)


def _hgrn_kernel(x0_ref, xn_ref, w_ref, lb_ref, nw_ref, o_ref, cur_ref, nxt_ref, st_ref, *, steps_per_seq):
    between_stages = _begin_mixer_step(x0_ref, xn_ref, w_ref, cur_ref, nxt_ref, st_ref, steps_per_seq)
    lb = lb_ref[...]
    for rows in _chunk_rows(cur_ref.shape[0]):
        f = lb + (1.0 - lb) * _sigmoid_rel(cur_ref[rows, HGRN_F:HGRN_F + HGRN_KEY])
        qr = cur_ref[rows, HGRN_Q:HGRN_Q + HGRN_KEY]
        outs = _gated_chunks(qr * _sigmoid(qr), 1.0 - f, cur_ref[rows, HGRN_I:HGRN_I + BRANCH_WIDTH],
                             jnp.log(f), st_ref, between_stages)
        for hd, o in enumerate(outs):
            vs = slice(hd * HGRN_DV, (hd + 1) * HGRN_DV)
            o = o * _sigmoid(cur_ref[rows, HGRN_GATE + hd * HGRN_DV:HGRN_GATE + (hd + 1) * HGRN_DV])
            o = o * lax.rsqrt(jnp.mean(o * o, axis=-1, keepdims=True) + RMS_EPS) * nw_ref[...]
            o_ref[rows, vs] = o.astype(o_ref.dtype)
    cur_ref[...] = nxt_ref[...]


def _hgrn(xb, w, lb, nw, seq, layer):
    rows, steps, x0, xn, wspec, out, proj = _mixer_specs(xb, w, seq, layer)
    vec = lambda i: (layer, 0, 0)
    return pl.pallas_call(
        functools.partial(_hgrn_kernel, steps_per_seq=seq // rows),
        grid=(steps,),
        in_specs=[x0, xn, wspec,
                  pl.BlockSpec((None, 1, HGRN_KEY), vec),
                  pl.BlockSpec((None, 1, HGRN_DV), vec)],
        out_specs=out,
        out_shape=jax.ShapeDtypeStruct((xb.shape[0], BRANCH_WIDTH), BF16),
        scratch_shapes=[proj, proj, pltpu.VMEM((HGRN_HEADS, HGRN_DV, HGRN_EXPAND), F32)],
        compiler_params=_cparams(("arbitrary",)),
        name="hgrn",
    )(xb, xb, w, lb, nw)


def _zero_first_step(o_ref):
    @pl.when(pl.program_id(1) == 0)
    def _():
        o_ref[...] = jnp.zeros_like(o_ref)


def _norm_last_step(x_ref, lg_ref, lb_ref, o_ref, ob_ref):
    @pl.when(pl.program_id(1) == pl.num_programs(1) - 1)
    def _():
        xn = _layer_norm(DN_ALPHA * x_ref[...] + o_ref[...], lg_ref[...], lb_ref[...])
        o_ref[...] = xn
        ob_ref[...] = xn.astype(BF16)


def _merge_kernel(ya_ref, yb_ref, yc_ref, ga_ref, gb_ref, gc_ref, wup_ref, o_ref):
    merged = (_sigmoid(ga_ref[...].astype(F32)) * _dot(ya_ref[...], wup_ref[0])
              + _sigmoid(gb_ref[...].astype(F32)) * _dot(yb_ref[...], wup_ref[1])
              + _sigmoid(gc_ref[...].astype(F32)) * _dot(yc_ref[...], wup_ref[2]))
    o_ref[...] = merged.astype(o_ref.dtype)


def _merge(ya, yb, yc, gates, wup, layer):
    n = ya.shape[0]
    tm = min(MERGE_TM, n)
    tn = MERGE_TILE

    def gate(branch):
        off = branch * D_MODEL // tn
        return pl.BlockSpec((tm, tn), lambda i, j: (i, off + j))

    ybs = pl.BlockSpec((tm, BRANCH_WIDTH), lambda i, j: (i, 0))
    return pl.pallas_call(
        _merge_kernel,
        grid=(n // tm, D_MODEL // tn),
        in_specs=[ybs, ybs, ybs, gate(0), gate(1), gate(2),
                  pl.BlockSpec((None, N_BRANCH, BRANCH_WIDTH, tn), lambda i, j: (layer, 0, 0, j))],
        out_specs=pl.BlockSpec((tm, tn), lambda i, j: (i, j)),
        out_shape=jax.ShapeDtypeStruct((n, D_MODEL), BF16),
        compiler_params=_cparams(("parallel", "arbitrary")),
        name="merge",
    )(ya, yb, yc, gates, gates, gates, wup)


def _mixer_out_kernel(m_ref, wout_ref, x_ref, lg_ref, lb_ref, o_ref, ob_ref):
    xn = _layer_norm(DN_ALPHA * x_ref[...] + _dot(m_ref[...], wout_ref[...]), lg_ref[...], lb_ref[...])
    o_ref[...] = xn
    ob_ref[...] = xn.astype(BF16)


def _mixer_out(merged, wout, x, ln_g, ln_b, layer):
    n = x.shape[0]
    tm = min(ROW_TILE, n)
    row = pl.BlockSpec((tm, D_MODEL), lambda i: (i, 0))
    vec = pl.BlockSpec((None, 1, D_MODEL), lambda i: (layer, 0, 0))
    return pl.pallas_call(
        _mixer_out_kernel,
        grid=(n // tm,),
        in_specs=[row, pl.BlockSpec((None, D_MODEL, D_MODEL), lambda i: (layer, 0, 0)), row, vec, vec],
        out_specs=[row, row],
        out_shape=[jax.ShapeDtypeStruct((n, D_MODEL), F32), jax.ShapeDtypeStruct((n, D_MODEL), BF16)],
        compiler_params=_cparams(("parallel",)),
        name="mixer_out",
    )(merged, wout, x, ln_g, ln_b)


def _mlp_kernel(xb_ref, w1_ref, w2_ref, x_ref, lg_ref, lb_ref, o_ref, ob_ref):
    _zero_first_step(o_ref)
    hid = jnp.maximum(_dot(xb_ref[...], w1_ref[...]), 0.0)
    o_ref[...] += _dot((hid * hid).astype(BF16), w2_ref[...])
    _norm_last_step(x_ref, lg_ref, lb_ref, o_ref, ob_ref)


def _mlp(xb, w1, w2, x, ln_g, ln_b, layer):
    n = x.shape[0]
    tm = min(ROW_TILE, n)
    th = HID_TILE
    row = pl.BlockSpec((tm, D_MODEL), lambda i, k: (i, 0))
    vec = pl.BlockSpec((None, 1, D_MODEL), lambda i, k: (layer, 0, 0))
    return pl.pallas_call(
        _mlp_kernel,
        grid=(n // tm, MLP_HIDDEN // th),
        in_specs=[row,
                  pl.BlockSpec((None, D_MODEL, th), lambda i, k: (layer, 0, k)),
                  pl.BlockSpec((None, th, D_MODEL), lambda i, k: (layer, k, 0)),
                  row, vec, vec],
        out_specs=[row, row],
        out_shape=[jax.ShapeDtypeStruct((n, D_MODEL), F32), jax.ShapeDtypeStruct((n, D_MODEL), BF16)],
        compiler_params=_cparams(("parallel", "arbitrary")),
        name="mlp",
    )(xb, w1, w2, x, ln_g, ln_b)


def _split_w_in(w):
    w = w.astype(BF16)
    pad = jnp.zeros(w.shape[:-1] + (LANES - GLA_GATE_RANK,), w.dtype)
    w_gla = jnp.concatenate([w[..., :W_IN_GLOW], w[..., W_IN_GATEB:W_IN_HGRN], w[..., W_IN_GLOW:W_IN_GATEB], pad],
                            axis=-1)
    return w_gla, w[..., W_IN_HGRN:W_IN_MERGE], w[..., W_IN_MERGE:]


def _row(v):
    return v.astype(F32)[:, None, :]


def kernel(x, w_in, s5_lam_re, s5_lam_im, s5_log_dt, s5_b_re, s5_b_im, s5_c_re, s5_c_im, s5_d, s5_w_glu, s5_b_glu, gla_w_gate, gla_b_gate, gla_norm_w, hgrn_lb_logits, hgrn_norm_w, w_up, w_out, ln1_g, ln1_b, ln2_g, ln2_b, w_mlp_in, w_mlp_out):
    bsz, seq, d = x.shape
    n = bsz * seq
    depth = w_in.shape[0]
    p = jax.nn.softmax(hgrn_lb_logits.astype(F32), axis=0)
    lb = _row(jnp.cumsum(p, axis=0) - p[0])
    w_gla_b, w_hgrn_b, w_gates_b = _split_w_in(w_in)
    s5w = jax.vmap(functools.partial(_s5_weights, folds_per_seq=seq // S5_FOLD))(
        s5_lam_re, s5_lam_im, s5_log_dt, s5_b_re, s5_b_im, s5_c_re, s5_c_im, s5_d)
    w_glu_b = s5_w_glu.astype(BF16)
    b_glu = _row(s5_b_glu)
    wg = jnp.concatenate([gla_w_gate, jnp.zeros((depth, LANES - GLA_GATE_RANK, GLA_KEY), gla_w_gate.dtype)],
                         axis=1).astype(BF16)
    bg, gla_nw, hgrn_nw = _row(gla_b_gate), _row(gla_norm_w), _row(hgrn_norm_w)
    w_up_b, w_out_b = w_up.astype(BF16), w_out.astype(BF16)
    w1_b, w2_b = w_mlp_in.astype(BF16), w_mlp_out.astype(BF16)
    g1, b1, g2, b2 = _row(ln1_g), _row(ln1_b), _row(ln2_g), _row(ln2_b)

    xf = x.reshape(n, d).astype(F32)
    xb = xf.astype(BF16)
    for l in range(depth):
        gates = _proj(xb, w_gates_b, l, 0, MERGE_WIDTH, BF16)
        u, yb = _gla(xb, w_gla_b, wg, bg, gla_nw, seq, l)
        ya = _glu(_s5_scan(u, bsz, seq, s5w, l), w_glu_b, b_glu, l)
        yc = _hgrn(xb, w_hgrn_b, lb, hgrn_nw, seq, l)
        xf, xb = _mixer_out(_merge(ya, yb, yc, gates, w_up_b, l), w_out_b, xf, g1, b1, l)
        xf, xb = _mlp(xb, w1_b, w2_b, xf, g2, b2, l)
    return xf.reshape(bsz, seq, d).astype(x.dtype)
```

```python
import functools
import math

import jax
import jax.numpy as jnp
from jax import lax
from jax.experimental import pallas as pl
from jax.experimental.pallas import tpu as pltpu

F32 = jnp.float32
BF16 = jnp.bfloat16

D_MODEL = 2048
DEPTH = 4
N_BRANCH = 3
BRANCH_WIDTH = D_MODEL // 2
S5_GROUP = 16
S5_GROUPS = BRANCH_WIDTH // S5_GROUP
S5_STATE = 64
S5_MIN_DECAY = 1e-4
GLA_HEADS = 4
GLA_DV = BRANCH_WIDTH // GLA_HEADS
GLA_DK = GLA_DV // 2
GLA_KEY = GLA_HEADS * GLA_DK
GLA_GATE_RANK = 16
GLA_GATE_TAU = 16.0
HGRN_EXPAND = 128
HGRN_HEADS = BRANCH_WIDTH // HGRN_EXPAND
HGRN_DV = BRANCH_WIDTH // HGRN_HEADS
HGRN_KEY = HGRN_HEADS * HGRN_EXPAND
MLP_HIDDEN = 4 * D_MODEL
DN_ALPHA = (2 * DEPTH) ** 0.25
LN_EPS = 1e-5
RMS_EPS = 1e-6
LOG2E = 1.4426950408889634

LANES = 128
SUBLANES = 8
VMEM_LIMIT_BYTES = 56 * 1024 * 1024

W_IN_GLOW = BRANCH_WIDTH + 2 * GLA_KEY + BRANCH_WIDTH
W_IN_GATEB = W_IN_GLOW + GLA_GATE_RANK
W_IN_HGRN = W_IN_GATEB + BRANCH_WIDTH
W_IN_MERGE = W_IN_HGRN + 2 * HGRN_KEY + 2 * BRANCH_WIDTH
GLA_U = 0
GLA_Q = GLA_U + BRANCH_WIDTH
GLA_K = GLA_Q + GLA_KEY
GLA_V = GLA_K + GLA_KEY
GLA_GATE = GLA_V + BRANCH_WIDTH
GLA_GLOW = GLA_GATE + BRANCH_WIDTH
GLA_COLS = GLA_GLOW + LANES
HGRN_Q = 0
HGRN_F = HGRN_Q + HGRN_KEY
HGRN_I = HGRN_F + HGRN_KEY
HGRN_GATE = HGRN_I + BRANCH_WIDTH
HGRN_COLS = HGRN_GATE + BRANCH_WIDTH
MERGE_WIDTH = N_BRANCH * D_MODEL

S5_FOLD = SUBLANES
S5_TILE_GROUPS = LANES // S5_GROUP
S5_TILES = S5_GROUPS // S5_TILE_GROUPS
S5_TILE_STATE = S5_TILE_GROUPS * S5_STATE
S5_ROWS = S5_FOLD * LANES
S5_INTRA_PIECE = 256
ATT_CHUNK = 128
ATT_STEP_ROWS = 512
PROJ_PIECE = 256
PROJ_PIECE_ROWS = 256
ROW_TILE = 512
PROJ_TM = 2048
PROJ_TN = 1024
MERGE_TM = 1024
MERGE_TILE = 1024
HID_TILE = 1024


def _cparams(sem):
    return pltpu.CompilerParams(dimension_semantics=sem, vmem_limit_bytes=VMEM_LIMIT_BYTES)


def _dot(a, b):
    return jnp.dot(a, b, preferred_element_type=F32)


def _dot_nt(a, b):
    return lax.dot_general(a, b, (((1,), (1,)), ((), ())), preferred_element_type=F32)


def _dot_tn(a, b):
    return lax.dot_general(a, b, (((0,), (0,)), ((), ())), preferred_element_type=F32)


def _sigmoid(x):
    return 0.5 * jnp.tanh(0.5 * x) + 0.5


def _sigmoid_rel(x):
    return 1.0 / (1.0 + jnp.exp2(x * -LOG2E))


def _layer_norm(v, g, b):
    mu = jnp.mean(v, axis=-1, keepdims=True)
    xc = v - mu
    var = jnp.mean(xc * xc, axis=-1, keepdims=True)
    return xc * lax.rsqrt(var + LN_EPS) * g + b


def _proj_kernel(x_ref, w_ref, o_ref):
    o_ref[...] = _dot_nt(x_ref[...], w_ref[...]).astype(o_ref.dtype)


def _proj(xb, wt, layer, out_dtype):
    n, k = xb.shape
    width = wt.shape[1]
    tm = min(PROJ_TM, n)
    return pl.pallas_call(
        _proj_kernel,
        grid=(n // tm, width // PROJ_TN),
        in_specs=[pl.BlockSpec((tm, k), lambda i, j: (i, 0)),
                  pl.BlockSpec((None, PROJ_TN, k), lambda i, j: (layer, j, 0))],
        out_specs=pl.BlockSpec((tm, PROJ_TN), lambda i, j: (i, j)),
        out_shape=jax.ShapeDtypeStruct((n, width), out_dtype),
        compiler_params=_cparams(("parallel", "arbitrary")),
        name="proj",
    )(xb, wt)


def _s5_expand(ky_ref, kv_ref, kq_ref, my_ref, mv_ref, mq_ref):
    kk = lax.broadcasted_iota(jnp.int32, (LANES, S5_ROWS), 0)
    cc = lax.broadcasted_iota(jnp.int32, (LANES, S5_ROWS), 1)
    e_tc = jnp.where(((kk >> 4) == (cc >> 7)) & ((kk & 15) == (cc & 15)), 1.0, 0.0).astype(BF16)
    e_rp = jnp.where(((kk >> 6) == (cc >> 9)) & ((kk & 63) == (cc & 63)), 1.0, 0.0).astype(BF16)
    rr = lax.broadcasted_iota(jnp.int32, (S5_ROWS, S5_ROWS), 0)
    cc = lax.broadcasted_iota(jnp.int32, (S5_ROWS, S5_ROWS), 1)
    g_jgd, g_rgp = (rr >> 4) & 7, (rr >> 6) & 7
    h_thc, h_rhp = (cc >> 4) & 7, (cc >> 6) & 7
    my_ref[...] = jnp.where(g_jgd == h_thc, _dot(ky_ref[...], e_tc), 0.0).astype(BF16)
    mv_ref[...] = jnp.where(g_jgd == h_rhp, _dot(kv_ref[...], e_rp), 0.0).astype(BF16)
    mq_ref[...] = jnp.where(g_rgp == h_thc, _dot(kq_ref[...], e_tc), 0.0).astype(BF16)


def _scan_steps(s_re, s_im, pos, apr_ref, api_ref, first_step, length, before_step=lambda: None):
    step, dist = first_step, 1
    while dist < length:
        before_step()
        ok = pos >= dist
        sh_re = jnp.where(ok, pltpu.roll(s_re, dist, axis=0), 0.0)
        sh_im = jnp.where(ok, pltpu.roll(s_im, dist, axis=0), 0.0)
        ar = apr_ref[step:step + 1, :]
        ai = api_ref[step:step + 1, :]
        s_re, s_im = s_re + ar * sh_re - ai * sh_im, s_im + ar * sh_im + ai * sh_re
        step += 1
        dist *= 2
    return s_re, s_im


def _s5_kernel(u_ref, ky_ref, kv_ref, kq_ref, d_ref, apr_ref, api_ref, pwr_ref, pwi_ref, y_ref,
               my_ref, mv_ref, mq_ref, sre_ref, sim_ref, pre_ref, pim_ref):
    @pl.when(pl.program_id(1) == 0)
    def _():
        _s5_expand(ky_ref, kv_ref, kq_ref, my_ref, mv_ref, mq_ref)

    folds = u_ref.shape[0] // S5_FOLD
    runs = folds // SUBLANES
    us = [u_ref[pl.ds(t, folds, stride=S5_FOLD), :] for t in range(S5_FOLD)]
    ucat = jnp.concatenate([u.astype(BF16) for u in us], axis=1)
    r = _dot(ucat, mv_ref[...])
    y_parts = []

    def intra_piece():
        cs = slice(len(y_parts) * S5_INTRA_PIECE, (len(y_parts) + 1) * S5_INTRA_PIECE)
        y_parts.append(_dot(ucat, my_ref[:, cs]))

    fold = lax.broadcasted_iota(jnp.int32, (folds, S5_TILE_STATE), 0)
    s_re, s_im = _scan_steps(r[:, :S5_TILE_STATE], r[:, S5_TILE_STATE:], fold & (SUBLANES - 1),
                             apr_ref, api_ref, 0, SUBLANES, intra_piece)
    while len(y_parts) * S5_INTRA_PIECE < S5_ROWS:
        intra_piece()
    y_intra = jnp.concatenate(y_parts, axis=1)
    lane_tiles = S5_TILE_STATE // LANES

    def put(ref, j, val):
        for lt in range(lane_tiles):
            ref[lt, pl.ds(j, runs, stride=SUBLANES), :] = val[:, lt * LANES:(lt + 1) * LANES]

    def get(ref, j):
        return jnp.concatenate([ref[lt, pl.ds(j, runs, stride=SUBLANES), :] for lt in range(lane_tiles)], axis=1)

    for lt in range(lane_tiles):
        sre_ref[lt] = s_re[:, lt * LANES:(lt + 1) * LANES]
        sim_ref[lt] = s_im[:, lt * LANES:(lt + 1) * LANES]
    last = SUBLANES - 1
    run = lax.broadcasted_iota(jnp.int32, (runs, S5_TILE_STATE), 0)
    e_re, e_im = _scan_steps(get(sre_ref, last), get(sim_ref, last), run, apr_ref, api_ref,
                             int(math.log2(SUBLANES)), runs)
    c_re = jnp.where(run >= 1, pltpu.roll(e_re, 1, axis=0), 0.0)
    c_im = jnp.where(run >= 1, pltpu.roll(e_im, 1, axis=0), 0.0)
    put(pre_ref, 0, c_re)
    put(pim_ref, 0, c_im)
    for j in range(last):
        l_re, l_im = get(sre_ref, j), get(sim_ref, j)
        br, bi = pwr_ref[j:j + 1, :], pwi_ref[j:j + 1, :]
        put(pre_ref, j + 1, l_re + br * c_re - bi * c_im)
        put(pim_ref, j + 1, l_im + br * c_im + bi * c_re)
    prev = jnp.concatenate([pre_ref[lt] for lt in range(lane_tiles)]
                           + [pim_ref[lt] for lt in range(lane_tiles)], axis=1).astype(BF16)
    y = y_intra + _dot(prev, mq_ref[...])
    for t in range(S5_FOLD):
        y_ref[pl.ds(t, folds, stride=S5_FOLD), :] = y[:, t * LANES:(t + 1) * LANES] + d_ref[...] * us[t]


def _s5_scan(h, bsz, seq, s5w, layer):
    k_y, k_v, k_q, d_t, ap_re, ap_im, pw_re, pw_im = s5w
    nsteps = ap_re.shape[2]
    folds = seq // S5_FOLD
    tile = lambda j, b: (layer, j, 0, 0)
    compact = pl.BlockSpec((None, None, S5_ROWS, LANES), tile)
    steps = pl.BlockSpec((None, None, nsteps, S5_TILE_STATE), tile)
    powers = pl.BlockSpec((None, None, SUBLANES, S5_TILE_STATE), tile)
    return pl.pallas_call(
        _s5_kernel,
        grid=(S5_TILES, bsz),
        in_specs=[pl.BlockSpec((seq, LANES), lambda j, b: (b, j)),
                  compact, compact, compact,
                  pl.BlockSpec((None, None, 1, LANES), tile),
                  steps, steps, powers, powers],
        out_specs=pl.BlockSpec((seq, LANES), lambda j, b: (b, j)),
        out_shape=jax.ShapeDtypeStruct((bsz * seq, BRANCH_WIDTH), F32),
        scratch_shapes=[pltpu.VMEM((S5_ROWS, S5_ROWS), BF16)] * 3
        + [pltpu.VMEM((S5_TILE_STATE // LANES, folds, LANES), F32)] * 4,
        compiler_params=_cparams(("parallel", "arbitrary")),
        name="s5_scan",
    )(h, k_y, k_v, k_q, d_t, ap_re, ap_im, pw_re, pw_im)


def _s5_weights(lam_re, lam_im, log_dt, b_re, b_im, c_re, c_im, d_skip, folds_per_seq):
    t_fold, g_n, p_n, c_n, tg = S5_FOLD, S5_GROUPS, S5_STATE, S5_GROUP, S5_TILE_GROUPS
    tiles = g_n // tg
    lr = jnp.minimum(lam_re.astype(F32), -S5_MIN_DECAY)
    li = lam_im.astype(F32)
    dt = jnp.exp(log_dt.astype(F32))[:, None]
    mag = jnp.exp(lr * dt)
    abar_re, abar_im = mag * jnp.cos(li * dt), mag * jnp.sin(li * dt)
    den = lr * lr + li * li
    fac_re = ((abar_re - 1.0) * lr + abar_im * li) / den
    fac_im = (abar_im * lr - (abar_re - 1.0) * li) / den
    br, bi = b_re.astype(F32), b_im.astype(F32)
    bbar_re = fac_re[..., None] * br - fac_im[..., None] * bi
    bbar_im = fac_re[..., None] * bi + fac_im[..., None] * br

    def apow(n):
        nn = jnp.asarray(n, F32)[:, None, None]
        m = jnp.exp(nn * (lr * dt)[None])
        return m * jnp.cos(nn * (li * dt)[None]), m * jnp.sin(nn * (li * dt)[None])

    pr, pi = apow(jnp.arange(t_fold + 1))
    cr, ci = c_re.astype(F32), c_im.astype(F32)
    ca_re = cr[None] * pr[:, :, None, :] - ci[None] * pi[:, :, None, :]
    ca_im = cr[None] * pi[:, :, None, :] + ci[None] * pr[:, :, None, :]
    k_tau = jnp.sum(ca_re[:t_fold, :, :, :, None] * bbar_re[None, :, None, :, :]
                    - ca_im[:t_fold, :, :, :, None] * bbar_im[None, :, None, :, :], axis=3)
    jj = jnp.arange(t_fold)[:, None]
    tt = jnp.arange(t_fold)[None, :]
    lag = jnp.clip(tt - jj, 0, t_fold - 1)
    toe = jnp.where((tt >= jj)[:, :, None, None, None], k_tau[lag], 0.0)
    k_y = (toe.reshape(t_fold, t_fold, tiles, tg, c_n, c_n).transpose(2, 0, 3, 5, 1, 4)
           .reshape(tiles, S5_ROWS, LANES))
    rev = t_fold - 1 - jnp.arange(t_fold)
    prj, pij = pr[rev], pi[rev]
    pv = jnp.stack([prj[..., None] * bbar_re[None] - pij[..., None] * bbar_im[None],
                    prj[..., None] * bbar_im[None] + pij[..., None] * bbar_re[None]])
    k_v = (pv.reshape(2, t_fold, tiles, tg, p_n, c_n).transpose(2, 1, 3, 5, 0, 4)
           .reshape(tiles, S5_ROWS, LANES))
    qa = jnp.stack([ca_re[1:], -ca_im[1:]])
    k_q = (qa.reshape(2, t_fold, tiles, tg, c_n, p_n).transpose(2, 0, 3, 5, 1, 4)
           .reshape(tiles, S5_ROWS, LANES))
    d_t = d_skip.astype(F32).reshape(tiles, 1, LANES)
    nsteps = max(int(math.log2(folds_per_seq)), 1)
    sr, si = apow(t_fold * (2 ** jnp.arange(nsteps)))
    ap_re = sr.reshape(nsteps, tiles, tg * p_n).transpose(1, 0, 2)
    ap_im = si.reshape(nsteps, tiles, tg * p_n).transpose(1, 0, 2)
    wr, wi = apow(t_fold * (1 + jnp.arange(SUBLANES)))
    pw_re = wr.reshape(SUBLANES, tiles, tg * p_n).transpose(1, 0, 2)
    pw_im = wi.reshape(SUBLANES, tiles, tg * p_n).transpose(1, 0, 2)
    return k_y.astype(BF16), k_v.astype(BF16), k_q.astype(BF16), d_t, ap_re, ap_im, pw_re, pw_im


def _glu_kernel(y_ref, w_ref, b_ref, o_ref):
    y = y_ref[...]
    z = 0.5 * y * (1.0 + jnp.tanh(math.sqrt(2.0 / math.pi) * (y + 0.044715 * (y * y * y))))
    a = _dot(z.astype(BF16), w_ref[...]) + b_ref[...]
    o_ref[...] = (z * _sigmoid(a)).astype(o_ref.dtype)


def _glu(y, w, b, layer):
    n, width = y.shape
    tm = min(ROW_TILE, n)
    return pl.pallas_call(
        _glu_kernel,
        grid=(n // tm,),
        in_specs=[pl.BlockSpec((tm, width), lambda i: (i, 0)),
                  pl.BlockSpec((None, width, width), lambda i: (layer, 0, 0)),
                  pl.BlockSpec((None, 1, width), lambda i: (layer, 0, 0))],
        out_specs=pl.BlockSpec((tm, width), lambda i: (i, 0)),
        out_shape=jax.ShapeDtypeStruct((n, width), BF16),
        compiler_params=_cparams(("parallel",)),
        name="s5_glu",
    )(y, w, b)


ATT_STAGES = 9


def _gated_chunks(q, k, v, g, st_ref, between_stages):
    r, kw = q.shape
    c = min(ATT_CHUNK, r)
    chunks = [slice(i * c, (i + 1) * c) for i in range(r // c)]
    heads = kw // LANES
    dv = v.shape[1] // heads
    row = lax.broadcasted_iota(jnp.int32, (r, kw), 0)
    ri = lax.broadcasted_iota(jnp.int32, (c, c), 0)
    ci = lax.broadcasted_iota(jnp.int32, (c, c), 1)
    xr = jnp.where(ri > ci, ri ^ ci, 0)
    ksl = [slice(hd * LANES, (hd + 1) * LANES) for hd in range(heads)]
    vsl = [slice(hd * dv, (hd + 1) * dv) for hd in range(heads)]
    pairs = [(rows, s) for rows in chunks for s in ksl]
    tril = jnp.where(ri >= ci, 1.0, 0.0).astype(BF16)
    g_hi = g.astype(BF16)
    g_rest = g - g_hi.astype(F32)
    g_mid = g_rest.astype(BF16)
    g_lo = (g_rest - g_mid.astype(F32)).astype(BF16)
    gc = jnp.concatenate([_dot(tril, g_hi[rows]) + _dot(tril, g_mid[rows]) + _dot(tril, g_lo[rows])
                          for rows in chunks], axis=0)

    def block_row(x, blk, pos):
        x3 = x.reshape(r // blk, blk, kw)
        return jnp.broadcast_to(x3[:, pos:pos + 1, :], x3.shape).reshape(r, kw)

    between_stages()
    qb, kb = q.astype(BF16), k.astype(BF16)
    a = [jnp.where(ri == ci, _dot_nt(qb[rows, s], kb[rows, s]), 0.0) for rows, s in pairs]
    end = gc
    half = 1
    while half < c:
        between_stages()
        if half < SUBLANES:
            first = (row & half) == 0
            ref = jnp.where(first, end, pltpu.roll(end, half, axis=0))
            if 2 * half < SUBLANES:
                end = jnp.where(first, pltpu.roll(end, r - half, axis=0), end)
        else:
            ref = block_row(gc, 2 * half, half - 1)
        e = jnp.exp2(jnp.abs(gc - ref) * -LOG2E).astype(BF16)
        eq, ek = e * qb, e * kb
        level = xr >= half
        a = [jnp.where(level, _dot_nt(eq[rows, s], ek[rows, s]), a_p) for (rows, s), a_p in zip(pairs, a)]
        half *= 2
    between_stages()
    last = block_row(gc, c, c - 1)
    vb = v.astype(BF16)
    qg = (q * jnp.exp(gc)).astype(BF16)
    kd = (k * jnp.exp(last - gc)).astype(BF16)
    outs = []
    for hd in range(heads):
        st = st_ref[hd]
        o_hd = []
        for ic, rows in enumerate(chunks):
            o_hd.append(_dot(a[ic * heads + hd].astype(BF16), vb[rows, vsl[hd]])
                        + _dot_nt(qg[rows, ksl[hd]], st.astype(BF16)))
            keep = jnp.exp(last[rows.start:rows.start + 1, ksl[hd]])
            st = st * keep + _dot_tn(vb[rows, vsl[hd]], kd[rows, ksl[hd]])
        st_ref[hd] = st
        outs.append(jnp.concatenate(o_hd, axis=0))
    return outs


def _chunk_rows(step_rows):
    c = min(ATT_CHUNK, step_rows)
    return [slice(i * c, (i + 1) * c) for i in range(step_rows // c)]


def _begin_mixer_step(x0_ref, xn_ref, w_ref, cur_ref, nxt_ref, st_ref, steps_per_seq):
    step = pl.program_id(0)

    @pl.when(step == 0)
    def _():
        cur_ref[...] = _dot(x0_ref[...], w_ref[...])

    @pl.when(step % steps_per_seq == 0)
    def _():
        st_ref[...] = jnp.zeros_like(st_ref)

    rows, cols = cur_ref.shape
    pieces = [(slice(r0, r0 + PROJ_PIECE_ROWS), slice(c0, min(c0 + PROJ_PIECE, cols)))
              for c0 in range(0, cols, PROJ_PIECE) for r0 in range(0, rows, PROJ_PIECE_ROWS)]
    slots = ATT_STAGES * len(_chunk_rows(rows))
    calls = [0]

    def between_stages():
        slot = calls[0]
        calls[0] += 1
        for p, (rs, cs) in enumerate(pieces):
            if (p * slots) // len(pieces) == slot:
                nxt_ref[rs, cs] = _dot(xn_ref[rs, :], w_ref[:, cs])

    def finish_step():
        assert calls[0] == slots, "every piece of nxt_ref must have been written"
        cur_ref[...] = nxt_ref[...]

    return between_stages, finish_step


def _gla_kernel(x0_ref, xn_ref, w_ref, wg_ref, bg_ref, nw_ref, u_ref, o_ref, cur_ref, nxt_ref, st_ref, *,
                steps_per_seq):
    between_stages, finish_step = _begin_mixer_step(x0_ref, xn_ref, w_ref, cur_ref, nxt_ref, st_ref, steps_per_seq)
    u_ref[...] = cur_ref[:, GLA_U:GLA_U + BRANCH_WIDTH]
    for rows in _chunk_rows(cur_ref.shape[0]):
        glow = cur_ref[rows, GLA_GLOW:GLA_GLOW + LANES].astype(BF16)
        z = _dot(glow, wg_ref[...]) + bg_ref[...]
        g = (jnp.minimum(z, 0.0) - jnp.log1p(jnp.exp(-jnp.abs(z)))) / GLA_GATE_TAU
        outs = _gated_chunks(cur_ref[rows, GLA_Q:GLA_Q + GLA_KEY] * (GLA_DK ** -0.5),
                             cur_ref[rows, GLA_K:GLA_K + GLA_KEY],
                             cur_ref[rows, GLA_V:GLA_V + BRANCH_WIDTH], g, st_ref, between_stages)
        for hd, o in enumerate(outs):
            vs = slice(hd * GLA_DV, (hd + 1) * GLA_DV)
            o = o * lax.rsqrt(jnp.mean(o * o, axis=-1, keepdims=True) + RMS_EPS) * nw_ref[...]
            gate = cur_ref[rows, GLA_GATE + hd * GLA_DV:GLA_GATE + (hd + 1) * GLA_DV]
            o_ref[rows, vs] = (o * (gate * _sigmoid(gate))).astype(o_ref.dtype)
    finish_step()


def _mixer_specs(xb, w, seq, layer):
    n, d = xb.shape
    rows = min(ATT_STEP_ROWS, seq)
    steps = n // rows
    x0 = pl.BlockSpec((rows, d), lambda i: (0, 0))
    xn = pl.BlockSpec((rows, d), lambda i: (jnp.minimum(i + 1, steps - 1), 0))
    wspec = pl.BlockSpec((None,) + w.shape[1:], lambda i: (layer, 0, 0), pipeline_mode=pl.Buffered(1))
    out = pl.BlockSpec((rows, BRANCH_WIDTH), lambda i: (i, 0))
    proj = pltpu.VMEM((rows, w.shape[2]), F32)
    return rows, steps, x0, xn, wspec, out, proj


def _gla(xb, w, wg, bg, nw, seq, layer):
    rows, steps, x0, xn, wspec, out, proj = _mixer_specs(xb, w, seq, layer)
    vec = lambda i: (layer, 0, 0)
    return pl.pallas_call(
        functools.partial(_gla_kernel, steps_per_seq=seq // rows),
        grid=(steps,),
        in_specs=[x0, xn, wspec,
                  pl.BlockSpec((None, LANES, GLA_KEY), vec),
                  pl.BlockSpec((None, 1, GLA_KEY), vec),
                  pl.BlockSpec((None, 1, GLA_DV), vec)],
        out_specs=[out, out],
        out_shape=[jax.ShapeDtypeStruct((xb.shape[0], BRANCH_WIDTH), F32),
                   jax.ShapeDtypeStruct((xb.shape[0], BRANCH_WIDTH), BF16)],
        scratch_shapes=[proj, proj, pltpu.VMEM((GLA_HEADS, GLA_DV, GLA_DK), F32)],
        compiler_params=_cparams(("arbitrary",)),
        name="gla",
    )(xb, xb, w, wg, bg, nw)


def _hgrn_kernel(x0_ref, xn_ref, w_ref, lb_ref, nw_ref, o_ref, cur_ref, nxt_ref, st_ref, *, steps_per_seq):
    between_stages, finish_step = _begin_mixer_step(x0_ref, xn_ref, w_ref, cur_ref, nxt_ref, st_ref, steps_per_seq)
    lb = lb_ref[...]
    for rows in _chunk_rows(cur_ref.shape[0]):
        f = lb + (1.0 - lb) * _sigmoid_rel(cur_ref[rows, HGRN_F:HGRN_F + HGRN_KEY])
        qr = cur_ref[rows, HGRN_Q:HGRN_Q + HGRN_KEY]
        outs = _gated_chunks(qr * _sigmoid(qr), 1.0 - f, cur_ref[rows, HGRN_I:HGRN_I + BRANCH_WIDTH],
                             jnp.log(f), st_ref, between_stages)
        for hd, o in enumerate(outs):
            vs = slice(hd * HGRN_DV, (hd + 1) * HGRN_DV)
            o = o * _sigmoid(cur_ref[rows, HGRN_GATE + hd * HGRN_DV:HGRN_GATE + (hd + 1) * HGRN_DV])
            o = o * lax.rsqrt(jnp.mean(o * o, axis=-1, keepdims=True) + RMS_EPS) * nw_ref[...]
            o_ref[rows, vs] = o.astype(o_ref.dtype)
    finish_step()


def _hgrn(xb, w, lb, nw, seq, layer):
    rows, steps, x0, xn, wspec, out, proj = _mixer_specs(xb, w, seq, layer)
    vec = lambda i: (layer, 0, 0)
    return pl.pallas_call(
        functools.partial(_hgrn_kernel, steps_per_seq=seq // rows),
        grid=(steps,),
        in_specs=[x0, xn, wspec,
                  pl.BlockSpec((None, 1, HGRN_KEY), vec),
                  pl.BlockSpec((None, 1, HGRN_DV), vec)],
        out_specs=out,
        out_shape=jax.ShapeDtypeStruct((xb.shape[0], BRANCH_WIDTH), BF16),
        scratch_shapes=[proj, proj, pltpu.VMEM((HGRN_HEADS, HGRN_DV, HGRN_EXPAND), F32)],
        compiler_params=_cparams(("arbitrary",)),
        name="hgrn",
    )(xb, xb, w, lb, nw)


def _zero_first_step(o_ref):
    @pl.when(pl.program_id(1) == 0)
    def _():
        o_ref[...] = jnp.zeros_like(o_ref)


def _norm_last_step(x_ref, lg_ref, lb_ref, o_ref, ob_ref):
    @pl.when(pl.program_id(1) == pl.num_programs(1) - 1)
    def _():
        xn = _layer_norm(DN_ALPHA * x_ref[...] + o_ref[...], lg_ref[...], lb_ref[...])
        o_ref[...] = xn
        ob_ref[...] = xn.astype(BF16)


def _merge_kernel(ya_ref, yb_ref, yc_ref, ga_ref, gb_ref, gc_ref, wup_ref, o_ref):
    merged = (_sigmoid(ga_ref[...].astype(F32)) * _dot(ya_ref[...], wup_ref[0])
              + _sigmoid(gb_ref[...].astype(F32)) * _dot(yb_ref[...], wup_ref[1])
              + _sigmoid(gc_ref[...].astype(F32)) * _dot(yc_ref[...], wup_ref[2]))
    o_ref[...] = merged.astype(o_ref.dtype)


def _merge(ya, yb, yc, gates, wup, layer):
    n = ya.shape[0]
    tm = min(MERGE_TM, n)
    tn = MERGE_TILE

    def gate(branch):
        off = branch * D_MODEL // tn
        return pl.BlockSpec((tm, tn), lambda i, j: (i, off + j))

    ybs = pl.BlockSpec((tm, BRANCH_WIDTH), lambda i, j: (i, 0))
    return pl.pallas_call(
        _merge_kernel,
        grid=(n // tm, D_MODEL // tn),
        in_specs=[ybs, ybs, ybs, gate(0), gate(1), gate(2),
                  pl.BlockSpec((None, N_BRANCH, BRANCH_WIDTH, tn), lambda i, j: (layer, 0, 0, j))],
        out_specs=pl.BlockSpec((tm, tn), lambda i, j: (i, j)),
        out_shape=jax.ShapeDtypeStruct((n, D_MODEL), BF16),
        compiler_params=_cparams(("parallel", "arbitrary")),
        name="merge",
    )(ya, yb, yc, gates, gates, gates, wup)


def _mixer_out_kernel(m_ref, wout_ref, x_ref, lg_ref, lb_ref, o_ref, ob_ref):
    xn = _layer_norm(DN_ALPHA * x_ref[...] + _dot(m_ref[...], wout_ref[...]), lg_ref[...], lb_ref[...])
    o_ref[...] = xn
    ob_ref[...] = xn.astype(BF16)


def _mixer_out(merged, wout, x, ln_g, ln_b, layer):
    n = x.shape[0]
    tm = min(ROW_TILE, n)
    row = pl.BlockSpec((tm, D_MODEL), lambda i: (i, 0))
    vec = pl.BlockSpec((None, 1, D_MODEL), lambda i: (layer, 0, 0))
    return pl.pallas_call(
        _mixer_out_kernel,
        grid=(n // tm,),
        in_specs=[row, pl.BlockSpec((None, D_MODEL, D_MODEL), lambda i: (layer, 0, 0)), row, vec, vec],
        out_specs=[row, row],
        out_shape=[jax.ShapeDtypeStruct((n, D_MODEL), F32), jax.ShapeDtypeStruct((n, D_MODEL), BF16)],
        compiler_params=_cparams(("parallel",)),
        name="mixer_out",
    )(merged, wout, x, ln_g, ln_b)


def _mlp_kernel(xb_ref, w1_ref, w2_ref, x_ref, lg_ref, lb_ref, o_ref, ob_ref):
    _zero_first_step(o_ref)
    hid = jnp.maximum(_dot(xb_ref[...], w1_ref[...]), 0.0)
    o_ref[...] += _dot((hid * hid).astype(BF16), w2_ref[...])
    _norm_last_step(x_ref, lg_ref, lb_ref, o_ref, ob_ref)


def _mlp(xb, w1, w2, x, ln_g, ln_b, layer):
    n = x.shape[0]
    tm = min(ROW_TILE, n)
    th = HID_TILE
    row = pl.BlockSpec((tm, D_MODEL), lambda i, k: (i, 0))
    vec = pl.BlockSpec((None, 1, D_MODEL), lambda i, k: (layer, 0, 0))
    return pl.pallas_call(
        _mlp_kernel,
        grid=(n // tm, MLP_HIDDEN // th),
        in_specs=[row,
                  pl.BlockSpec((None, D_MODEL, th), lambda i, k: (layer, 0, k)),
                  pl.BlockSpec((None, th, D_MODEL), lambda i, k: (layer, k, 0)),
                  row, vec, vec],
        out_specs=[row, row],
        out_shape=[jax.ShapeDtypeStruct((n, D_MODEL), F32), jax.ShapeDtypeStruct((n, D_MODEL), BF16)],
        compiler_params=_cparams(("parallel", "arbitrary")),
        name="mlp",
    )(xb, w1, w2, x, ln_g, ln_b)


def _split_w_in(w):
    w = w.astype(BF16)
    pad = jnp.zeros(w.shape[:-1] + (LANES - GLA_GATE_RANK,), w.dtype)
    w_gla = jnp.concatenate([w[..., :W_IN_GLOW], w[..., W_IN_GATEB:W_IN_HGRN], w[..., W_IN_GLOW:W_IN_GATEB], pad],
                            axis=-1)
    return w_gla, w[..., W_IN_HGRN:W_IN_MERGE], jnp.swapaxes(w, 1, 2)[:, W_IN_MERGE:]


def _row(v):
    return v.astype(F32)[:, None, :]


def kernel(x, w_in, s5_lam_re, s5_lam_im, s5_log_dt, s5_b_re, s5_b_im, s5_c_re, s5_c_im, s5_d, s5_w_glu, s5_b_glu, gla_w_gate, gla_b_gate, gla_norm_w, hgrn_lb_logits, hgrn_norm_w, w_up, w_out, ln1_g, ln1_b, ln2_g, ln2_b, w_mlp_in, w_mlp_out):
    bsz, seq, d = x.shape
    n = bsz * seq
    depth = w_in.shape[0]
    p = jax.nn.softmax(hgrn_lb_logits.astype(F32), axis=0)
    lb = _row(jnp.cumsum(p, axis=0) - p[0])
    w_gla_b, w_hgrn_b, w_gates_b = _split_w_in(w_in)
    s5w = jax.vmap(functools.partial(_s5_weights, folds_per_seq=seq // S5_FOLD))(
        s5_lam_re, s5_lam_im, s5_log_dt, s5_b_re, s5_b_im, s5_c_re, s5_c_im, s5_d)
    w_glu_b = s5_w_glu.astype(BF16)
    b_glu = _row(s5_b_glu)
    wg = jnp.concatenate([gla_w_gate, jnp.zeros((depth, LANES - GLA_GATE_RANK, GLA_KEY), gla_w_gate.dtype)],
                         axis=1).astype(BF16)
    bg, gla_nw, hgrn_nw = _row(gla_b_gate), _row(gla_norm_w), _row(hgrn_norm_w)
    w_up_b, w_out_b = w_up.astype(BF16), w_out.astype(BF16)
    w1_b, w2_b = w_mlp_in.astype(BF16), w_mlp_out.astype(BF16)
    g1, b1, g2, b2 = _row(ln1_g), _row(ln1_b), _row(ln2_g), _row(ln2_b)

    xf = x.reshape(n, d).astype(F32)
    xb = xf.astype(BF16)
    for l in range(depth):
        gates = _proj(xb, w_gates_b, l, BF16)
        u, yb = _gla(xb, w_gla_b, wg, bg, gla_nw, seq, l)
        ya = _glu(_s5_scan(u, bsz, seq, s5w, l), w_glu_b, b_glu, l)
        yc = _hgrn(xb, w_hgrn_b, lb, hgrn_nw, seq, l)
        xf, xb = _mixer_out(_merge(ya, yb, yc, gates, w_up_b, l), w_out_b, xf, g1, b1, l)
        xf, xb = _mlp(xb, w1_b, w2_b, xf, g2, b2, l)
    return xf.reshape(bsz, seq, d).astype(x.dtype)
```

```python
import functools
import math

import jax
import jax.numpy as jnp
from jax import lax
from jax.experimental import pallas as pl
from jax.experimental.pallas import tpu as pltpu

F32 = jnp.float32
BF16 = jnp.bfloat16

D_MODEL = 2048
DEPTH = 4
N_BRANCH = 3
BRANCH_WIDTH = D_MODEL // 2
S5_GROUP = 16
S5_GROUPS = BRANCH_WIDTH // S5_GROUP
S5_STATE = 64
S5_MIN_DECAY = 1e-4
GLA_HEADS = 4
GLA_DV = BRANCH_WIDTH // GLA_HEADS
GLA_DK = GLA_DV // 2
GLA_KEY = GLA_HEADS * GLA_DK
GLA_GATE_RANK = 16
GLA_GATE_TAU = 16.0
HGRN_EXPAND = 128
HGRN_HEADS = BRANCH_WIDTH // HGRN_EXPAND
HGRN_DV = BRANCH_WIDTH // HGRN_HEADS
HGRN_KEY = HGRN_HEADS * HGRN_EXPAND
MLP_HIDDEN = 4 * D_MODEL
DN_ALPHA = (2 * DEPTH) ** 0.25
LN_EPS = 1e-5
RMS_EPS = 1e-6
LOG2E = 1.4426950408889634

LANES = 128
SUBLANES = 8
VMEM_LIMIT_BYTES = 56 * 1024 * 1024

W_IN_GLOW = BRANCH_WIDTH + 2 * GLA_KEY + BRANCH_WIDTH
W_IN_GATEB = W_IN_GLOW + GLA_GATE_RANK
W_IN_HGRN = W_IN_GATEB + BRANCH_WIDTH
W_IN_MERGE = W_IN_HGRN + 2 * HGRN_KEY + 2 * BRANCH_WIDTH
GLA_U = 0
GLA_Q = GLA_U + BRANCH_WIDTH
GLA_K = GLA_Q + GLA_KEY
GLA_V = GLA_K + GLA_KEY
GLA_GATE = GLA_V + BRANCH_WIDTH
GLA_GLOW = GLA_GATE + BRANCH_WIDTH
GLA_COLS = GLA_GLOW + LANES
HGRN_Q = 0
HGRN_F = HGRN_Q + HGRN_KEY
HGRN_I = HGRN_F + HGRN_KEY
HGRN_GATE = HGRN_I + BRANCH_WIDTH
HGRN_COLS = HGRN_GATE + BRANCH_WIDTH
MERGE_WIDTH = N_BRANCH * D_MODEL

S5_FOLD = SUBLANES
S5_TILE_GROUPS = LANES // S5_GROUP
S5_TILES = S5_GROUPS // S5_TILE_GROUPS
S5_TILE_STATE = S5_TILE_GROUPS * S5_STATE
S5_ROWS = S5_FOLD * LANES
S5_INTRA_PIECE = 256
ATT_CHUNK = 128
ATT_STEP_ROWS = 512
PROJ_PIECE = 256
PROJ_PIECE_ROWS = 512
ROW_TILE = 512
PROJ_TM = 2048
PROJ_TN = 1024
MERGE_TM = 1024
MERGE_TILE = 1024
HID_TILE = 1024


def _cparams(sem):
    return pltpu.CompilerParams(dimension_semantics=sem, vmem_limit_bytes=VMEM_LIMIT_BYTES)


def _dot(a, b):
    return jnp.dot(a, b, preferred_element_type=F32)


def _dot_nt(a, b):
    return lax.dot_general(a, b, (((1,), (1,)), ((), ())), preferred_element_type=F32)


def _dot_tn(a, b):
    return lax.dot_general(a, b, (((0,), (0,)), ((), ())), preferred_element_type=F32)


def _sigmoid(x):
    return 0.5 * jnp.tanh(0.5 * x) + 0.5


def _sigmoid_rel(x):
    return 1.0 / (1.0 + jnp.exp2(x * -LOG2E))


def _layer_norm(v, g, b):
    mu = jnp.mean(v, axis=-1, keepdims=True)
    xc = v - mu
    var = jnp.mean(xc * xc, axis=-1, keepdims=True)
    return xc * lax.rsqrt(var + LN_EPS) * g + b


def _proj_kernel(x_ref, w_ref, o_ref):
    o_ref[...] = _dot_nt(x_ref[...], w_ref[...]).astype(o_ref.dtype)


def _proj(xb, wt, layer, out_dtype):
    n, k = xb.shape
    width = wt.shape[1]
    tm = min(PROJ_TM, n)
    return pl.pallas_call(
        _proj_kernel,
        grid=(n // tm, width // PROJ_TN),
        in_specs=[pl.BlockSpec((tm, k), lambda i, j: (i, 0)),
                  pl.BlockSpec((None, PROJ_TN, k), lambda i, j: (layer, j, 0))],
        out_specs=pl.BlockSpec((tm, PROJ_TN), lambda i, j: (i, j)),
        out_shape=jax.ShapeDtypeStruct((n, width), out_dtype),
        compiler_params=_cparams(("parallel", "arbitrary")),
        name="proj",
    )(xb, wt)


def _s5_expand(ky_ref, kv_ref, kq_ref, my_ref, mv_ref, mq_ref):
    kk = lax.broadcasted_iota(jnp.int32, (LANES, S5_ROWS), 0)
    cc = lax.broadcasted_iota(jnp.int32, (LANES, S5_ROWS), 1)
    e_tc = jnp.where(((kk >> 4) == (cc >> 7)) & ((kk & 15) == (cc & 15)), 1.0, 0.0).astype(BF16)
    e_rp = jnp.where(((kk >> 6) == (cc >> 9)) & ((kk & 63) == (cc & 63)), 1.0, 0.0).astype(BF16)
    rr = lax.broadcasted_iota(jnp.int32, (S5_ROWS, S5_ROWS), 0)
    cc = lax.broadcasted_iota(jnp.int32, (S5_ROWS, S5_ROWS), 1)
    g_jgd, g_rgp = (rr >> 4) & 7, (rr >> 6) & 7
    h_thc, h_rhp = (cc >> 4) & 7, (cc >> 6) & 7
    my_ref[...] = jnp.where(g_jgd == h_thc, _dot(ky_ref[...], e_tc), 0.0).astype(BF16)
    mv_ref[...] = jnp.where(g_jgd == h_rhp, _dot(kv_ref[...], e_rp), 0.0).astype(BF16)
    mq_ref[...] = jnp.where(g_rgp == h_thc, _dot(kq_ref[...], e_tc), 0.0).astype(BF16)


def _scan_steps(s_re, s_im, pos, apr_ref, api_ref, first_step, length, before_step=lambda: None):
    step, dist = first_step, 1
    while dist < length:
        before_step()
        ok = pos >= dist
        sh_re = jnp.where(ok, pltpu.roll(s_re, dist, axis=0), 0.0)
        sh_im = jnp.where(ok, pltpu.roll(s_im, dist, axis=0), 0.0)
        ar = apr_ref[step:step + 1, :]
        ai = api_ref[step:step + 1, :]
        s_re, s_im = s_re + ar * sh_re - ai * sh_im, s_im + ar * sh_im + ai * sh_re
        step += 1
        dist *= 2
    return s_re, s_im


def _s5_kernel(u_ref, ky_ref, kv_ref, kq_ref, d_ref, apr_ref, api_ref, pwr_ref, pwi_ref, y_ref,
               my_ref, mv_ref, mq_ref, sre_ref, sim_ref, pre_ref, pim_ref):
    @pl.when(pl.program_id(1) == 0)
    def _():
        _s5_expand(ky_ref, kv_ref, kq_ref, my_ref, mv_ref, mq_ref)

    folds = u_ref.shape[0] // S5_FOLD
    runs = folds // SUBLANES
    us = [u_ref[pl.ds(t, folds, stride=S5_FOLD), :] for t in range(S5_FOLD)]
    ucat = jnp.concatenate([u.astype(BF16) for u in us], axis=1)
    r = _dot(ucat, mv_ref[...])
    y_parts = []

    def intra_piece():
        cs = slice(len(y_parts) * S5_INTRA_PIECE, (len(y_parts) + 1) * S5_INTRA_PIECE)
        y_parts.append(_dot(ucat, my_ref[:, cs]))

    fold = lax.broadcasted_iota(jnp.int32, (folds, S5_TILE_STATE), 0)
    s_re, s_im = _scan_steps(r[:, :S5_TILE_STATE], r[:, S5_TILE_STATE:], fold & (SUBLANES - 1),
                             apr_ref, api_ref, 0, SUBLANES, intra_piece)
    while len(y_parts) * S5_INTRA_PIECE < S5_ROWS:
        intra_piece()
    y_intra = jnp.concatenate(y_parts, axis=1)
    lane_tiles = S5_TILE_STATE // LANES

    def put(ref, j, val):
        for lt in range(lane_tiles):
            ref[lt, pl.ds(j, runs, stride=SUBLANES), :] = val[:, lt * LANES:(lt + 1) * LANES]

    def get(ref, j):
        return jnp.concatenate([ref[lt, pl.ds(j, runs, stride=SUBLANES), :] for lt in range(lane_tiles)], axis=1)

    for lt in range(lane_tiles):
        sre_ref[lt] = s_re[:, lt * LANES:(lt + 1) * LANES]
        sim_ref[lt] = s_im[:, lt * LANES:(lt + 1) * LANES]
    last = SUBLANES - 1
    run = lax.broadcasted_iota(jnp.int32, (runs, S5_TILE_STATE), 0)
    e_re, e_im = _scan_steps(get(sre_ref, last), get(sim_ref, last), run, apr_ref, api_ref,
                             int(math.log2(SUBLANES)), runs)
    c_re = jnp.where(run >= 1, pltpu.roll(e_re, 1, axis=0), 0.0)
    c_im = jnp.where(run >= 1, pltpu.roll(e_im, 1, axis=0), 0.0)
    put(pre_ref, 0, c_re)
    put(pim_ref, 0, c_im)
    for j in range(last):
        l_re, l_im = get(sre_ref, j), get(sim_ref, j)
        br, bi = pwr_ref[j:j + 1, :], pwi_ref[j:j + 1, :]
        put(pre_ref, j + 1, l_re + br * c_re - bi * c_im)
        put(pim_ref, j + 1, l_im + br * c_im + bi * c_re)
    prev = jnp.concatenate([pre_ref[lt] for lt in range(lane_tiles)]
                           + [pim_ref[lt] for lt in range(lane_tiles)], axis=1).astype(BF16)
    y = y_intra + _dot(prev, mq_ref[...])
    for t in range(S5_FOLD):
        y_ref[pl.ds(t, folds, stride=S5_FOLD), :] = y[:, t * LANES:(t + 1) * LANES] + d_ref[...] * us[t]


def _s5_scan(h, bsz, seq, s5w, layer):
    k_y, k_v, k_q, d_t, ap_re, ap_im, pw_re, pw_im = s5w
    nsteps = ap_re.shape[2]
    folds = seq // S5_FOLD
    tile = lambda j, b: (layer, j, 0, 0)
    compact = pl.BlockSpec((None, None, S5_ROWS, LANES), tile)
    steps = pl.BlockSpec((None, None, nsteps, S5_TILE_STATE), tile)
    powers = pl.BlockSpec((None, None, SUBLANES, S5_TILE_STATE), tile)
    return pl.pallas_call(
        _s5_kernel,
        grid=(S5_TILES, bsz),
        in_specs=[pl.BlockSpec((seq, LANES), lambda j, b: (b, j)),
                  compact, compact, compact,
                  pl.BlockSpec((None, None, 1, LANES), tile),
                  steps, steps, powers, powers],
        out_specs=pl.BlockSpec((seq, LANES), lambda j, b: (b, j)),
        out_shape=jax.ShapeDtypeStruct((bsz * seq, BRANCH_WIDTH), F32),
        scratch_shapes=[pltpu.VMEM((S5_ROWS, S5_ROWS), BF16)] * 3
        + [pltpu.VMEM((S5_TILE_STATE // LANES, folds, LANES), F32)] * 4,
        compiler_params=_cparams(("parallel", "arbitrary")),
        name="s5_scan",
    )(h, k_y, k_v, k_q, d_t, ap_re, ap_im, pw_re, pw_im)


def _s5_weights(lam_re, lam_im, log_dt, b_re, b_im, c_re, c_im, d_skip, folds_per_seq):
    t_fold, g_n, p_n, c_n, tg = S5_FOLD, S5_GROUPS, S5_STATE, S5_GROUP, S5_TILE_GROUPS
    tiles = g_n // tg
    lr = jnp.minimum(lam_re.astype(F32), -S5_MIN_DECAY)
    li = lam_im.astype(F32)
    dt = jnp.exp(log_dt.astype(F32))[:, None]
    mag = jnp.exp(lr * dt)
    abar_re, abar_im = mag * jnp.cos(li * dt), mag * jnp.sin(li * dt)
    den = lr * lr + li * li
    fac_re = ((abar_re - 1.0) * lr + abar_im * li) / den
    fac_im = (abar_im * lr - (abar_re - 1.0) * li) / den
    br, bi = b_re.astype(F32), b_im.astype(F32)
    bbar_re = fac_re[..., None] * br - fac_im[..., None] * bi
    bbar_im = fac_re[..., None] * bi + fac_im[..., None] * br

    def apow(n):
        nn = jnp.asarray(n, F32)[:, None, None]
        m = jnp.exp(nn * (lr * dt)[None])
        return m * jnp.cos(nn * (li * dt)[None]), m * jnp.sin(nn * (li * dt)[None])

    pr, pi = apow(jnp.arange(t_fold + 1))
    cr, ci = c_re.astype(F32), c_im.astype(F32)
    ca_re = cr[None] * pr[:, :, None, :] - ci[None] * pi[:, :, None, :]
    ca_im = cr[None] * pi[:, :, None, :] + ci[None] * pr[:, :, None, :]
    k_tau = jnp.sum(ca_re[:t_fold, :, :, :, None] * bbar_re[None, :, None, :, :]
                    - ca_im[:t_fold, :, :, :, None] * bbar_im[None, :, None, :, :], axis=3)
    jj = jnp.arange(t_fold)[:, None]
    tt = jnp.arange(t_fold)[None, :]
    lag = jnp.clip(tt - jj, 0, t_fold - 1)
    toe = jnp.where((tt >= jj)[:, :, None, None, None], k_tau[lag], 0.0)
    k_y = (toe.reshape(t_fold, t_fold, tiles, tg, c_n, c_n).transpose(2, 0, 3, 5, 1, 4)
           .reshape(tiles, S5_ROWS, LANES))
    rev = t_fold - 1 - jnp.arange(t_fold)
    prj, pij = pr[rev], pi[rev]
    pv = jnp.stack([prj[..., None] * bbar_re[None] - pij[..., None] * bbar_im[None],
                    prj[..., None] * bbar_im[None] + pij[..., None] * bbar_re[None]])
    k_v = (pv.reshape(2, t_fold, tiles, tg, p_n, c_n).transpose(2, 1, 3, 5, 0, 4)
           .reshape(tiles, S5_ROWS, LANES))
    qa = jnp.stack([ca_re[1:], -ca_im[1:]])
    k_q = (qa.reshape(2, t_fold, tiles, tg, c_n, p_n).transpose(2, 0, 3, 5, 1, 4)
           .reshape(tiles, S5_ROWS, LANES))
    d_t = d_skip.astype(F32).reshape(tiles, 1, LANES)
    nsteps = max(int(math.log2(folds_per_seq)), 1)
    sr, si = apow(t_fold * (2 ** jnp.arange(nsteps)))
    ap_re = sr.reshape(nsteps, tiles, tg * p_n).transpose(1, 0, 2)
    ap_im = si.reshape(nsteps, tiles, tg * p_n).transpose(1, 0, 2)
    wr, wi = apow(t_fold * (1 + jnp.arange(SUBLANES)))
    pw_re = wr.reshape(SUBLANES, tiles, tg * p_n).transpose(1, 0, 2)
    pw_im = wi.reshape(SUBLANES, tiles, tg * p_n).transpose(1, 0, 2)
    return k_y.astype(BF16), k_v.astype(BF16), k_q.astype(BF16), d_t, ap_re, ap_im, pw_re, pw_im


def _glu_kernel(y_ref, w_ref, b_ref, o_ref):
    y = y_ref[...]
    z = 0.5 * y * (1.0 + jnp.tanh(math.sqrt(2.0 / math.pi) * (y + 0.044715 * (y * y * y))))
    a = _dot(z.astype(BF16), w_ref[...]) + b_ref[...]
    o_ref[...] = (z * _sigmoid(a)).astype(o_ref.dtype)


def _glu(y, w, b, layer):
    n, width = y.shape
    tm = min(ROW_TILE, n)
    return pl.pallas_call(
        _glu_kernel,
        grid=(n // tm,),
        in_specs=[pl.BlockSpec((tm, width), lambda i: (i, 0)),
                  pl.BlockSpec((None, width, width), lambda i: (layer, 0, 0)),
                  pl.BlockSpec((None, 1, width), lambda i: (layer, 0, 0))],
        out_specs=pl.BlockSpec((tm, width), lambda i: (i, 0)),
        out_shape=jax.ShapeDtypeStruct((n, width), BF16),
        compiler_params=_cparams(("parallel",)),
        name="s5_glu",
    )(y, w, b)


ATT_STAGES = 9


def _gated_chunks(q, k, v, g, st_ref, between_stages):
    r, kw = q.shape
    c = min(ATT_CHUNK, r)
    chunks = [slice(i * c, (i + 1) * c) for i in range(r // c)]
    heads = kw // LANES
    dv = v.shape[1] // heads
    row = lax.broadcasted_iota(jnp.int32, (r, kw), 0)
    ri = lax.broadcasted_iota(jnp.int32, (c, c), 0)
    ci = lax.broadcasted_iota(jnp.int32, (c, c), 1)
    xr = jnp.where(ri > ci, ri ^ ci, 0)
    ksl = [slice(hd * LANES, (hd + 1) * LANES) for hd in range(heads)]
    vsl = [slice(hd * dv, (hd + 1) * dv) for hd in range(heads)]
    pairs = [(rows, s) for rows in chunks for s in ksl]
    tril = jnp.where(ri >= ci, 1.0, 0.0).astype(BF16)
    g_hi = g.astype(BF16)
    g_rest = g - g_hi.astype(F32)
    g_mid = g_rest.astype(BF16)
    g_lo = (g_rest - g_mid.astype(F32)).astype(BF16)
    gc = jnp.concatenate([_dot(tril, g_hi[rows]) + _dot(tril, g_mid[rows]) + _dot(tril, g_lo[rows])
                          for rows in chunks], axis=0)

    def block_row(x, blk, pos):
        x3 = x.reshape(r // blk, blk, kw)
        return jnp.broadcast_to(x3[:, pos:pos + 1, :], x3.shape).reshape(r, kw)

    between_stages()
    qb, kb = q.astype(BF16), k.astype(BF16)
    a = [jnp.where(ri == ci, _dot_nt(qb[rows, s], kb[rows, s]), 0.0) for rows, s in pairs]
    end = gc
    half = 1
    while half < c:
        between_stages()
        if half < SUBLANES:
            first = (row & half) == 0
            ref = jnp.where(first, end, pltpu.roll(end, half, axis=0))
            if 2 * half < SUBLANES:
                end = jnp.where(first, pltpu.roll(end, r - half, axis=0), end)
        else:
            ref = block_row(gc, 2 * half, half - 1)
        e = jnp.exp2(jnp.abs(gc - ref) * -LOG2E).astype(BF16)
        eq, ek = e * qb, e * kb
        level = xr >= half
        a = [jnp.where(level, _dot_nt(eq[rows, s], ek[rows, s]), a_p) for (rows, s), a_p in zip(pairs, a)]
        half *= 2
    between_stages()
    last = block_row(gc, c, c - 1)
    vb = v.astype(BF16)
    qg = (q * jnp.exp(gc)).astype(BF16)
    kd = (k * jnp.exp(last - gc)).astype(BF16)
    outs = []
    for hd in range(heads):
        st = st_ref[hd]
        o_hd = []
        for ic, rows in enumerate(chunks):
            o_hd.append(_dot(a[ic * heads + hd].astype(BF16), vb[rows, vsl[hd]])
                        + _dot_nt(qg[rows, ksl[hd]], st.astype(BF16)))
            keep = jnp.exp(last[rows.start:rows.start + 1, ksl[hd]])
            st = st * keep + _dot_tn(vb[rows, vsl[hd]], kd[rows, ksl[hd]])
        st_ref[hd] = st
        outs.append(jnp.concatenate(o_hd, axis=0))
    return outs


def _chunk_rows(step_rows):
    c = min(ATT_CHUNK, step_rows)
    return [slice(i * c, (i + 1) * c) for i in range(step_rows // c)]


def _begin_mixer_step(x0_ref, xn_ref, w_ref, cur_ref, nxt_ref, st_ref, steps_per_seq):
    step = pl.program_id(0)

    @pl.when(step == 0)
    def _():
        cur_ref[...] = _dot_nt(x0_ref[...], w_ref[...])

    @pl.when(step % steps_per_seq == 0)
    def _():
        st_ref[...] = jnp.zeros_like(st_ref)

    rows, cols = cur_ref.shape
    pieces = [(slice(r0, r0 + PROJ_PIECE_ROWS), slice(c0, min(c0 + PROJ_PIECE, cols)))
              for c0 in range(0, cols, PROJ_PIECE) for r0 in range(0, rows, PROJ_PIECE_ROWS)]
    slots = ATT_STAGES * len(_chunk_rows(rows))
    calls = [0]

    def between_stages():
        slot = calls[0]
        calls[0] += 1
        for p, (rs, cs) in enumerate(pieces):
            if (p * slots) // len(pieces) == slot:
                nxt_ref[rs, cs] = _dot_nt(xn_ref[rs, :], w_ref[cs, :])

    def finish_step():
        assert calls[0] == slots, "every piece of nxt_ref must have been written"
        cur_ref[...] = nxt_ref[...]

    return between_stages, finish_step


def _gla_kernel(x0_ref, xn_ref, w_ref, wg_ref, bg_ref, nw_ref, u_ref, o_ref, cur_ref, nxt_ref, st_ref, *,
                steps_per_seq):
    between_stages, finish_step = _begin_mixer_step(x0_ref, xn_ref, w_ref, cur_ref, nxt_ref, st_ref, steps_per_seq)
    u_ref[...] = cur_ref[:, GLA_U:GLA_U + BRANCH_WIDTH]
    for rows in _chunk_rows(cur_ref.shape[0]):
        glow = cur_ref[rows, GLA_GLOW:GLA_GLOW + LANES].astype(BF16)
        z = _dot(glow, wg_ref[...]) + bg_ref[...]
        g = (jnp.minimum(z, 0.0) - jnp.log1p(jnp.exp(-jnp.abs(z)))) / GLA_GATE_TAU
        outs = _gated_chunks(cur_ref[rows, GLA_Q:GLA_Q + GLA_KEY] * (GLA_DK ** -0.5),
                             cur_ref[rows, GLA_K:GLA_K + GLA_KEY],
                             cur_ref[rows, GLA_V:GLA_V + BRANCH_WIDTH], g, st_ref, between_stages)
        for hd, o in enumerate(outs):
            vs = slice(hd * GLA_DV, (hd + 1) * GLA_DV)
            o = o * lax.rsqrt(jnp.mean(o * o, axis=-1, keepdims=True) + RMS_EPS) * nw_ref[...]
            gate = cur_ref[rows, GLA_GATE + hd * GLA_DV:GLA_GATE + (hd + 1) * GLA_DV]
            o_ref[rows, vs] = (o * (gate * _sigmoid(gate))).astype(o_ref.dtype)
    finish_step()


def _mixer_specs(xb, w, seq, layer):
    n, d = xb.shape
    rows = min(ATT_STEP_ROWS, seq)
    steps = n // rows
    x0 = pl.BlockSpec((rows, d), lambda i: (0, 0))
    xn = pl.BlockSpec((rows, d), lambda i: (jnp.minimum(i + 1, steps - 1), 0))
    wspec = pl.BlockSpec((None,) + w.shape[1:], lambda i: (layer, 0, 0), pipeline_mode=pl.Buffered(1))
    out = pl.BlockSpec((rows, BRANCH_WIDTH), lambda i: (i, 0))
    proj = pltpu.VMEM((rows, w.shape[1]), F32)
    return rows, steps, x0, xn, wspec, out, proj


def _gla(xb, w, wg, bg, nw, seq, layer):
    rows, steps, x0, xn, wspec, out, proj = _mixer_specs(xb, w, seq, layer)
    vec = lambda i: (layer, 0, 0)
    return pl.pallas_call(
        functools.partial(_gla_kernel, steps_per_seq=seq // rows),
        grid=(steps,),
        in_specs=[x0, xn, wspec,
                  pl.BlockSpec((None, LANES, GLA_KEY), vec),
                  pl.BlockSpec((None, 1, GLA_KEY), vec),
                  pl.BlockSpec((None, 1, GLA_DV), vec)],
        out_specs=[out, out],
        out_shape=[jax.ShapeDtypeStruct((xb.shape[0], BRANCH_WIDTH), F32),
                   jax.ShapeDtypeStruct((xb.shape[0], BRANCH_WIDTH), BF16)],
        scratch_shapes=[proj, proj, pltpu.VMEM((GLA_HEADS, GLA_DV, GLA_DK), F32)],
        compiler_params=_cparams(("arbitrary",)),
        name="gla",
    )(xb, xb, w, wg, bg, nw)


def _hgrn_kernel(x0_ref, xn_ref, w_ref, lb_ref, nw_ref, o_ref, cur_ref, nxt_ref, st_ref, *, steps_per_seq):
    between_stages, finish_step = _begin_mixer_step(x0_ref, xn_ref, w_ref, cur_ref, nxt_ref, st_ref, steps_per_seq)
    lb = lb_ref[...]
    for rows in _chunk_rows(cur_ref.shape[0]):
        f = lb + (1.0 - lb) * _sigmoid_rel(cur_ref[rows, HGRN_F:HGRN_F + HGRN_KEY])
        qr = cur_ref[rows, HGRN_Q:HGRN_Q + HGRN_KEY]
        outs = _gated_chunks(qr * _sigmoid(qr), 1.0 - f, cur_ref[rows, HGRN_I:HGRN_I + BRANCH_WIDTH],
                             jnp.log(f), st_ref, between_stages)
        for hd, o in enumerate(outs):
            vs = slice(hd * HGRN_DV, (hd + 1) * HGRN_DV)
            o = o * _sigmoid(cur_ref[rows, HGRN_GATE + hd * HGRN_DV:HGRN_GATE + (hd + 1) * HGRN_DV])
            o = o * lax.rsqrt(jnp.mean(o * o, axis=-1, keepdims=True) + RMS_EPS) * nw_ref[...]
            o_ref[rows, vs] = o.astype(o_ref.dtype)
    finish_step()


def _hgrn(xb, w, lb, nw, seq, layer):
    rows, steps, x0, xn, wspec, out, proj = _mixer_specs(xb, w, seq, layer)
    vec = lambda i: (layer, 0, 0)
    return pl.pallas_call(
        functools.partial(_hgrn_kernel, steps_per_seq=seq // rows),
        grid=(steps,),
        in_specs=[x0, xn, wspec,
                  pl.BlockSpec((None, 1, HGRN_KEY), vec),
                  pl.BlockSpec((None, 1, HGRN_DV), vec)],
        out_specs=out,
        out_shape=jax.ShapeDtypeStruct((xb.shape[0], BRANCH_WIDTH), BF16),
        scratch_shapes=[proj, proj, pltpu.VMEM((HGRN_HEADS, HGRN_DV, HGRN_EXPAND), F32)],
        compiler_params=_cparams(("arbitrary",)),
        name="hgrn",
    )(xb, xb, w, lb, nw)


def _zero_first_step(o_ref):
    @pl.when(pl.program_id(1) == 0)
    def _():
        o_ref[...] = jnp.zeros_like(o_ref)


def _norm_last_step(x_ref, lg_ref, lb_ref, o_ref, ob_ref):
    @pl.when(pl.program_id(1) == pl.num_programs(1) - 1)
    def _():
        xn = _layer_norm(DN_ALPHA * x_ref[...] + o_ref[...], lg_ref[...], lb_ref[...])
        o_ref[...] = xn
        ob_ref[...] = xn.astype(BF16)


def _merge_kernel(ya_ref, yb_ref, yc_ref, ga_ref, gb_ref, gc_ref, wup_ref, o_ref):
    merged = (_sigmoid(ga_ref[...].astype(F32)) * _dot(ya_ref[...], wup_ref[0])
              + _sigmoid(gb_ref[...].astype(F32)) * _dot(yb_ref[...], wup_ref[1])
              + _sigmoid(gc_ref[...].astype(F32)) * _dot(yc_ref[...], wup_ref[2]))
    o_ref[...] = merged.astype(o_ref.dtype)


def _merge(ya, yb, yc, gates, wup, layer):
    n = ya.shape[0]
    tm = min(MERGE_TM, n)
    tn = MERGE_TILE

    def gate(branch):
        off = branch * D_MODEL // tn
        return pl.BlockSpec((tm, tn), lambda i, j: (i, off + j))

    ybs = pl.BlockSpec((tm, BRANCH_WIDTH), lambda i, j: (i, 0))
    return pl.pallas_call(
        _merge_kernel,
        grid=(n // tm, D_MODEL // tn),
        in_specs=[ybs, ybs, ybs, gate(0), gate(1), gate(2),
                  pl.BlockSpec((None, N_BRANCH, BRANCH_WIDTH, tn), lambda i, j: (layer, 0, 0, j))],
        out_specs=pl.BlockSpec((tm, tn), lambda i, j: (i, j)),
        out_shape=jax.ShapeDtypeStruct((n, D_MODEL), BF16),
        compiler_params=_cparams(("parallel", "arbitrary")),
        name="merge",
    )(ya, yb, yc, gates, gates, gates, wup)


def _mixer_out_kernel(m_ref, wout_ref, x_ref, lg_ref, lb_ref, o_ref, ob_ref):
    xn = _layer_norm(DN_ALPHA * x_ref[...] + _dot(m_ref[...], wout_ref[...]), lg_ref[...], lb_ref[...])
    o_ref[...] = xn
    ob_ref[...] = xn.astype(BF16)


def _mixer_out(merged, wout, x, ln_g, ln_b, layer):
    n = x.shape[0]
    tm = min(ROW_TILE, n)
    row = pl.BlockSpec((tm, D_MODEL), lambda i: (i, 0))
    vec = pl.BlockSpec((None, 1, D_MODEL), lambda i: (layer, 0, 0))
    return pl.pallas_call(
        _mixer_out_kernel,
        grid=(n // tm,),
        in_specs=[row, pl.BlockSpec((None, D_MODEL, D_MODEL), lambda i: (layer, 0, 0)), row, vec, vec],
        out_specs=[row, row],
        out_shape=[jax.ShapeDtypeStruct((n, D_MODEL), F32), jax.ShapeDtypeStruct((n, D_MODEL), BF16)],
        compiler_params=_cparams(("parallel",)),
        name="mixer_out",
    )(merged, wout, x, ln_g, ln_b)


def _mlp_kernel(xb_ref, w1_ref, w2_ref, x_ref, lg_ref, lb_ref, o_ref, ob_ref):
    _zero_first_step(o_ref)
    hid = jnp.maximum(_dot(xb_ref[...], w1_ref[...]), 0.0)
    o_ref[...] += _dot((hid * hid).astype(BF16), w2_ref[...])
    _norm_last_step(x_ref, lg_ref, lb_ref, o_ref, ob_ref)


def _mlp(xb, w1, w2, x, ln_g, ln_b, layer):
    n = x.shape[0]
    tm = min(ROW_TILE, n)
    th = HID_TILE
    row = pl.BlockSpec((tm, D_MODEL), lambda i, k: (i, 0))
    vec = pl.BlockSpec((None, 1, D_MODEL), lambda i, k: (layer, 0, 0))
    return pl.pallas_call(
        _mlp_kernel,
        grid=(n // tm, MLP_HIDDEN // th),
        in_specs=[row,
                  pl.BlockSpec((None, D_MODEL, th), lambda i, k: (layer, 0, k)),
                  pl.BlockSpec((None, th, D_MODEL), lambda i, k: (layer, k, 0)),
                  row, vec, vec],
        out_specs=[row, row],
        out_shape=[jax.ShapeDtypeStruct((n, D_MODEL), F32), jax.ShapeDtypeStruct((n, D_MODEL), BF16)],
        compiler_params=_cparams(("parallel", "arbitrary")),
        name="mlp",
    )(xb, w1, w2, x, ln_g, ln_b)


def _split_w_in(w):
    wt = jnp.swapaxes(w, 1, 2).astype(BF16)
    pad = jnp.zeros((w.shape[0], LANES - GLA_GATE_RANK, w.shape[1]), BF16)
    w_gla = jnp.concatenate([wt[:, :W_IN_GLOW], wt[:, W_IN_GATEB:W_IN_HGRN], wt[:, W_IN_GLOW:W_IN_GATEB], pad], axis=1)
    return w_gla, wt[:, W_IN_HGRN:W_IN_MERGE], wt[:, W_IN_MERGE:]


def _row(v):
    return v.astype(F32)[:, None, :]


def kernel(x, w_in, s5_lam_re, s5_lam_im, s5_log_dt, s5_b_re, s5_b_im, s5_c_re, s5_c_im, s5_d, s5_w_glu, s5_b_glu, gla_w_gate, gla_b_gate, gla_norm_w, hgrn_lb_logits, hgrn_norm_w, w_up, w_out, ln1_g, ln1_b, ln2_g, ln2_b, w_mlp_in, w_mlp_out):
    bsz, seq, d = x.shape
    n = bsz * seq
    depth = w_in.shape[0]
    p = jax.nn.softmax(hgrn_lb_logits.astype(F32), axis=0)
    lb = _row(jnp.cumsum(p, axis=0) - p[0])
    w_gla_b, w_hgrn_b, w_gates_b = _split_w_in(w_in)
    s5w = jax.vmap(functools.partial(_s5_weights, folds_per_seq=seq // S5_FOLD))(
        s5_lam_re, s5_lam_im, s5_log_dt, s5_b_re, s5_b_im, s5_c_re, s5_c_im, s5_d)
    w_glu_b = s5_w_glu.astype(BF16)
    b_glu = _row(s5_b_glu)
    wg = jnp.concatenate([gla_w_gate, jnp.zeros((depth, LANES - GLA_GATE_RANK, GLA_KEY), gla_w_gate.dtype)],
                         axis=1).astype(BF16)
    bg, gla_nw, hgrn_nw = _row(gla_b_gate), _row(gla_norm_w), _row(hgrn_norm_w)
    w_up_b, w_out_b = w_up.astype(BF16), w_out.astype(BF16)
    w1_b, w2_b = w_mlp_in.astype(BF16), w_mlp_out.astype(BF16)
    g1, b1, g2, b2 = _row(ln1_g), _row(ln1_b), _row(ln2_g), _row(ln2_b)

    xf = x.reshape(n, d).astype(F32)
    xb = xf.astype(BF16)
    for l in range(depth):
        gates = _proj(xb, w_gates_b, l, BF16)
        u, yb = _gla(xb, w_gla_b, wg, bg, gla_nw, seq, l)
        ya = _glu(_s5_scan(u, bsz, seq, s5w, l), w_glu_b, b_glu, l)
        yc = _hgrn(xb, w_hgrn_b, lb, hgrn_nw, seq, l)
        xf, xb = _mixer_out(_merge(ya, yb, yc, gates, w_up_b, l), w_out_b, xf, g1, b1, l)
        xf, xb = _mlp(xb, w1_b, w2_b, xf, g2, b2, l)
    return xf.reshape(bsz, seq, d).astype(x.dtype)
```

```python
import functools
import math

import jax
import jax.numpy as jnp
from jax import lax
from jax.experimental import pallas as pl
from jax.experimental.pallas import tpu as pltpu

F32 = jnp.float32
BF16 = jnp.bfloat16

D_MODEL = 2048
DEPTH = 4
N_BRANCH = 3
BRANCH_WIDTH = D_MODEL // 2
S5_GROUP = 16
S5_GROUPS = BRANCH_WIDTH // S5_GROUP
S5_STATE = 64
S5_MIN_DECAY = 1e-4
GLA_HEADS = 4
GLA_DV = BRANCH_WIDTH // GLA_HEADS
GLA_DK = GLA_DV // 2
GLA_KEY = GLA_HEADS * GLA_DK
GLA_GATE_RANK = 16
GLA_GATE_TAU = 16.0
HGRN_EXPAND = 128
HGRN_HEADS = BRANCH_WIDTH // HGRN_EXPAND
HGRN_DV = BRANCH_WIDTH // HGRN_HEADS
HGRN_KEY = HGRN_HEADS * HGRN_EXPAND
MLP_HIDDEN = 4 * D_MODEL
DN_ALPHA = (2 * DEPTH) ** 0.25
LN_EPS = 1e-5
RMS_EPS = 1e-6
LOG2E = 1.4426950408889634

LANES = 128
SUBLANES = 8
VMEM_LIMIT_BYTES = 56 * 1024 * 1024

W_IN_GLOW = BRANCH_WIDTH + 2 * GLA_KEY + BRANCH_WIDTH
W_IN_GATEB = W_IN_GLOW + GLA_GATE_RANK
W_IN_HGRN = W_IN_GATEB + BRANCH_WIDTH
W_IN_MERGE = W_IN_HGRN + 2 * HGRN_KEY + 2 * BRANCH_WIDTH
GLA_U = 0
GLA_Q = GLA_U + BRANCH_WIDTH
GLA_K = GLA_Q + GLA_KEY
GLA_V = GLA_K + GLA_KEY
GLA_GATE = GLA_V + BRANCH_WIDTH
GLA_GLOW = GLA_GATE + BRANCH_WIDTH
GLA_COLS = GLA_GLOW + LANES
HGRN_Q = 0
HGRN_F = HGRN_Q + HGRN_KEY
HGRN_I = HGRN_F + HGRN_KEY
HGRN_GATE = HGRN_I + BRANCH_WIDTH
HGRN_COLS = HGRN_GATE + BRANCH_WIDTH
MERGE_WIDTH = N_BRANCH * D_MODEL

S5_FOLD = SUBLANES
S5_TILE_GROUPS = LANES // S5_GROUP
S5_TILES = S5_GROUPS // S5_TILE_GROUPS
S5_TILE_STATE = S5_TILE_GROUPS * S5_STATE
S5_ROWS = S5_FOLD * LANES
S5_INTRA_PIECE = 256
ATT_CHUNK = 128
ATT_STEP_ROWS = 512
PROJ_PIECE = 256
PROJ_PIECE_ROWS = 512
ROW_TILE = 512
PROJ_TM = 2048
PROJ_TN = 1024
MERGE_TM = 1024
MERGE_TILE = 1024
HID_TILE = 1024


def _cparams(sem):
    return pltpu.CompilerParams(dimension_semantics=sem, vmem_limit_bytes=VMEM_LIMIT_BYTES)


def _dot(a, b):
    return jnp.dot(a, b, preferred_element_type=F32)


def _dot_nt(a, b):
    return lax.dot_general(a, b, (((1,), (1,)), ((), ())), preferred_element_type=F32)


def _dot_tn(a, b):
    return lax.dot_general(a, b, (((0,), (0,)), ((), ())), preferred_element_type=F32)


def _sigmoid(x):
    return 0.5 * jnp.tanh(0.5 * x) + 0.5


def _sigmoid_rel(x):
    return 1.0 / (1.0 + jnp.exp2(x * -LOG2E))


def _layer_norm(v, g, b):
    mu = jnp.mean(v, axis=-1, keepdims=True)
    xc = v - mu
    var = jnp.mean(xc * xc, axis=-1, keepdims=True)
    return xc * lax.rsqrt(var + LN_EPS) * g + b


def _proj_kernel(x_ref, w_ref, o_ref):
    o_ref[...] = _dot_nt(x_ref[...], w_ref[...]).astype(o_ref.dtype)


def _proj(xb, wt, layer, out_dtype):
    n, k = xb.shape
    width = wt.shape[1]
    tm = min(PROJ_TM, n)
    return pl.pallas_call(
        _proj_kernel,
        grid=(n // tm, width // PROJ_TN),
        in_specs=[pl.BlockSpec((tm, k), lambda i, j: (i, 0)),
                  pl.BlockSpec((None, PROJ_TN, k), lambda i, j: (layer, j, 0))],
        out_specs=pl.BlockSpec((tm, PROJ_TN), lambda i, j: (i, j)),
        out_shape=jax.ShapeDtypeStruct((n, width), out_dtype),
        compiler_params=_cparams(("parallel", "arbitrary")),
        name="proj",
    )(xb, wt)


def _s5_expand(ky_ref, kv_ref, kq_ref, my_ref, mv_ref, mq_ref):
    kk = lax.broadcasted_iota(jnp.int32, (LANES, S5_ROWS), 0)
    cc = lax.broadcasted_iota(jnp.int32, (LANES, S5_ROWS), 1)
    e_tc = jnp.where(((kk >> 4) == (cc >> 7)) & ((kk & 15) == (cc & 15)), 1.0, 0.0).astype(BF16)
    e_rp = jnp.where(((kk >> 6) == (cc >> 9)) & ((kk & 63) == (cc & 63)), 1.0, 0.0).astype(BF16)
    rr = lax.broadcasted_iota(jnp.int32, (S5_ROWS, S5_ROWS), 0)
    cc = lax.broadcasted_iota(jnp.int32, (S5_ROWS, S5_ROWS), 1)
    g_jgd, g_rgp = (rr >> 4) & 7, (rr >> 6) & 7
    h_thc, h_rhp = (cc >> 4) & 7, (cc >> 6) & 7
    my_ref[...] = jnp.where(g_jgd == h_thc, _dot(ky_ref[...], e_tc), 0.0).astype(BF16)
    mv_ref[...] = jnp.where(g_jgd == h_rhp, _dot(kv_ref[...], e_rp), 0.0).astype(BF16)
    mq_ref[...] = jnp.where(g_rgp == h_thc, _dot(kq_ref[...], e_tc), 0.0).astype(BF16)


def _scan_steps(s_re, s_im, pos, apr_ref, api_ref, first_step, length, before_step=lambda: None):
    step, dist = first_step, 1
    while dist < length:
        before_step()
        ok = pos >= dist
        sh_re = jnp.where(ok, pltpu.roll(s_re, dist, axis=0), 0.0)
        sh_im = jnp.where(ok, pltpu.roll(s_im, dist, axis=0), 0.0)
        ar = apr_ref[step:step + 1, :]
        ai = api_ref[step:step + 1, :]
        s_re, s_im = s_re + ar * sh_re - ai * sh_im, s_im + ar * sh_im + ai * sh_re
        step += 1
        dist *= 2
    return s_re, s_im


def _s5_kernel(u_ref, ky_ref, kv_ref, kq_ref, d_ref, apr_ref, api_ref, pwr_ref, pwi_ref, y_ref,
               my_ref, mv_ref, mq_ref, sre_ref, sim_ref, pre_ref, pim_ref):
    @pl.when(pl.program_id(1) == 0)
    def _():
        _s5_expand(ky_ref, kv_ref, kq_ref, my_ref, mv_ref, mq_ref)

    folds = u_ref.shape[0] // S5_FOLD
    runs = folds // SUBLANES
    us = [u_ref[pl.ds(t, folds, stride=S5_FOLD), :] for t in range(S5_FOLD)]
    ucat = jnp.concatenate([u.astype(BF16) for u in us], axis=1)
    r = _dot(ucat, mv_ref[...])
    y_parts = []

    def intra_piece():
        cs = slice(len(y_parts) * S5_INTRA_PIECE, (len(y_parts) + 1) * S5_INTRA_PIECE)
        y_parts.append(_dot(ucat, my_ref[:, cs]))

    fold = lax.broadcasted_iota(jnp.int32, (folds, S5_TILE_STATE), 0)
    s_re, s_im = _scan_steps(r[:, :S5_TILE_STATE], r[:, S5_TILE_STATE:], fold & (SUBLANES - 1),
                             apr_ref, api_ref, 0, SUBLANES, intra_piece)
    while len(y_parts) * S5_INTRA_PIECE < S5_ROWS:
        intra_piece()
    y_intra = jnp.concatenate(y_parts, axis=1)
    lane_tiles = S5_TILE_STATE // LANES

    def put(ref, j, val):
        for lt in range(lane_tiles):
            ref[lt, pl.ds(j, runs, stride=SUBLANES), :] = val[:, lt * LANES:(lt + 1) * LANES]

    def get(ref, j):
        return jnp.concatenate([ref[lt, pl.ds(j, runs, stride=SUBLANES), :] for lt in range(lane_tiles)], axis=1)

    for lt in range(lane_tiles):
        sre_ref[lt] = s_re[:, lt * LANES:(lt + 1) * LANES]
        sim_ref[lt] = s_im[:, lt * LANES:(lt + 1) * LANES]
    last = SUBLANES - 1
    run = lax.broadcasted_iota(jnp.int32, (runs, S5_TILE_STATE), 0)
    e_re, e_im = _scan_steps(get(sre_ref, last), get(sim_ref, last), run, apr_ref, api_ref,
                             int(math.log2(SUBLANES)), runs)
    c_re = jnp.where(run >= 1, pltpu.roll(e_re, 1, axis=0), 0.0)
    c_im = jnp.where(run >= 1, pltpu.roll(e_im, 1, axis=0), 0.0)
    put(pre_ref, 0, c_re)
    put(pim_ref, 0, c_im)
    for j in range(last):
        l_re, l_im = get(sre_ref, j), get(sim_ref, j)
        br, bi = pwr_ref[j:j + 1, :], pwi_ref[j:j + 1, :]
        put(pre_ref, j + 1, l_re + br * c_re - bi * c_im)
        put(pim_ref, j + 1, l_im + br * c_im + bi * c_re)
    prev = jnp.concatenate([pre_ref[lt] for lt in range(lane_tiles)]
                           + [pim_ref[lt] for lt in range(lane_tiles)], axis=1).astype(BF16)
    y = y_intra + _dot(prev, mq_ref[...])
    for t in range(S5_FOLD):
        y_ref[pl.ds(t, folds, stride=S5_FOLD), :] = y[:, t * LANES:(t + 1) * LANES] + d_ref[...] * us[t]


def _s5_scan(h, bsz, seq, s5w, layer):
    k_y, k_v, k_q, d_t, ap_re, ap_im, pw_re, pw_im = s5w
    nsteps = ap_re.shape[2]
    folds = seq // S5_FOLD
    tile = lambda j, b: (layer, j, 0, 0)
    compact = pl.BlockSpec((None, None, S5_ROWS, LANES), tile)
    steps = pl.BlockSpec((None, None, nsteps, S5_TILE_STATE), tile)
    powers = pl.BlockSpec((None, None, SUBLANES, S5_TILE_STATE), tile)
    return pl.pallas_call(
        _s5_kernel,
        grid=(S5_TILES, bsz),
        in_specs=[pl.BlockSpec((seq, LANES), lambda j, b: (b, j)),
                  compact, compact, compact,
                  pl.BlockSpec((None, None, 1, LANES), tile),
                  steps, steps, powers, powers],
        out_specs=pl.BlockSpec((seq, LANES), lambda j, b: (b, j)),
        out_shape=jax.ShapeDtypeStruct((bsz * seq, BRANCH_WIDTH), F32),
        scratch_shapes=[pltpu.VMEM((S5_ROWS, S5_ROWS), BF16)] * 3
        + [pltpu.VMEM((S5_TILE_STATE // LANES, folds, LANES), F32)] * 4,
        compiler_params=_cparams(("parallel", "arbitrary")),
        name="s5_scan",
    )(h, k_y, k_v, k_q, d_t, ap_re, ap_im, pw_re, pw_im)


def _s5_weights(lam_re, lam_im, log_dt, b_re, b_im, c_re, c_im, d_skip, folds_per_seq):
    t_fold, g_n, p_n, c_n, tg = S5_FOLD, S5_GROUPS, S5_STATE, S5_GROUP, S5_TILE_GROUPS
    tiles = g_n // tg
    lr = jnp.minimum(lam_re.astype(F32), -S5_MIN_DECAY)
    li = lam_im.astype(F32)
    dt = jnp.exp(log_dt.astype(F32))[:, None]
    mag = jnp.exp(lr * dt)
    abar_re, abar_im = mag * jnp.cos(li * dt), mag * jnp.sin(li * dt)
    den = lr * lr + li * li
    fac_re = ((abar_re - 1.0) * lr + abar_im * li) / den
    fac_im = (abar_im * lr - (abar_re - 1.0) * li) / den
    br, bi = b_re.astype(F32), b_im.astype(F32)
    bbar_re = fac_re[..., None] * br - fac_im[..., None] * bi
    bbar_im = fac_re[..., None] * bi + fac_im[..., None] * br

    def apow(n):
        nn = jnp.asarray(n, F32)[:, None, None]
        m = jnp.exp(nn * (lr * dt)[None])
        return m * jnp.cos(nn * (li * dt)[None]), m * jnp.sin(nn * (li * dt)[None])

    pr, pi = apow(jnp.arange(t_fold + 1))
    cr, ci = c_re.astype(F32), c_im.astype(F32)
    ca_re = cr[None] * pr[:, :, None, :] - ci[None] * pi[:, :, None, :]
    ca_im = cr[None] * pi[:, :, None, :] + ci[None] * pr[:, :, None, :]
    k_tau = jnp.sum(ca_re[:t_fold, :, :, :, None] * bbar_re[None, :, None, :, :]
                    - ca_im[:t_fold, :, :, :, None] * bbar_im[None, :, None, :, :], axis=3)
    jj = jnp.arange(t_fold)[:, None]
    tt = jnp.arange(t_fold)[None, :]
    lag = jnp.clip(tt - jj, 0, t_fold - 1)
    toe = jnp.where((tt >= jj)[:, :, None, None, None], k_tau[lag], 0.0)
    k_y = (toe.reshape(t_fold, t_fold, tiles, tg, c_n, c_n).transpose(2, 0, 3, 5, 1, 4)
           .reshape(tiles, S5_ROWS, LANES))
    rev = t_fold - 1 - jnp.arange(t_fold)
    prj, pij = pr[rev], pi[rev]
    pv = jnp.stack([prj[..., None] * bbar_re[None] - pij[..., None] * bbar_im[None],
                    prj[..., None] * bbar_im[None] + pij[..., None] * bbar_re[None]])
    k_v = (pv.reshape(2, t_fold, tiles, tg, p_n, c_n).transpose(2, 1, 3, 5, 0, 4)
           .reshape(tiles, S5_ROWS, LANES))
    qa = jnp.stack([ca_re[1:], -ca_im[1:]])
    k_q = (qa.reshape(2, t_fold, tiles, tg, c_n, p_n).transpose(2, 0, 3, 5, 1, 4)
           .reshape(tiles, S5_ROWS, LANES))
    d_t = d_skip.astype(F32).reshape(tiles, 1, LANES)
    nsteps = max(int(math.log2(folds_per_seq)), 1)
    sr, si = apow(t_fold * (2 ** jnp.arange(nsteps)))
    ap_re = sr.reshape(nsteps, tiles, tg * p_n).transpose(1, 0, 2)
    ap_im = si.reshape(nsteps, tiles, tg * p_n).transpose(1, 0, 2)
    wr, wi = apow(t_fold * (1 + jnp.arange(SUBLANES)))
    pw_re = wr.reshape(SUBLANES, tiles, tg * p_n).transpose(1, 0, 2)
    pw_im = wi.reshape(SUBLANES, tiles, tg * p_n).transpose(1, 0, 2)
    return k_y.astype(BF16), k_v.astype(BF16), k_q.astype(BF16), d_t, ap_re, ap_im, pw_re, pw_im


def _glu_kernel(y_ref, w_ref, b_ref, o_ref):
    y = y_ref[...]
    z = 0.5 * y * (1.0 + jnp.tanh(math.sqrt(2.0 / math.pi) * (y + 0.044715 * (y * y * y))))
    a = _dot(z.astype(BF16), w_ref[...]) + b_ref[...]
    o_ref[...] = (z * _sigmoid(a)).astype(o_ref.dtype)


def _glu(y, w, b, layer):
    n, width = y.shape
    tm = min(ROW_TILE, n)
    return pl.pallas_call(
        _glu_kernel,
        grid=(n // tm,),
        in_specs=[pl.BlockSpec((tm, width), lambda i: (i, 0)),
                  pl.BlockSpec((None, width, width), lambda i: (layer, 0, 0)),
                  pl.BlockSpec((None, 1, width), lambda i: (layer, 0, 0))],
        out_specs=pl.BlockSpec((tm, width), lambda i: (i, 0)),
        out_shape=jax.ShapeDtypeStruct((n, width), BF16),
        compiler_params=_cparams(("parallel",)),
        name="s5_glu",
    )(y, w, b)


ATT_STAGES = 9


def _gated_chunks(q, k, v, g, st_ref, between_stages):
    r, kw = q.shape
    c = min(ATT_CHUNK, r)
    chunks = [slice(i * c, (i + 1) * c) for i in range(r // c)]
    heads = kw // LANES
    dv = v.shape[1] // heads
    row = lax.broadcasted_iota(jnp.int32, (r, kw), 0)
    ri = lax.broadcasted_iota(jnp.int32, (c, c), 0)
    ci = lax.broadcasted_iota(jnp.int32, (c, c), 1)
    xr = jnp.where(ri > ci, ri ^ ci, 0)
    ksl = [slice(hd * LANES, (hd + 1) * LANES) for hd in range(heads)]
    vsl = [slice(hd * dv, (hd + 1) * dv) for hd in range(heads)]
    pairs = [(rows, s) for rows in chunks for s in ksl]
    tril = jnp.where(ri >= ci, 1.0, 0.0).astype(BF16)
    g_hi = g.astype(BF16)
    g_rest = g - g_hi.astype(F32)
    g_mid = g_rest.astype(BF16)
    g_lo = (g_rest - g_mid.astype(F32)).astype(BF16)
    gc = jnp.concatenate([_dot(tril, g_hi[rows]) + _dot(tril, g_mid[rows]) + _dot(tril, g_lo[rows])
                          for rows in chunks], axis=0)

    def block_row(x, blk, pos):
        x3 = x.reshape(r // blk, blk, kw)
        return jnp.broadcast_to(x3[:, pos:pos + 1, :], x3.shape).reshape(r, kw)

    between_stages()
    qb, kb = q.astype(BF16), k.astype(BF16)
    a = [jnp.where(ri == ci, _dot_nt(qb[rows, s], kb[rows, s]), 0.0) for rows, s in pairs]
    end = gc
    half = 1
    while half < c:
        between_stages()
        if half < SUBLANES:
            first = (row & half) == 0
            ref = jnp.where(first, end, pltpu.roll(end, half, axis=0))
            if 2 * half < SUBLANES:
                end = jnp.where(first, pltpu.roll(end, r - half, axis=0), end)
        else:
            ref = block_row(gc, 2 * half, half - 1)
        e = jnp.exp2(jnp.abs(gc - ref) * -LOG2E).astype(BF16)
        eq, ek = e * qb, e * kb
        level = xr >= half
        a = [jnp.where(level, _dot_nt(eq[rows, s], ek[rows, s]), a_p) for (rows, s), a_p in zip(pairs, a)]
        half *= 2
    between_stages()
    last = block_row(gc, c, c - 1)
    vb = v.astype(BF16)
    qg = (q * jnp.exp(gc)).astype(BF16)
    kd = (k * jnp.exp(last - gc)).astype(BF16)
    outs = []
    for hd in range(heads):
        st = st_ref[hd]
        o_hd = []
        for ic, rows in enumerate(chunks):
            o_hd.append(_dot(a[ic * heads + hd].astype(BF16), vb[rows, vsl[hd]])
                        + _dot_nt(qg[rows, ksl[hd]], st.astype(BF16)))
            keep = jnp.exp(last[rows.start:rows.start + 1, ksl[hd]])
            st = st * keep + _dot_tn(vb[rows, vsl[hd]], kd[rows, ksl[hd]])
        st_ref[hd] = st
        outs.append(jnp.concatenate(o_hd, axis=0))
    return outs


def _chunk_rows(step_rows):
    c = min(ATT_CHUNK, step_rows)
    return [slice(i * c, (i + 1) * c) for i in range(step_rows // c)]


def _begin_mixer_step(x0_ref, xn_ref, w_ref, cur_ref, nxt_ref, st_ref, steps_per_seq):
    step = pl.program_id(0)

    @pl.when(step == 0)
    def _():
        cur_ref[...] = _dot_nt(x0_ref[...], w_ref[...])

    @pl.when(step % steps_per_seq == 0)
    def _():
        st_ref[...] = jnp.zeros_like(st_ref)

    rows, cols = cur_ref.shape
    pieces = [(slice(r0, r0 + PROJ_PIECE_ROWS), slice(c0, min(c0 + PROJ_PIECE, cols)))
              for c0 in range(0, cols, PROJ_PIECE) for r0 in range(0, rows, PROJ_PIECE_ROWS)]
    slots = ATT_STAGES * len(_chunk_rows(rows))
    calls = [0]

    def between_stages():
        slot = calls[0]
        calls[0] += 1
        for p, (rs, cs) in enumerate(pieces):
            if (p * slots) // len(pieces) == slot:
                nxt_ref[rs, cs] = _dot_nt(xn_ref[rs, :], w_ref[cs, :])

    def finish_step():
        assert calls[0] == slots, "every piece of nxt_ref must have been written"
        cur_ref[...] = nxt_ref[...]

    return between_stages, finish_step


def _gla_kernel(x0_ref, xn_ref, w_ref, wg_ref, bg_ref, nw_ref, u_ref, o_ref, cur_ref, nxt_ref, st_ref, *,
                steps_per_seq):
    between_stages, finish_step = _begin_mixer_step(x0_ref, xn_ref, w_ref, cur_ref, nxt_ref, st_ref, steps_per_seq)
    u_ref[...] = cur_ref[:, GLA_U:GLA_U + BRANCH_WIDTH]
    for rows in _chunk_rows(cur_ref.shape[0]):
        glow = cur_ref[rows, GLA_GLOW:GLA_GLOW + LANES].astype(BF16)
        z = _dot(glow, wg_ref[...]) + bg_ref[...]
        g = (jnp.minimum(z, 0.0) - jnp.log1p(jnp.exp(-jnp.abs(z)))) / GLA_GATE_TAU
        outs = _gated_chunks(cur_ref[rows, GLA_Q:GLA_Q + GLA_KEY] * (GLA_DK ** -0.5),
                             cur_ref[rows, GLA_K:GLA_K + GLA_KEY],
                             cur_ref[rows, GLA_V:GLA_V + BRANCH_WIDTH], g, st_ref, between_stages)
        for hd, o in enumerate(outs):
            vs = slice(hd * GLA_DV, (hd + 1) * GLA_DV)
            o = o * lax.rsqrt(jnp.mean(o * o, axis=-1, keepdims=True) + RMS_EPS) * nw_ref[...]
            gate = cur_ref[rows, GLA_GATE + hd * GLA_DV:GLA_GATE + (hd + 1) * GLA_DV]
            o_ref[rows, vs] = (o * (gate * _sigmoid(gate))).astype(o_ref.dtype)
    finish_step()


def _mixer_specs(xb, w, seq, layer):
    n, d = xb.shape
    rows = min(ATT_STEP_ROWS, seq)
    steps = n // rows
    x0 = pl.BlockSpec((rows, d), lambda i: (0, 0))
    xn = pl.BlockSpec((rows, d), lambda i: (jnp.minimum(i + 1, steps - 1), 0))
    wspec = pl.BlockSpec((None,) + w.shape[1:], lambda i: (layer, 0, 0), pipeline_mode=pl.Buffered(1))
    out = pl.BlockSpec((rows, BRANCH_WIDTH), lambda i: (i, 0))
    proj = pltpu.VMEM((rows, w.shape[1]), F32)
    return rows, steps, x0, xn, wspec, out, proj


def _gla(xb, w, wg, bg, nw, seq, layer):
    rows, steps, x0, xn, wspec, out, proj = _mixer_specs(xb, w, seq, layer)
    vec = lambda i: (layer, 0, 0)
    return pl.pallas_call(
        functools.partial(_gla_kernel, steps_per_seq=seq // rows),
        grid=(steps,),
        in_specs=[x0, xn, wspec,
                  pl.BlockSpec((None, LANES, GLA_KEY), vec),
                  pl.BlockSpec((None, 1, GLA_KEY), vec),
                  pl.BlockSpec((None, 1, GLA_DV), vec)],
        out_specs=[out, out],
        out_shape=[jax.ShapeDtypeStruct((xb.shape[0], BRANCH_WIDTH), F32),
                   jax.ShapeDtypeStruct((xb.shape[0], BRANCH_WIDTH), BF16)],
        scratch_shapes=[proj, proj, pltpu.VMEM((GLA_HEADS, GLA_DV, GLA_DK), F32)],
        compiler_params=_cparams(("arbitrary",)),
        name="gla",
    )(xb, xb, w, wg, bg, nw)


def _hgrn_kernel(x0_ref, xn_ref, w_ref, lb_ref, nw_ref, o_ref, cur_ref, nxt_ref, st_ref, *, steps_per_seq):
    between_stages, finish_step = _begin_mixer_step(x0_ref, xn_ref, w_ref, cur_ref, nxt_ref, st_ref, steps_per_seq)
    lb = lb_ref[...]
    for rows in _chunk_rows(cur_ref.shape[0]):
        f = lb + (1.0 - lb) * _sigmoid_rel(cur_ref[rows, HGRN_F:HGRN_F + HGRN_KEY])
        qr = cur_ref[rows, HGRN_Q:HGRN_Q + HGRN_KEY]
        outs = _gated_chunks(qr * _sigmoid(qr), 1.0 - f, cur_ref[rows, HGRN_I:HGRN_I + BRANCH_WIDTH],
                             jnp.log(f), st_ref, between_stages)
        for hd, o in enumerate(outs):
            vs = slice(hd * HGRN_DV, (hd + 1) * HGRN_DV)
            o = o * _sigmoid(cur_ref[rows, HGRN_GATE + hd * HGRN_DV:HGRN_GATE + (hd + 1) * HGRN_DV])
            o = o * lax.rsqrt(jnp.mean(o * o, axis=-1, keepdims=True) + RMS_EPS) * nw_ref[...]
            o_ref[rows, vs] = o.astype(o_ref.dtype)
    finish_step()


def _hgrn(xb, w, lb, nw, seq, layer):
    rows, steps, x0, xn, wspec, out, proj = _mixer_specs(xb, w, seq, layer)
    vec = lambda i: (layer, 0, 0)
    return pl.pallas_call(
        functools.partial(_hgrn_kernel, steps_per_seq=seq // rows),
        grid=(steps,),
        in_specs=[x0, xn, wspec,
                  pl.BlockSpec((None, 1, HGRN_KEY), vec),
                  pl.BlockSpec((None, 1, HGRN_DV), vec)],
        out_specs=out,
        out_shape=jax.ShapeDtypeStruct((xb.shape[0], BRANCH_WIDTH), BF16),
        scratch_shapes=[proj, proj, pltpu.VMEM((HGRN_HEADS, HGRN_DV, HGRN_EXPAND), F32)],
        compiler_params=_cparams(("arbitrary",)),
        name="hgrn",
    )(xb, xb, w, lb, nw)


def _zero_first_step(o_ref):
    @pl.when(pl.program_id(1) == 0)
    def _():
        o_ref[...] = jnp.zeros_like(o_ref)


def _norm_last_step(x_ref, lg_ref, lb_ref, o_ref, ob_ref):
    @pl.when(pl.program_id(1) == pl.num_programs(1) - 1)
    def _():
        xn = _layer_norm(DN_ALPHA * x_ref[...] + o_ref[...], lg_ref[...], lb_ref[...])
        o_ref[...] = xn
        ob_ref[...] = xn.astype(BF16)


def _merge_kernel(ya_ref, yb_ref, yc_ref, ga_ref, gb_ref, gc_ref, wup_ref, o_ref):
    merged = (_sigmoid(ga_ref[...].astype(F32)) * _dot(ya_ref[...], wup_ref[0])
              + _sigmoid(gb_ref[...].astype(F32)) * _dot(yb_ref[...], wup_ref[1])
              + _sigmoid(gc_ref[...].astype(F32)) * _dot(yc_ref[...], wup_ref[2]))
    o_ref[...] = merged.astype(o_ref.dtype)


def _merge(ya, yb, yc, gates, wup, layer):
    n = ya.shape[0]
    tm = min(MERGE_TM, n)
    tn = MERGE_TILE

    def gate(branch):
        off = branch * D_MODEL // tn
        return pl.BlockSpec((tm, tn), lambda i, j: (i, off + j))

    ybs = pl.BlockSpec((tm, BRANCH_WIDTH), lambda i, j: (i, 0))
    return pl.pallas_call(
        _merge_kernel,
        grid=(n // tm, D_MODEL // tn),
        in_specs=[ybs, ybs, ybs, gate(0), gate(1), gate(2),
                  pl.BlockSpec((None, N_BRANCH, BRANCH_WIDTH, tn), lambda i, j: (layer, 0, 0, j))],
        out_specs=pl.BlockSpec((tm, tn), lambda i, j: (i, j)),
        out_shape=jax.ShapeDtypeStruct((n, D_MODEL), BF16),
        compiler_params=_cparams(("parallel", "arbitrary")),
        name="merge",
    )(ya, yb, yc, gates, gates, gates, wup)


def _mixer_out_kernel(m_ref, wout_ref, x_ref, lg_ref, lb_ref, o_ref, ob_ref):
    half = m_ref.shape[0] // 2
    halves = [slice(0, half), slice(half, 2 * half)]
    mix = [_dot(m_ref[rows, :], wout_ref[...]) for rows in halves]
    for rows, mx in zip(halves, mix):
        xn = _layer_norm(DN_ALPHA * x_ref[rows, :] + mx, lg_ref[...], lb_ref[...])
        o_ref[rows, :] = xn
        ob_ref[rows, :] = xn.astype(BF16)


def _mixer_out(merged, wout, x, ln_g, ln_b, layer):
    n = x.shape[0]
    tm = min(ROW_TILE, n)
    row = pl.BlockSpec((tm, D_MODEL), lambda i: (i, 0))
    vec = pl.BlockSpec((None, 1, D_MODEL), lambda i: (layer, 0, 0))
    return pl.pallas_call(
        _mixer_out_kernel,
        grid=(n // tm,),
        in_specs=[row, pl.BlockSpec((None, D_MODEL, D_MODEL), lambda i: (layer, 0, 0)), row, vec, vec],
        out_specs=[row, row],
        out_shape=[jax.ShapeDtypeStruct((n, D_MODEL), F32), jax.ShapeDtypeStruct((n, D_MODEL), BF16)],
        compiler_params=_cparams(("parallel",)),
        name="mixer_out",
    )(merged, wout, x, ln_g, ln_b)


def _mlp_kernel(xb_ref, w1_ref, w2_ref, x_ref, lg_ref, lb_ref, o_ref, ob_ref):
    _zero_first_step(o_ref)
    hid = jnp.maximum(_dot(xb_ref[...], w1_ref[...]), 0.0)
    o_ref[...] += _dot((hid * hid).astype(BF16), w2_ref[...])
    _norm_last_step(x_ref, lg_ref, lb_ref, o_ref, ob_ref)


def _mlp(xb, w1, w2, x, ln_g, ln_b, layer):
    n = x.shape[0]
    tm = min(ROW_TILE, n)
    th = HID_TILE
    row = pl.BlockSpec((tm, D_MODEL), lambda i, k: (i, 0))
    vec = pl.BlockSpec((None, 1, D_MODEL), lambda i, k: (layer, 0, 0))
    return pl.pallas_call(
        _mlp_kernel,
        grid=(n // tm, MLP_HIDDEN // th),
        in_specs=[row,
                  pl.BlockSpec((None, D_MODEL, th), lambda i, k: (layer, 0, k)),
                  pl.BlockSpec((None, th, D_MODEL), lambda i, k: (layer, k, 0)),
                  row, vec, vec],
        out_specs=[row, row],
        out_shape=[jax.ShapeDtypeStruct((n, D_MODEL), F32), jax.ShapeDtypeStruct((n, D_MODEL), BF16)],
        compiler_params=_cparams(("parallel", "arbitrary")),
        name="mlp",
    )(xb, w1, w2, x, ln_g, ln_b)


def _split_w_in(w):
    wt = jnp.swapaxes(w, 1, 2)

    def rows(lo, hi):
        return wt[:, lo:hi].astype(BF16)

    pad = jnp.zeros((w.shape[0], LANES - GLA_GATE_RANK, w.shape[1]), BF16)
    w_gla = jnp.concatenate([rows(0, W_IN_GLOW), rows(W_IN_GATEB, W_IN_HGRN), rows(W_IN_GLOW, W_IN_GATEB), pad],
                            axis=1)
    return w_gla, rows(W_IN_HGRN, W_IN_MERGE), rows(W_IN_MERGE, w.shape[2])


def _row(v):
    return v.astype(F32)[:, None, :]


def kernel(x, w_in, s5_lam_re, s5_lam_im, s5_log_dt, s5_b_re, s5_b_im, s5_c_re, s5_c_im, s5_d, s5_w_glu, s5_b_glu, gla_w_gate, gla_b_gate, gla_norm_w, hgrn_lb_logits, hgrn_norm_w, w_up, w_out, ln1_g, ln1_b, ln2_g, ln2_b, w_mlp_in, w_mlp_out):
    bsz, seq, d = x.shape
    n = bsz * seq
    depth = w_in.shape[0]
    p = jax.nn.softmax(hgrn_lb_logits.astype(F32), axis=0)
    lb = _row(jnp.cumsum(p, axis=0) - p[0])
    w_gla_b, w_hgrn_b, w_gates_b = _split_w_in(w_in)
    s5w = jax.vmap(functools.partial(_s5_weights, folds_per_seq=seq // S5_FOLD))(
        s5_lam_re, s5_lam_im, s5_log_dt, s5_b_re, s5_b_im, s5_c_re, s5_c_im, s5_d)
    w_glu_b = s5_w_glu.astype(BF16)
    b_glu = _row(s5_b_glu)
    wg = jnp.concatenate([gla_w_gate, jnp.zeros((depth, LANES - GLA_GATE_RANK, GLA_KEY), gla_w_gate.dtype)],
                         axis=1).astype(BF16)
    bg, gla_nw, hgrn_nw = _row(gla_b_gate), _row(gla_norm_w), _row(hgrn_norm_w)
    w_up_b, w_out_b = w_up.astype(BF16), w_out.astype(BF16)
    w1_b, w2_b = w_mlp_in.astype(BF16), w_mlp_out.astype(BF16)
    g1, b1, g2, b2 = _row(ln1_g), _row(ln1_b), _row(ln2_g), _row(ln2_b)

    xf = x.reshape(n, d).astype(F32)
    xb = xf.astype(BF16)
    for l in range(depth):
        gates = _proj(xb, w_gates_b, l, BF16)
        u, yb = _gla(xb, w_gla_b, wg, bg, gla_nw, seq, l)
        ya = _glu(_s5_scan(u, bsz, seq, s5w, l), w_glu_b, b_glu, l)
        yc = _hgrn(xb, w_hgrn_b, lb, hgrn_nw, seq, l)
        xf, xb = _mixer_out(_merge(ya, yb, yc, gates, w_up_b, l), w_out_b, xf, g1, b1, l)
        xf, xb = _mlp(xb, w1_b, w2_b, xf, g2, b2, l)
    return xf.reshape(bsz, seq, d).astype(x.dtype)
```

```python
import functools
import math

import jax
import jax.numpy as jnp
from jax import lax
from jax.experimental import pallas as pl
from jax.experimental.pallas import tpu as pltpu

F32 = jnp.float32
BF16 = jnp.bfloat16

D_MODEL = 2048
DEPTH = 4
N_BRANCH = 3
BRANCH_WIDTH = D_MODEL // 2
S5_GROUP = 16
S5_GROUPS = BRANCH_WIDTH // S5_GROUP
S5_STATE = 64
S5_MIN_DECAY = 1e-4
GLA_HEADS = 4
GLA_DV = BRANCH_WIDTH // GLA_HEADS
GLA_DK = GLA_DV // 2
GLA_KEY = GLA_HEADS * GLA_DK
GLA_GATE_RANK = 16
GLA_GATE_TAU = 16.0
HGRN_EXPAND = 128
HGRN_HEADS = BRANCH_WIDTH // HGRN_EXPAND
HGRN_DV = BRANCH_WIDTH // HGRN_HEADS
HGRN_KEY = HGRN_HEADS * HGRN_EXPAND
MLP_HIDDEN = 4 * D_MODEL
DN_ALPHA = (2 * DEPTH) ** 0.25
LN_EPS = 1e-5
RMS_EPS = 1e-6
LOG2E = 1.4426950408889634

LANES = 128
SUBLANES = 8
VMEM_LIMIT_BYTES = 56 * 1024 * 1024

W_IN_GLOW = BRANCH_WIDTH + 2 * GLA_KEY + BRANCH_WIDTH
W_IN_GATEB = W_IN_GLOW + GLA_GATE_RANK
W_IN_HGRN = W_IN_GATEB + BRANCH_WIDTH
W_IN_MERGE = W_IN_HGRN + 2 * HGRN_KEY + 2 * BRANCH_WIDTH
GLA_U = 0
GLA_Q = GLA_U + BRANCH_WIDTH
GLA_K = GLA_Q + GLA_KEY
GLA_V = GLA_K + GLA_KEY
GLA_GATE = GLA_V + BRANCH_WIDTH
GLA_GLOW = GLA_GATE + BRANCH_WIDTH
GLA_COLS = GLA_GLOW + LANES
HGRN_Q = 0
HGRN_F = HGRN_Q + HGRN_KEY
HGRN_I = HGRN_F + HGRN_KEY
HGRN_GATE = HGRN_I + BRANCH_WIDTH
HGRN_COLS = HGRN_GATE + BRANCH_WIDTH
MERGE_WIDTH = N_BRANCH * D_MODEL

S5_FOLD = SUBLANES
S5_TILE_GROUPS = LANES // S5_GROUP
S5_TILES = S5_GROUPS // S5_TILE_GROUPS
S5_TILE_STATE = S5_TILE_GROUPS * S5_STATE
S5_ROWS = S5_FOLD * LANES
S5_INTRA_PIECE = 256
ATT_CHUNK = 128
ATT_STEP_ROWS = 512
PROJ_PIECE = 256
PROJ_PIECE_ROWS = 512
ROW_TILE = 512
PROJ_TM = 2048
PROJ_TN = 1024
MERGE_TM = 1024
MERGE_TILE = 1024
HID_TILE = 1024


def _cparams(sem):
    return pltpu.CompilerParams(dimension_semantics=sem, vmem_limit_bytes=VMEM_LIMIT_BYTES)


def _dot(a, b):
    return jnp.dot(a, b, preferred_element_type=F32)


def _dot_nt(a, b):
    return lax.dot_general(a, b, (((1,), (1,)), ((), ())), preferred_element_type=F32)


def _dot_tn(a, b):
    return lax.dot_general(a, b, (((0,), (0,)), ((), ())), preferred_element_type=F32)


def _sigmoid(x):
    return 0.5 * jnp.tanh(0.5 * x) + 0.5


def _sigmoid_rel(x):
    return 1.0 / (1.0 + jnp.exp2(x * -LOG2E))


def _layer_norm(v, g, b):
    mu = jnp.mean(v, axis=-1, keepdims=True)
    xc = v - mu
    var = jnp.mean(xc * xc, axis=-1, keepdims=True)
    return xc * lax.rsqrt(var + LN_EPS) * g + b


def _proj_kernel(x_ref, w_ref, o_ref):
    o_ref[...] = _dot_nt(x_ref[...], w_ref[...]).astype(o_ref.dtype)


def _proj(xb, wt, layer, out_dtype):
    n, k = xb.shape
    width = wt.shape[1]
    tm = min(PROJ_TM, n)
    return pl.pallas_call(
        _proj_kernel,
        grid=(n // tm, width // PROJ_TN),
        in_specs=[pl.BlockSpec((tm, k), lambda i, j: (i, 0)),
                  pl.BlockSpec((None, PROJ_TN, k), lambda i, j: (layer, j, 0))],
        out_specs=pl.BlockSpec((tm, PROJ_TN), lambda i, j: (i, j)),
        out_shape=jax.ShapeDtypeStruct((n, width), out_dtype),
        compiler_params=_cparams(("parallel", "arbitrary")),
        name="proj",
    )(xb, wt)


def _s5_expand(ky_ref, kv_ref, kq_ref, my_ref, mv_ref, mq_ref):
    kk = lax.broadcasted_iota(jnp.int32, (LANES, S5_ROWS), 0)
    cc = lax.broadcasted_iota(jnp.int32, (LANES, S5_ROWS), 1)
    e_tc = jnp.where(((kk >> 4) == (cc >> 7)) & ((kk & 15) == (cc & 15)), 1.0, 0.0).astype(BF16)
    e_rp = jnp.where(((kk >> 6) == (cc >> 9)) & ((kk & 63) == (cc & 63)), 1.0, 0.0).astype(BF16)
    rr = lax.broadcasted_iota(jnp.int32, (S5_ROWS, S5_ROWS), 0)
    cc = lax.broadcasted_iota(jnp.int32, (S5_ROWS, S5_ROWS), 1)
    g_jgd, g_rgp = (rr >> 4) & 7, (rr >> 6) & 7
    h_thc, h_rhp = (cc >> 4) & 7, (cc >> 6) & 7
    my_ref[...] = jnp.where(g_jgd == h_thc, _dot(ky_ref[...], e_tc), 0.0).astype(BF16)
    mv_ref[...] = jnp.where(g_jgd == h_rhp, _dot(kv_ref[...], e_rp), 0.0).astype(BF16)
    mq_ref[...] = jnp.where(g_rgp == h_thc, _dot(kq_ref[...], e_tc), 0.0).astype(BF16)


def _scan_steps(s_re, s_im, pos, apr_ref, api_ref, first_step, length, before_step=lambda: None):
    step, dist = first_step, 1
    while dist < length:
        before_step()
        ok = pos >= dist
        sh_re = jnp.where(ok, pltpu.roll(s_re, dist, axis=0), 0.0)
        sh_im = jnp.where(ok, pltpu.roll(s_im, dist, axis=0), 0.0)
        ar = apr_ref[step:step + 1, :]
        ai = api_ref[step:step + 1, :]
        s_re, s_im = s_re + ar * sh_re - ai * sh_im, s_im + ar * sh_im + ai * sh_re
        step += 1
        dist *= 2
    return s_re, s_im


def _s5_kernel(u_ref, ky_ref, kv_ref, kq_ref, d_ref, apr_ref, api_ref, pwr_ref, pwi_ref, y_ref,
               my_ref, mv_ref, mq_ref, sre_ref, sim_ref, pre_ref, pim_ref):
    @pl.when(pl.program_id(1) == 0)
    def _():
        _s5_expand(ky_ref, kv_ref, kq_ref, my_ref, mv_ref, mq_ref)

    folds = u_ref.shape[0] // S5_FOLD
    runs = folds // SUBLANES
    us = [u_ref[pl.ds(t, folds, stride=S5_FOLD), :] for t in range(S5_FOLD)]
    ucat = jnp.concatenate([u.astype(BF16) for u in us], axis=1)
    r = _dot(ucat, mv_ref[...])
    y_parts = []

    def intra_piece():
        cs = slice(len(y_parts) * S5_INTRA_PIECE, (len(y_parts) + 1) * S5_INTRA_PIECE)
        y_parts.append(_dot(ucat, my_ref[:, cs]))

    fold = lax.broadcasted_iota(jnp.int32, (folds, S5_TILE_STATE), 0)
    s_re, s_im = _scan_steps(r[:, :S5_TILE_STATE], r[:, S5_TILE_STATE:], fold & (SUBLANES - 1),
                             apr_ref, api_ref, 0, SUBLANES, intra_piece)
    while len(y_parts) * S5_INTRA_PIECE < S5_ROWS:
        intra_piece()
    y_intra = jnp.concatenate(y_parts, axis=1)
    lane_tiles = S5_TILE_STATE // LANES

    def put(ref, j, val):
        for lt in range(lane_tiles):
            ref[lt, pl.ds(j, runs, stride=SUBLANES), :] = val[:, lt * LANES:(lt + 1) * LANES]

    def get(ref, j):
        return jnp.concatenate([ref[lt, pl.ds(j, runs, stride=SUBLANES), :] for lt in range(lane_tiles)], axis=1)

    for lt in range(lane_tiles):
        sre_ref[lt] = s_re[:, lt * LANES:(lt + 1) * LANES]
        sim_ref[lt] = s_im[:, lt * LANES:(lt + 1) * LANES]
    last = SUBLANES - 1
    run = lax.broadcasted_iota(jnp.int32, (runs, S5_TILE_STATE), 0)
    e_re, e_im = _scan_steps(get(sre_ref, last), get(sim_ref, last), run, apr_ref, api_ref,
                             int(math.log2(SUBLANES)), runs)
    c_re = jnp.where(run >= 1, pltpu.roll(e_re, 1, axis=0), 0.0)
    c_im = jnp.where(run >= 1, pltpu.roll(e_im, 1, axis=0), 0.0)
    put(pre_ref, 0, c_re)
    put(pim_ref, 0, c_im)
    for j in range(last):
        l_re, l_im = get(sre_ref, j), get(sim_ref, j)
        br, bi = pwr_ref[j:j + 1, :], pwi_ref[j:j + 1, :]
        put(pre_ref, j + 1, l_re + br * c_re - bi * c_im)
        put(pim_ref, j + 1, l_im + br * c_im + bi * c_re)
    prev = jnp.concatenate([pre_ref[lt] for lt in range(lane_tiles)]
                           + [pim_ref[lt] for lt in range(lane_tiles)], axis=1).astype(BF16)
    y = y_intra + _dot(prev, mq_ref[...])
    for t in range(S5_FOLD):
        y_ref[pl.ds(t, folds, stride=S5_FOLD), :] = y[:, t * LANES:(t + 1) * LANES] + d_ref[...] * us[t]


def _s5_scan(h, bsz, seq, s5w, layer):
    k_y, k_v, k_q, d_t, ap_re, ap_im, pw_re, pw_im = s5w
    nsteps = ap_re.shape[2]
    folds = seq // S5_FOLD
    tile = lambda j, b: (layer, j, 0, 0)
    compact = pl.BlockSpec((None, None, S5_ROWS, LANES), tile)
    steps = pl.BlockSpec((None, None, nsteps, S5_TILE_STATE), tile)
    powers = pl.BlockSpec((None, None, SUBLANES, S5_TILE_STATE), tile)
    return pl.pallas_call(
        _s5_kernel,
        grid=(S5_TILES, bsz),
        in_specs=[pl.BlockSpec((seq, LANES), lambda j, b: (b, j)),
                  compact, compact, compact,
                  pl.BlockSpec((None, None, 1, LANES), tile),
                  steps, steps, powers, powers],
        out_specs=pl.BlockSpec((seq, LANES), lambda j, b: (b, j)),
        out_shape=jax.ShapeDtypeStruct((bsz * seq, BRANCH_WIDTH), F32),
        scratch_shapes=[pltpu.VMEM((S5_ROWS, S5_ROWS), BF16)] * 3
        + [pltpu.VMEM((S5_TILE_STATE // LANES, folds, LANES), F32)] * 4,
        compiler_params=_cparams(("parallel", "arbitrary")),
        name="s5_scan",
    )(h, k_y, k_v, k_q, d_t, ap_re, ap_im, pw_re, pw_im)


def _s5_weights(lam_re, lam_im, log_dt, b_re, b_im, c_re, c_im, d_skip, folds_per_seq):
    t_fold, g_n, p_n, c_n, tg = S5_FOLD, S5_GROUPS, S5_STATE, S5_GROUP, S5_TILE_GROUPS
    tiles = g_n // tg
    lr = jnp.minimum(lam_re.astype(F32), -S5_MIN_DECAY)
    li = lam_im.astype(F32)
    dt = jnp.exp(log_dt.astype(F32))[:, None]
    mag = jnp.exp(lr * dt)
    abar_re, abar_im = mag * jnp.cos(li * dt), mag * jnp.sin(li * dt)
    den = lr * lr + li * li
    fac_re = ((abar_re - 1.0) * lr + abar_im * li) / den
    fac_im = (abar_im * lr - (abar_re - 1.0) * li) / den
    br, bi = b_re.astype(F32), b_im.astype(F32)
    bbar_re = fac_re[..., None] * br - fac_im[..., None] * bi
    bbar_im = fac_re[..., None] * bi + fac_im[..., None] * br

    def apow(n):
        nn = jnp.asarray(n, F32)[:, None, None]
        m = jnp.exp(nn * (lr * dt)[None])
        return m * jnp.cos(nn * (li * dt)[None]), m * jnp.sin(nn * (li * dt)[None])

    pr, pi = apow(jnp.arange(t_fold + 1))
    cr, ci = c_re.astype(F32), c_im.astype(F32)
    ca_re = cr[None] * pr[:, :, None, :] - ci[None] * pi[:, :, None, :]
    ca_im = cr[None] * pi[:, :, None, :] + ci[None] * pr[:, :, None, :]
    k_tau = jnp.sum(ca_re[:t_fold, :, :, :, None] * bbar_re[None, :, None, :, :]
                    - ca_im[:t_fold, :, :, :, None] * bbar_im[None, :, None, :, :], axis=3)
    jj = jnp.arange(t_fold)[:, None]
    tt = jnp.arange(t_fold)[None, :]
    lag = jnp.clip(tt - jj, 0, t_fold - 1)
    toe = jnp.where((tt >= jj)[:, :, None, None, None], k_tau[lag], 0.0)
    k_y = (toe.reshape(t_fold, t_fold, tiles, tg, c_n, c_n).transpose(2, 0, 3, 5, 1, 4)
           .reshape(tiles, S5_ROWS, LANES))
    rev = t_fold - 1 - jnp.arange(t_fold)
    prj, pij = pr[rev], pi[rev]
    pv = jnp.stack([prj[..., None] * bbar_re[None] - pij[..., None] * bbar_im[None],
                    prj[..., None] * bbar_im[None] + pij[..., None] * bbar_re[None]])
    k_v = (pv.reshape(2, t_fold, tiles, tg, p_n, c_n).transpose(2, 1, 3, 5, 0, 4)
           .reshape(tiles, S5_ROWS, LANES))
    qa = jnp.stack([ca_re[1:], -ca_im[1:]])
    k_q = (qa.reshape(2, t_fold, tiles, tg, c_n, p_n).transpose(2, 0, 3, 5, 1, 4)
           .reshape(tiles, S5_ROWS, LANES))
    d_t = d_skip.astype(F32).reshape(tiles, 1, LANES)
    nsteps = max(int(math.log2(folds_per_seq)), 1)
    sr, si = apow(t_fold * (2 ** jnp.arange(nsteps)))
    ap_re = sr.reshape(nsteps, tiles, tg * p_n).transpose(1, 0, 2)
    ap_im = si.reshape(nsteps, tiles, tg * p_n).transpose(1, 0, 2)
    wr, wi = apow(t_fold * (1 + jnp.arange(SUBLANES)))
    pw_re = wr.reshape(SUBLANES, tiles, tg * p_n).transpose(1, 0, 2)
    pw_im = wi.reshape(SUBLANES, tiles, tg * p_n).transpose(1, 0, 2)
    return k_y.astype(BF16), k_v.astype(BF16), k_q.astype(BF16), d_t, ap_re, ap_im, pw_re, pw_im


def _glu_kernel(y_ref, w_ref, b_ref, o_ref):
    y = y_ref[...]
    z = 0.5 * y * (1.0 + jnp.tanh(math.sqrt(2.0 / math.pi) * (y + 0.044715 * (y * y * y))))
    a = _dot(z.astype(BF16), w_ref[...]) + b_ref[...]
    o_ref[...] = (z * _sigmoid(a)).astype(o_ref.dtype)


def _glu(y, w, b, layer):
    n, width = y.shape
    tm = min(ROW_TILE, n)
    return pl.pallas_call(
        _glu_kernel,
        grid=(n // tm,),
        in_specs=[pl.BlockSpec((tm, width), lambda i: (i, 0)),
                  pl.BlockSpec((None, width, width), lambda i: (layer, 0, 0)),
                  pl.BlockSpec((None, 1, width), lambda i: (layer, 0, 0))],
        out_specs=pl.BlockSpec((tm, width), lambda i: (i, 0)),
        out_shape=jax.ShapeDtypeStruct((n, width), BF16),
        compiler_params=_cparams(("parallel",)),
        name="s5_glu",
    )(y, w, b)


ATT_STAGES = 9


def _gated_chunks(q, k, v, g, st_ref, between_stages):
    r, kw = q.shape
    c = min(ATT_CHUNK, r)
    chunks = [slice(i * c, (i + 1) * c) for i in range(r // c)]
    heads = kw // LANES
    dv = v.shape[1] // heads
    row = lax.broadcasted_iota(jnp.int32, (r, kw), 0)
    ri = lax.broadcasted_iota(jnp.int32, (c, c), 0)
    ci = lax.broadcasted_iota(jnp.int32, (c, c), 1)
    xr = jnp.where(ri > ci, ri ^ ci, 0)
    ksl = [slice(hd * LANES, (hd + 1) * LANES) for hd in range(heads)]
    vsl = [slice(hd * dv, (hd + 1) * dv) for hd in range(heads)]
    pairs = [(rows, s) for rows in chunks for s in ksl]
    tril = jnp.where(ri >= ci, 1.0, 0.0).astype(BF16)
    g_hi = g.astype(BF16)
    g_rest = g - g_hi.astype(F32)
    g_mid = g_rest.astype(BF16)
    g_lo = (g_rest - g_mid.astype(F32)).astype(BF16)
    gc = jnp.concatenate([_dot(tril, g_hi[rows]) + _dot(tril, g_mid[rows]) + _dot(tril, g_lo[rows])
                          for rows in chunks], axis=0)

    def block_row(x, blk, pos):
        x3 = x.reshape(r // blk, blk, kw)
        return jnp.broadcast_to(x3[:, pos:pos + 1, :], x3.shape).reshape(r, kw)

    between_stages()
    qb, kb = q.astype(BF16), k.astype(BF16)
    a = [jnp.where(ri == ci, _dot_nt(qb[rows, s], kb[rows, s]), 0.0) for rows, s in pairs]
    end = gc
    half = 1
    while half < c:
        between_stages()
        if half < SUBLANES:
            first = (row & half) == 0
            ref = jnp.where(first, end, pltpu.roll(end, half, axis=0))
            if 2 * half < SUBLANES:
                end = jnp.where(first, pltpu.roll(end, r - half, axis=0), end)
        else:
            ref = block_row(gc, 2 * half, half - 1)
        e = jnp.exp2(jnp.abs(gc - ref) * -LOG2E).astype(BF16)
        eq, ek = e * qb, e * kb
        level = xr >= half
        a = [jnp.where(level, _dot_nt(eq[rows, s], ek[rows, s]), a_p) for (rows, s), a_p in zip(pairs, a)]
        half *= 2
    between_stages()
    last = block_row(gc, c, c - 1)
    vb = v.astype(BF16)
    qg = (q * jnp.exp(gc)).astype(BF16)
    kd = (k * jnp.exp(last - gc)).astype(BF16)
    outs = []
    for hd in range(heads):
        st = st_ref[hd]
        o_hd = []
        for ic, rows in enumerate(chunks):
            o_hd.append(_dot(a[ic * heads + hd].astype(BF16), vb[rows, vsl[hd]])
                        + _dot_nt(qg[rows, ksl[hd]], st.astype(BF16)))
            keep = jnp.exp(last[rows.start:rows.start + 1, ksl[hd]])
            st = st * keep + _dot_tn(vb[rows, vsl[hd]], kd[rows, ksl[hd]])
        st_ref[hd] = st
        outs.append(jnp.concatenate(o_hd, axis=0))
    return outs


def _chunk_rows(step_rows):
    c = min(ATT_CHUNK, step_rows)
    return [slice(i * c, (i + 1) * c) for i in range(step_rows // c)]


def _begin_mixer_step(x0_ref, xn_ref, w_ref, cur_ref, nxt_ref, st_ref, steps_per_seq):
    step = pl.program_id(0)

    @pl.when(step == 0)
    def _():
        cur_ref[...] = _dot_nt(x0_ref[...], w_ref[...])

    @pl.when(step % steps_per_seq == 0)
    def _():
        st_ref[...] = jnp.zeros_like(st_ref)

    rows, cols = cur_ref.shape
    pieces = [(slice(r0, r0 + PROJ_PIECE_ROWS), slice(c0, min(c0 + PROJ_PIECE, cols)))
              for c0 in range(0, cols, PROJ_PIECE) for r0 in range(0, rows, PROJ_PIECE_ROWS)]
    slots = ATT_STAGES * len(_chunk_rows(rows))
    calls = [0]

    def between_stages():
        slot = calls[0]
        calls[0] += 1
        for p, (rs, cs) in enumerate(pieces):
            if (p * slots) // len(pieces) == slot:
                nxt_ref[rs, cs] = _dot_nt(xn_ref[rs, :], w_ref[cs, :])

    def finish_step():
        assert calls[0] == slots, "every piece of nxt_ref must have been written"
        cur_ref[...] = nxt_ref[...]

    return between_stages, finish_step


def _gla_kernel(x0_ref, xn_ref, w_ref, wg_ref, bg_ref, nw_ref, u_ref, o_ref, cur_ref, nxt_ref, st_ref, *,
                steps_per_seq):
    between_stages, finish_step = _begin_mixer_step(x0_ref, xn_ref, w_ref, cur_ref, nxt_ref, st_ref, steps_per_seq)
    u_ref[...] = cur_ref[:, GLA_U:GLA_U + BRANCH_WIDTH]
    for rows in _chunk_rows(cur_ref.shape[0]):
        glow = cur_ref[rows, GLA_GLOW:GLA_GLOW + LANES].astype(BF16)
        z = _dot(glow, wg_ref[...]) + bg_ref[...]
        g = (jnp.minimum(z, 0.0) - jnp.log1p(jnp.exp(-jnp.abs(z)))) / GLA_GATE_TAU
        outs = _gated_chunks(cur_ref[rows, GLA_Q:GLA_Q + GLA_KEY] * (GLA_DK ** -0.5),
                             cur_ref[rows, GLA_K:GLA_K + GLA_KEY],
                             cur_ref[rows, GLA_V:GLA_V + BRANCH_WIDTH], g, st_ref, between_stages)
        for hd, o in enumerate(outs):
            vs = slice(hd * GLA_DV, (hd + 1) * GLA_DV)
            o = o * lax.rsqrt(jnp.mean(o * o, axis=-1, keepdims=True) + RMS_EPS) * nw_ref[...]
            gate = cur_ref[rows, GLA_GATE + hd * GLA_DV:GLA_GATE + (hd + 1) * GLA_DV]
            o_ref[rows, vs] = (o * (gate * _sigmoid(gate))).astype(o_ref.dtype)
    finish_step()


def _mixer_specs(xb, w, seq, layer):
    n, d = xb.shape
    rows = min(ATT_STEP_ROWS, seq)
    steps = n // rows
    x0 = pl.BlockSpec((rows, d), lambda i: (0, 0))
    xn = pl.BlockSpec((rows, d), lambda i: (jnp.minimum(i + 1, steps - 1), 0))
    wspec = pl.BlockSpec((None,) + w.shape[1:], lambda i: (layer, 0, 0), pipeline_mode=pl.Buffered(1))
    out = pl.BlockSpec((rows, BRANCH_WIDTH), lambda i: (i, 0))
    proj = pltpu.VMEM((rows, w.shape[1]), F32)
    return rows, steps, x0, xn, wspec, out, proj


def _gla(xb, w, wg, bg, nw, seq, layer):
    rows, steps, x0, xn, wspec, out, proj = _mixer_specs(xb, w, seq, layer)
    vec = lambda i: (layer, 0, 0)
    return pl.pallas_call(
        functools.partial(_gla_kernel, steps_per_seq=seq // rows),
        grid=(steps,),
        in_specs=[x0, xn, wspec,
                  pl.BlockSpec((None, LANES, GLA_KEY), vec),
                  pl.BlockSpec((None, 1, GLA_KEY), vec),
                  pl.BlockSpec((None, 1, GLA_DV), vec)],
        out_specs=[out, out],
        out_shape=[jax.ShapeDtypeStruct((xb.shape[0], BRANCH_WIDTH), F32),
                   jax.ShapeDtypeStruct((xb.shape[0], BRANCH_WIDTH), BF16)],
        scratch_shapes=[proj, proj, pltpu.VMEM((GLA_HEADS, GLA_DV, GLA_DK), F32)],
        compiler_params=_cparams(("arbitrary",)),
        name="gla",
    )(xb, xb, w, wg, bg, nw)


def _hgrn_kernel(x0_ref, xn_ref, w_ref, lb_ref, nw_ref, o_ref, cur_ref, nxt_ref, st_ref, *, steps_per_seq):
    between_stages, finish_step = _begin_mixer_step(x0_ref, xn_ref, w_ref, cur_ref, nxt_ref, st_ref, steps_per_seq)
    lb = lb_ref[...]
    for rows in _chunk_rows(cur_ref.shape[0]):
        f = lb + (1.0 - lb) * _sigmoid_rel(cur_ref[rows, HGRN_F:HGRN_F + HGRN_KEY])
        qr = cur_ref[rows, HGRN_Q:HGRN_Q + HGRN_KEY]
        outs = _gated_chunks(qr * _sigmoid(qr), 1.0 - f, cur_ref[rows, HGRN_I:HGRN_I + BRANCH_WIDTH],
                             jnp.log(f), st_ref, between_stages)
        for hd, o in enumerate(outs):
            vs = slice(hd * HGRN_DV, (hd + 1) * HGRN_DV)
            o = o * _sigmoid(cur_ref[rows, HGRN_GATE + hd * HGRN_DV:HGRN_GATE + (hd + 1) * HGRN_DV])
            o = o * lax.rsqrt(jnp.mean(o * o, axis=-1, keepdims=True) + RMS_EPS) * nw_ref[...]
            o_ref[rows, vs] = o.astype(o_ref.dtype)
    finish_step()


def _hgrn(xb, w, lb, nw, seq, layer):
    rows, steps, x0, xn, wspec, out, proj = _mixer_specs(xb, w, seq, layer)
    vec = lambda i: (layer, 0, 0)
    return pl.pallas_call(
        functools.partial(_hgrn_kernel, steps_per_seq=seq // rows),
        grid=(steps,),
        in_specs=[x0, xn, wspec,
                  pl.BlockSpec((None, 1, HGRN_KEY), vec),
                  pl.BlockSpec((None, 1, HGRN_DV), vec)],
        out_specs=out,
        out_shape=jax.ShapeDtypeStruct((xb.shape[0], BRANCH_WIDTH), BF16),
        scratch_shapes=[proj, proj, pltpu.VMEM((HGRN_HEADS, HGRN_DV, HGRN_EXPAND), F32)],
        compiler_params=_cparams(("arbitrary",)),
        name="hgrn",
    )(xb, xb, w, lb, nw)


def _zero_first_step(o_ref):
    @pl.when(pl.program_id(1) == 0)
    def _():
        o_ref[...] = jnp.zeros_like(o_ref)


def _merge_kernel(ya_ref, yb_ref, yc_ref, ga_ref, gb_ref, gc_ref, wup_ref, o_ref):
    merged = (_sigmoid(ga_ref[...].astype(F32)) * _dot(ya_ref[...], wup_ref[0])
              + _sigmoid(gb_ref[...].astype(F32)) * _dot(yb_ref[...], wup_ref[1])
              + _sigmoid(gc_ref[...].astype(F32)) * _dot(yc_ref[...], wup_ref[2]))
    o_ref[...] = merged.astype(o_ref.dtype)


def _merge(ya, yb, yc, gates, wup, layer):
    n = ya.shape[0]
    tm = min(MERGE_TM, n)
    tn = MERGE_TILE

    def gate(branch):
        off = branch * D_MODEL // tn
        return pl.BlockSpec((tm, tn), lambda i, j: (i, off + j))

    ybs = pl.BlockSpec((tm, BRANCH_WIDTH), lambda i, j: (i, 0))
    return pl.pallas_call(
        _merge_kernel,
        grid=(n // tm, D_MODEL // tn),
        in_specs=[ybs, ybs, ybs, gate(0), gate(1), gate(2),
                  pl.BlockSpec((None, N_BRANCH, BRANCH_WIDTH, tn), lambda i, j: (layer, 0, 0, j))],
        out_specs=pl.BlockSpec((tm, tn), lambda i, j: (i, j)),
        out_shape=jax.ShapeDtypeStruct((n, D_MODEL), BF16),
        compiler_params=_cparams(("parallel", "arbitrary")),
        name="merge",
    )(ya, yb, yc, gates, gates, gates, wup)


def _mixer_out_kernel(m_ref, wout_ref, x_ref, lg_ref, lb_ref, o_ref, ob_ref):
    half = m_ref.shape[0] // 2
    halves = [slice(0, half), slice(half, 2 * half)]
    mix = [_dot(m_ref[rows, :], wout_ref[...]) for rows in halves]
    for rows, mx in zip(halves, mix):
        xn = _layer_norm(DN_ALPHA * x_ref[rows, :] + mx, lg_ref[...], lb_ref[...])
        o_ref[rows, :] = xn
        ob_ref[rows, :] = xn.astype(BF16)


def _mixer_out(merged, wout, x, ln_g, ln_b, layer):
    n = x.shape[0]
    tm = min(ROW_TILE, n)
    row = pl.BlockSpec((tm, D_MODEL), lambda i: (i, 0))
    vec = pl.BlockSpec((None, 1, D_MODEL), lambda i: (layer, 0, 0))
    return pl.pallas_call(
        _mixer_out_kernel,
        grid=(n // tm,),
        in_specs=[row, pl.BlockSpec((None, D_MODEL, D_MODEL), lambda i: (layer, 0, 0)), row, vec, vec],
        out_specs=[row, row],
        out_shape=[jax.ShapeDtypeStruct((n, D_MODEL), F32), jax.ShapeDtypeStruct((n, D_MODEL), BF16)],
        compiler_params=_cparams(("parallel",)),
        name="mixer_out",
    )(merged, wout, x, ln_g, ln_b)


def _mlp_kernel(xb_ref, w1_ref, w2_ref, x_ref, lg_ref, lb_ref, o_ref, ob_ref):
    _zero_first_step(o_ref)
    last = pl.num_programs(1) - 1

    def partial_out(rows):
        hid = jnp.maximum(_dot(xb_ref[rows, :], w1_ref[...]), 0.0)
        return _dot((hid * hid).astype(BF16), w2_ref[...])

    @pl.when(pl.program_id(1) < last)
    def _():
        o_ref[...] += partial_out(slice(None))

    @pl.when(pl.program_id(1) == last)
    def _():
        half = o_ref.shape[0] // 2
        for rows in (slice(0, half), slice(half, 2 * half)):
            xn = _layer_norm(DN_ALPHA * x_ref[rows, :] + (o_ref[rows, :] + partial_out(rows)),
                             lg_ref[...], lb_ref[...])
            o_ref[rows, :] = xn
            ob_ref[rows, :] = xn.astype(BF16)


def _mlp(xb, w1, w2, x, ln_g, ln_b, layer):
    n = x.shape[0]
    tm = min(ROW_TILE, n)
    th = HID_TILE
    row = pl.BlockSpec((tm, D_MODEL), lambda i, k: (i, 0))
    vec = pl.BlockSpec((None, 1, D_MODEL), lambda i, k: (layer, 0, 0))
    return pl.pallas_call(
        _mlp_kernel,
        grid=(n // tm, MLP_HIDDEN // th),
        in_specs=[row,
                  pl.BlockSpec((None, D_MODEL, th), lambda i, k: (layer, 0, k)),
                  pl.BlockSpec((None, th, D_MODEL), lambda i, k: (layer, k, 0)),
                  row, vec, vec],
        out_specs=[row, row],
        out_shape=[jax.ShapeDtypeStruct((n, D_MODEL), F32), jax.ShapeDtypeStruct((n, D_MODEL), BF16)],
        compiler_params=_cparams(("parallel", "arbitrary")),
        name="mlp",
    )(xb, w1, w2, x, ln_g, ln_b)


def _split_w_in(w):
    wt = jnp.swapaxes(w, 1, 2)

    def rows(lo, hi):
        return wt[:, lo:hi].astype(BF16)

    pad = jnp.zeros((w.shape[0], LANES - GLA_GATE_RANK, w.shape[1]), BF16)
    w_gla = jnp.concatenate([rows(0, W_IN_GLOW), rows(W_IN_GATEB, W_IN_HGRN), rows(W_IN_GLOW, W_IN_GATEB), pad],
                            axis=1)
    return w_gla, rows(W_IN_HGRN, W_IN_MERGE), rows(W_IN_MERGE, w.shape[2])


def _row(v):
    return v.astype(F32)[:, None, :]


def kernel(x, w_in, s5_lam_re, s5_lam_im, s5_log_dt, s5_b_re, s5_b_im, s5_c_re, s5_c_im, s5_d, s5_w_glu, s5_b_glu, gla_w_gate, gla_b_gate, gla_norm_w, hgrn_lb_logits, hgrn_norm_w, w_up, w_out, ln1_g, ln1_b, ln2_g, ln2_b, w_mlp_in, w_mlp_out):
    bsz, seq, d = x.shape
    n = bsz * seq
    depth = w_in.shape[0]
    p = jax.nn.softmax(hgrn_lb_logits.astype(F32), axis=0)
    lb = _row(jnp.cumsum(p, axis=0) - p[0])
    w_gla_b, w_hgrn_b, w_gates_b = _split_w_in(w_in)
    s5w = jax.vmap(functools.partial(_s5_weights, folds_per_seq=seq // S5_FOLD))(
        s5_lam_re, s5_lam_im, s5_log_dt, s5_b_re, s5_b_im, s5_c_re, s5_c_im, s5_d)
    w_glu_b = s5_w_glu.astype(BF16)
    b_glu = _row(s5_b_glu)
    wg = jnp.concatenate([gla_w_gate, jnp.zeros((depth, LANES - GLA_GATE_RANK, GLA_KEY), gla_w_gate.dtype)],
                         axis=1).astype(BF16)
    bg, gla_nw, hgrn_nw = _row(gla_b_gate), _row(gla_norm_w), _row(hgrn_norm_w)
    w_up_b, w_out_b = w_up.astype(BF16), w_out.astype(BF16)
    w1_b, w2_b = w_mlp_in.astype(BF16), w_mlp_out.astype(BF16)
    g1, b1, g2, b2 = _row(ln1_g), _row(ln1_b), _row(ln2_g), _row(ln2_b)

    xf = x.reshape(n, d).astype(F32)
    xb = xf.astype(BF16)
    for l in range(depth):
        gates = _proj(xb, w_gates_b, l, BF16)
        u, yb = _gla(xb, w_gla_b, wg, bg, gla_nw, seq, l)
        ya = _glu(_s5_scan(u, bsz, seq, s5w, l), w_glu_b, b_glu, l)
        yc = _hgrn(xb, w_hgrn_b, lb, hgrn_nw, seq, l)
        xf, xb = _mixer_out(_merge(ya, yb, yc, gates, w_up_b, l), w_out_b, xf, g1, b1, l)
        xf, xb = _mlp(xb, w1_b, w2_b, xf, g2, b2, l)
    return xf.reshape(bsz, seq, d).astype(x.dtype)
```
